```python
import math
import jax, jax.numpy as jnp
from jax import lax
import numpy as np

D_MODEL = 1024
BATCH = 16
SEQ = 2048
DEPTH = 1

GRID_W = 64
CTX_LEN = 256
EPS = 1e-6
N_MOD = 9
D_FF = 2816
HEAD_DIM = 64
N_Q_HEADS = 8
N_KV_HEADS = 2
GROUP = N_Q_HEADS // N_KV_HEADS
ATT_WIDTH = N_Q_HEADS * HEAD_DIM
KV_WIDTH = N_KV_HEADS * HEAD_DIM
Q_BLOCK = 128
ROPE_THETA = 10000.0
ROPE_PAIRS = HEAD_DIM // 4
ATT_SCALE = HEAD_DIM ** -0.5
HG_HEADS = 4
HG_DK = 128
HG_DV = 128
HG_WIDTH = HG_HEADS * HG_DK
HG_VWIDTH = HG_HEADS * HG_DV
HG_SCALE = HG_DK ** -0.5
CHUNK = 64
IN_SPLITS = (ATT_WIDTH, KV_WIDTH, KV_WIDTH, HG_WIDTH, HG_VWIDTH, HG_WIDTH, HG_WIDTH, HG_VWIDTH, D_MODEL, D_MODEL)
D_IN = sum(IN_SPLITS)

kernel_name = "hybrid_gqa_hgrn2_macaron_prefix_block"


def rms_norm(x, gain):
    xf = x.astype(jnp.float32)
    y = xf * lax.rsqrt(jnp.mean(xf * xf, axis=-1, keepdims=True) + EPS)
    return (y * gain.astype(jnp.float32)).astype(x.dtype)


def modulation(cvec, w_mod_l, b_mod_l):
    m = jax.nn.silu(cvec) @ w_mod_l + b_mod_l
    return jnp.split(m[:, None, :], N_MOD, axis=-1)


def pre(h, g_pre, shift, scale):
    return rms_norm(h, g_pre) * (1.0 + scale) + shift


def post(y, g_post, gate):
    return gate * rms_norm(y, g_post)


def swiglu(u, w_gate, w_up, w_down):
    return (jax.nn.silu(u @ w_gate) * (u @ w_up)) @ w_down


def heads(a, d):
    return a.reshape(*a.shape[:-1], -1, d)


def split_in(p):
    out, start = [], 0
    for size in IN_SPLITS:
        out.append(p[..., start:start + size])
        start += size
    return out


def axial_rope(rows):
    row = jnp.repeat(jnp.arange(rows, dtype=jnp.float32), GRID_W)
    col = jnp.tile(jnp.arange(GRID_W, dtype=jnp.float32), rows)
    inv_freq = ROPE_THETA ** (-jnp.arange(ROPE_PAIRS, dtype=jnp.float32) / ROPE_PAIRS)
    ang_r = row[:, None] * inv_freq
    ang_c = col[:, None] * inv_freq
    ang = jnp.concatenate([ang_r, ang_r, ang_c, ang_c], axis=-1)
    return jnp.cos(ang), jnp.sin(ang)


def apply_rope(x, cos, sin):
    xa = x.reshape(*x.shape[:-1], 2, 2, ROPE_PAIRS)
    rot = jnp.stack([-xa[..., 1, :], xa[..., 0, :]], axis=-2).reshape(x.shape)
    return x * cos[:, None, :].astype(x.dtype) + rot * sin[:, None, :].astype(x.dtype)


def gqa_softmax(q, k, v):
    s = jnp.einsum('bqkgd,bskd->bkgqs', q, k).astype(jnp.float32) * ATT_SCALE
    p = jax.nn.softmax(s, axis=-1).astype(v.dtype)
    return jnp.einsum('bkgqs,bskd->bqkgd', p, v)


def latent_attention(q, k, v, k_ctx, v_ctx):
    bsz, t = q.shape[:2]
    k_all = jnp.concatenate([k, k_ctx], axis=1)
    v_all = jnp.concatenate([v, v_ctx], axis=1)
    nb = t // Q_BLOCK
    qb = jnp.moveaxis(q.reshape(bsz, nb, Q_BLOCK, N_KV_HEADS, GROUP, HEAD_DIM), 1, 0)
    o = lax.map(lambda qi: gqa_softmax(qi, k_all, v_all), qb)
    return jnp.moveaxis(o, 0, 1).reshape(bsz, t, ATT_WIDTH)


def att_kv(p, k_gain, cos=None, sin=None):
    k = rms_norm(heads(p[1], HEAD_DIM), k_gain)
    if cos is not None:
        k = apply_rope(k, cos, sin)
    return k, heads(p[2], HEAD_DIM)


def att_q(p, q_gain, cos=None, sin=None):
    q = rms_norm(heads(p[0], HEAD_DIM), q_gain)
    if cos is not None:
        q = apply_rope(q, cos, sin)
    return q.reshape(*q.shape[:2], N_KV_HEADS, GROUP, HEAD_DIM)


def forget_gate(f_raw, lb):
    lbh = lb.reshape(HG_HEADS, HG_DK)
    f = lbh + (1.0 - lbh) * jax.nn.sigmoid(heads(f_raw, HG_DK).astype(jnp.float32))
    return 1.0 - f, jnp.log(f)


def chunk_states(k, v, g, s0):
    bsz, t, h, dk = k.shape
    n = t // CHUNK
    kc = k.reshape(bsz, n, CHUNK, h, dk)
    vc = v.reshape(bsz, n, CHUNK, h, -1)
    b = jnp.cumsum(g.reshape(bsz, n, CHUNK, h, dk), axis=2)
    b_last = b[:, :, -1]
    ds = jnp.einsum('bnshk,bnshv->bnhkv', kc * jnp.exp(b_last[:, :, None] - b), vc)

    def step(s, inp):
        ds_n, dec_n = inp
        return dec_n[..., None] * s + ds_n, s

    s_fin, s_prev = lax.scan(step, s0, (jnp.moveaxis(ds, 1, 0), jnp.moveaxis(jnp.exp(b_last), 1, 0)))
    return jnp.moveaxis(s_prev, 0, 1), s_fin, b


def chunk_output(q, k, v, b, s_prev):
    bsz, t, h, dk = q.shape
    n = t // CHUNK
    qc = q.reshape(bsz, n, CHUNK, h, dk)
    kc = k.reshape(bsz, n, CHUNK, h, dk)
    vc = v.reshape(bsz, n, CHUNK, h, -1)
    b_mid = b[:, :, CHUNK // 2 - 1:CHUNK // 2]
    s = jnp.einsum('bnchk,bnshk->bnhcs', qc * jnp.exp(b - b_mid), kc * jnp.exp(b_mid - b))
    s = jnp.where(jnp.tril(jnp.ones((CHUNK, CHUNK), dtype=bool)), s, 0.0)
    o = (jnp.einsum('bnhcs,bnshv->bnchv', s, vc)
         + jnp.einsum('bnchk,bnhkv->bnchv', qc * jnp.exp(b), s_prev))
    return o.reshape(bsz, t, h, -1)


def hgrn_direction(q_x, v_x, f_x, q_c, v_c, f_c, lb):
    k_x, g_x = forget_gate(f_x, lb)
    k_c, g_c = forget_gate(f_c, lb)
    s0 = jnp.zeros((v_c.shape[0], HG_HEADS, HG_DK, HG_DV), jnp.float32)
    sp_c, sf_c, b_c = chunk_states(k_c, v_c, g_c, s0)
    sp_x, _, b_x = chunk_states(k_x, v_x, g_x, sf_c)
    o_x = chunk_output(q_x, k_x, v_x, b_x, sp_x)
    o_c = chunk_output(q_c, k_c, v_c, b_c, sp_c) if q_c is not None else None
    return o_x, o_c


def hg_query(p):
    return jax.nn.silu(heads(p[3], HG_DK).astype(jnp.float32)) * HG_SCALE


def hg_out(o, g_raw, hg_gain, dtype):
    o = rms_norm(o, hg_gain) * jax.nn.silu(heads(g_raw, HG_DV).astype(jnp.float32))
    return o.reshape(*o.shape[:2], HG_VWIDTH).astype(dtype)


def rev(a):
    return jnp.flip(a, axis=1)


def hgrn_branch(px, pc, lb_l, hg_gain, with_ctx):
    q_x = hg_query(px)
    v_x = heads(px[4], HG_DV).astype(jnp.float32)
    v_c = heads(pc[4], HG_DV).astype(jnp.float32)
    q_c = hg_query(pc) if with_ctx else None
    of_x, of_c = hgrn_direction(q_x, v_x, px[5], q_c, v_c, pc[5], lb_l[0])
    ob_x, ob_c = hgrn_direction(rev(q_x), rev(v_x), rev(px[6]),
                                rev(q_c) if with_ctx else None, rev(v_c), rev(pc[6]), lb_l[1])
    o_x = hg_out(of_x + rev(ob_x), px[7], hg_gain, px[7].dtype)
    o_c = hg_out(of_c + rev(ob_c), pc[7], hg_gain, pc[7].dtype) if with_ctx else None
    return o_x, o_c


def merge(p, o_att, o_hg, w_att_out_l, w_hg_out_l, w_o_l):
    y = jax.nn.sigmoid(p[8]) * (o_att @ w_att_out_l) + jax.nn.sigmoid(p[9]) * (o_hg @ w_hg_out_l)
    return y @ w_o_l


def token_mixer(ux, uc, cos, sin, w_in_l, q_gain, k_gain, lb_l, hg_gain,
                w_att_out_l, w_hg_out_l, w_o_l, with_ctx):
    px = split_in(ux @ w_in_l)
    pc = split_in(uc @ w_in_l)
    k_x, v_x = att_kv(px, k_gain, cos, sin)
    k_c, v_c = att_kv(pc, k_gain)
    o_att_x = latent_attention(att_q(px, q_gain, cos, sin), k_x, v_x, k_c, v_c)
    o_hg_x, o_hg_c = hgrn_branch(px, pc, lb_l, hg_gain, with_ctx)
    y_x = merge(px, o_att_x, o_hg_x, w_att_out_l, w_hg_out_l, w_o_l)
    y_c = None
    if with_ctx:
        q_c = att_q(pc, q_gain)
        o_att_c = gqa_softmax(q_c, k_c, v_c).reshape(*uc.shape[:2], ATT_WIDTH)
        y_c = merge(pc, o_att_c, o_hg_c, w_att_out_l, w_hg_out_l, w_o_l)
    return y_x, y_c


def setup_inputs(seed: int = 0) -> dict:
    key = jax.random.key(seed)
    ks = jax.random.split(key, 20)
    nrm = lambda k, shape, s: jax.random.normal(k, shape, jnp.float32) * s
    gain = lambda k, shape: 1.0 + 0.02 * jax.random.normal(k, shape, jnp.float32)
    return {
        "x": nrm(ks[0], (BATCH, SEQ, D_MODEL), 1.0),
        "c": nrm(ks[1], (BATCH, D_MODEL), 1.0),
        "ctx": nrm(ks[2], (BATCH, CTX_LEN, D_MODEL), 1.0),
        "c_ctx": nrm(ks[3], (D_MODEL,), 1.0),
        "w_mod": nrm(ks[4], (DEPTH, D_MODEL, N_MOD * D_MODEL), 0.5 * D_MODEL ** -0.5),
        "b_mod": nrm(ks[5], (DEPTH, N_MOD * D_MODEL), 0.02),
        "norm_pre": gain(ks[6], (DEPTH, 3, D_MODEL)),
        "norm_post": gain(ks[7], (DEPTH, 3, D_MODEL)),
        "ffn_w_gate": nrm(ks[8], (DEPTH, 2, D_MODEL, D_FF), D_MODEL ** -0.5),
        "ffn_w_up": nrm(ks[9], (DEPTH, 2, D_MODEL, D_FF), D_MODEL ** -0.5),
        "ffn_w_down": nrm(ks[10], (DEPTH, 2, D_FF, D_MODEL), D_FF ** -0.5),
        "w_in": nrm(ks[11], (DEPTH, D_MODEL, D_IN), D_MODEL ** -0.5),
        "q_norm": gain(ks[12], (DEPTH, HEAD_DIM)),
        "k_norm": gain(ks[13], (DEPTH, HEAD_DIM)),
        "hg_lower_bound": nrm(ks[14], (2, DEPTH + 1, HG_WIDTH), 0.1),
        "hg_norm": gain(ks[15], (DEPTH, HG_DV)),
        "w_att_out": nrm(ks[16], (DEPTH, ATT_WIDTH, D_MODEL), ATT_WIDTH ** -0.5),
        "w_hg_out": nrm(ks[17], (DEPTH, HG_VWIDTH, D_MODEL), HG_VWIDTH ** -0.5),
        "w_o": nrm(ks[18], (DEPTH, D_MODEL, D_MODEL), D_MODEL ** -0.5),
    }


def reference(x, c, ctx, c_ctx, w_mod, b_mod, norm_pre, norm_post, ffn_w_gate, ffn_w_up, ffn_w_down,
              w_in, q_norm, k_norm, hg_lower_bound, hg_norm, w_att_out, w_hg_out, w_o):
    t = x.shape[1]
    rows = t // GRID_W
    cos, sin = axial_rope(rows)
    lb_all = jnp.cumsum(jax.nn.softmax(hg_lower_bound.astype(jnp.float32), axis=1), axis=1)
    h_c = ctx
    for l in range(DEPTH):
        last = l == DEPTH - 1
        mx = modulation(c, w_mod[l], b_mod[l])
        mc = modulation(c_ctx[None, :], w_mod[l], b_mod[l])
        ffn1 = lambda h, m: 0.5 * post(swiglu(pre(h, norm_pre[l, 0], m[0], m[1]), ffn_w_gate[l, 0],
                                              ffn_w_up[l, 0], ffn_w_down[l, 0]), norm_post[l, 0], m[2])
        x = x + ffn1(x, mx)
        h_c = h_c + ffn1(h_c, mc)
        ux = pre(x, norm_pre[l, 1], mx[3], mx[4])
        uc = pre(h_c, norm_pre[l, 1], mc[3], mc[4])
        y_x, y_c = token_mixer(ux, uc, cos, sin, w_in[l], q_norm[l], k_norm[l], lb_all[:, l], hg_norm[l],
                               w_att_out[l], w_hg_out[l], w_o[l], with_ctx=not last)
        x = x + post(y_x, norm_post[l, 1], mx[5])
        ffn2 = lambda h, m: 0.5 * post(swiglu(pre(h, norm_pre[l, 2], m[6], m[7]), ffn_w_gate[l, 1],
                                              ffn_w_up[l, 1], ffn_w_down[l, 1]), norm_post[l, 2], m[8])
        if not last:
            h_c = h_c + post(y_c, norm_post[l, 1], mc[5])
            h_c = h_c + ffn2(h_c, mc)
        x = x + ffn2(x, mx)
    return x
```

```python
import functools

import jax
import jax.numpy as jnp
from jax import lax
from jax.experimental import pallas as pl
from jax.experimental.pallas import tpu as pltpu

EPS = 1e-6
N_MOD = 9
GRID_W = 64
ROPE_THETA = 10000.0
HEAD_DIM = 64
N_Q_HEADS = 8
N_KV_HEADS = 2
GROUP = N_Q_HEADS // N_KV_HEADS
ATT_WIDTH = N_Q_HEADS * HEAD_DIM
KV_WIDTH = N_KV_HEADS * HEAD_DIM
ROPE_PAIRS = HEAD_DIM // 4
ATT_SCALE = HEAD_DIM ** -0.5
HG_HEADS = 4
HG_DK = 128
HG_DV = 128
HG_WIDTH = HG_HEADS * HG_DK
HG_SCALE = HG_DK ** -0.5
CHUNK = 64
LANES = 128
VMEM_LIMIT = 56 * 1024 * 1024

BF16 = jnp.bfloat16
F32 = jnp.float32

NT_DIMS = (((1,), (1,)), ((), ()))
TN_DIMS = (((0,), (0,)), ((), ()))


def _dot(a, b):
    return jnp.dot(a, b, preferred_element_type=F32)


def _rms(x, gain):
    return x * lax.rsqrt(jnp.mean(x * x, axis=-1, keepdims=True) + EPS) * gain


def _sigmoid(x):
    return 1.0 / (1.0 + jnp.exp(-x))


def _silu(x):
    return x * _sigmoid(x)


def _params(n_grid):
    return pltpu.CompilerParams(dimension_semantics=("parallel",) * n_grid, vmem_limit_bytes=VMEM_LIMIT)


def _const_spec(shape):
    nd = len(shape)
    return pl.BlockSpec(shape, lambda *_: (0,) * nd, pipeline_mode=pl.Buffered(1))


def _mod_kernel(c_ref, w_ref, b_ref, o_ref):
    a = _silu(c_ref[...]).astype(BF16)
    o_ref[...] = _dot(a, w_ref[...].astype(BF16)) + b_ref[...]


def _modulation(cvec, w_mod, b_mod, tn=1024):
    rows, d = cvec.shape
    n = w_mod.shape[1]
    return pl.pallas_call(
        _mod_kernel,
        grid=(n // tn,),
        in_specs=[pl.BlockSpec((rows, d), lambda j: (0, 0)),
                  pl.BlockSpec((d, tn), lambda j: (0, j)),
                  pl.BlockSpec((1, tn), lambda j: (0, j))],
        out_specs=pl.BlockSpec((rows, tn), lambda j: (0, j)),
        out_shape=jax.ShapeDtypeStruct((rows, n), F32),
        compiler_params=_params(1),
        name="mod",
    )(cvec, w_mod, b_mod[None, :])


def _ffn_kernel(x_ref, m_ref, gpre_ref, gpost_ref, wg_ref, wu_ref, wd_ref, o_ref, *, mod0, d):
    x = x_ref[...]
    shift = m_ref[:, (mod0 + 0) * d:(mod0 + 1) * d]
    scale = m_ref[:, (mod0 + 1) * d:(mod0 + 2) * d]
    gate = m_ref[:, (mod0 + 2) * d:(mod0 + 3) * d]
    u = (_rms(x, gpre_ref[...]) * (1.0 + scale) + shift).astype(BF16)
    h = (_silu(_dot(u, wg_ref[...])) * _dot(u, wu_ref[...])).astype(BF16)
    y = _dot(h, wd_ref[...])
    o_ref[...] = x + 0.5 * (gate * _rms(y, gpost_ref[...]))


def _ffn(x, mods, mod_row, mod0, g_pre, g_post, wg, wu, wd, tm):
    bsz, t, d = x.shape
    f = wg.shape[1]
    nm = mods.shape[-1]
    return pl.pallas_call(
        functools.partial(_ffn_kernel, mod0=mod0, d=d),
        grid=(bsz, t // tm),
        in_specs=[pl.BlockSpec((None, tm, d), lambda b, i: (b, i, 0)),
                  pl.BlockSpec((None, 1, nm), lambda b, i: (mod_row(b), 0, 0)),
                  _const_spec((1, d)), _const_spec((1, d)),
                  _const_spec((d, f)), _const_spec((d, f)), _const_spec((f, d))],
        out_specs=pl.BlockSpec((None, tm, d), lambda b, i: (b, i, 0)),
        out_shape=jax.ShapeDtypeStruct((bsz, t, d), F32),
        compiler_params=_params(2),
        name="ffn",
    )(x, mods, g_pre[None, :], g_post[None, :], wg, wu, wd)


def _head_rms64(z, gain):
    lane = lax.broadcasted_iota(jnp.int32, (1, LANES), 1)
    first = lane < HEAD_DIM
    sq = z * z
    lo = jnp.sum(jnp.where(first, sq, 0.0), axis=-1, keepdims=True)
    hi = jnp.sum(jnp.where(first, 0.0, sq), axis=-1, keepdims=True)
    ms = jnp.where(first, lo, hi) * (1.0 / HEAD_DIM)
    return z * lax.rsqrt(ms + EPS) * gain


def _rope128(z, cos, sin_lo, sin_hi):
    q = ROPE_PAIRS
    return z * cos + pltpu.roll(z, LANES - q, 1) * sin_lo + pltpu.roll(z, q, 1) * sin_hi


def _inproj_latent_kernel(x_ref, m_ref, gpre_ref, w_ref, qg_ref, kg_ref, cos_ref, slo_ref, shi_ref,
                          q_ref, k_ref, v_ref, hq_ref, hv_ref, ff_ref, og_ref, mg_ref, *, d):
    x = x_ref[...]
    shift = m_ref[:, 3 * d:4 * d]
    scale = m_ref[:, 4 * d:5 * d]
    u = (_rms(x, gpre_ref[...]) * (1.0 + scale) + shift).astype(BF16)
    cos, slo, shi = cos_ref[...], slo_ref[...], shi_ref[...]

    def proj(lo, hi):
        return _dot(u, w_ref[:, lo:hi])

    c0 = 0
    pq = proj(c0, c0 + ATT_WIDTH)
    qs = []
    for j in range(ATT_WIDTH // LANES):
        z = _head_rms64(pq[:, j * LANES:(j + 1) * LANES], qg_ref[...])
        qs.append(_rope128(z, cos, slo, shi) * ATT_SCALE)
    q_ref[...] = jnp.concatenate(qs, axis=-1).astype(BF16)
    c0 += ATT_WIDTH
    pk = proj(c0, c0 + KV_WIDTH)
    k_ref[...] = _rope128(_head_rms64(pk, kg_ref[...]), cos, slo, shi).astype(BF16)
    c0 += KV_WIDTH
    v_ref[...] = proj(c0, c0 + KV_WIDTH).astype(BF16)
    c0 += KV_WIDTH
    hq_ref[...] = _silu(proj(c0, c0 + HG_WIDTH)) * HG_SCALE
    c0 += HG_WIDTH
    hv_ref[...] = proj(c0, c0 + HG_WIDTH).astype(BF16)
    c0 += HG_WIDTH
    ff_ref[...] = proj(c0, c0 + 2 * HG_WIDTH)
    c0 += 2 * HG_WIDTH
    og_ref[...] = proj(c0, c0 + HG_WIDTH)
    c0 += HG_WIDTH
    mg_ref[...] = proj(c0, c0 + 2 * d)


def _inproj_latent(x, mods, g_pre, w_in, q_gain2, k_gain2, cos, slo, shi, tm):
    bsz, t, d = x.shape
    nm = mods.shape[-1]
    n_in = w_in.shape[1]

    def tile(width, dtype):
        return (pl.BlockSpec((None, tm, width), lambda b, i: (b, i, 0)),
                jax.ShapeDtypeStruct((bsz, t, width), dtype))

    outs = [tile(ATT_WIDTH, BF16), tile(KV_WIDTH, BF16), tile(KV_WIDTH, BF16), tile(HG_WIDTH, F32),
            tile(HG_WIDTH, BF16), tile(2 * HG_WIDTH, F32), tile(HG_WIDTH, F32), tile(2 * d, F32)]
    rope_spec = pl.BlockSpec((tm, LANES), lambda b, i: (i, 0))
    return pl.pallas_call(
        functools.partial(_inproj_latent_kernel, d=d),
        grid=(bsz, t // tm),
        in_specs=[pl.BlockSpec((None, tm, d), lambda b, i: (b, i, 0)),
                  pl.BlockSpec((None, 1, nm), lambda b, i: (b, 0, 0)),
                  _const_spec((1, d)), _const_spec((d, n_in)),
                  _const_spec((1, LANES)), _const_spec((1, LANES)),
                  rope_spec, rope_spec, rope_spec],
        out_specs=[o[0] for o in outs],
        out_shape=[o[1] for o in outs],
        compiler_params=_params(2),
        name="inproj_latent",
    )(x, mods, g_pre[None, :], w_in, q_gain2, k_gain2, cos, slo, shi)


def _inproj_ctx_kernel(x_ref, m_ref, gpre_ref, wkv_ref, wh_ref, kg_ref, k_ref, v_ref, hv_ref, ff_ref, *, d):
    x = x_ref[...]
    shift = m_ref[:, 3 * d:4 * d]
    scale = m_ref[:, 4 * d:5 * d]
    u = (_rms(x, gpre_ref[...]) * (1.0 + scale) + shift).astype(BF16)
    k_ref[...] = _head_rms64(_dot(u, wkv_ref[:, :KV_WIDTH]), kg_ref[...]).astype(BF16)
    v_ref[...] = _dot(u, wkv_ref[:, KV_WIDTH:]).astype(BF16)
    hv_ref[...] = _dot(u, wh_ref[:, :HG_WIDTH]).astype(BF16)
    ff_ref[...] = _dot(u, wh_ref[:, HG_WIDTH:])


def _inproj_ctx(x, mods, ctx_row, g_pre, w_kv, w_h, k_gain2, tm):
    bsz, t, d = x.shape
    nm = mods.shape[-1]

    def tile(width, dtype):
        return (pl.BlockSpec((None, tm, width), lambda b, i: (b, i, 0)),
                jax.ShapeDtypeStruct((bsz, t, width), dtype))

    outs = [tile(KV_WIDTH, BF16), tile(KV_WIDTH, BF16), tile(HG_WIDTH, BF16), tile(2 * HG_WIDTH, F32)]
    return pl.pallas_call(
        functools.partial(_inproj_ctx_kernel, d=d),
        grid=(bsz, t // tm),
        in_specs=[pl.BlockSpec((None, tm, d), lambda b, i: (b, i, 0)),
                  pl.BlockSpec((None, 1, nm), lambda b, i: (ctx_row, 0, 0)),
                  _const_spec((1, d)), _const_spec(w_kv.shape), _const_spec(w_h.shape),
                  _const_spec((1, LANES))],
        out_specs=[o[0] for o in outs],
        out_shape=[o[1] for o in outs],
        compiler_params=_params(2),
        name="inproj_ctx",
    )(x, mods, g_pre[None, :], w_kv, w_h, k_gain2)


def _attn_kernel(q_ref, kt_ref, v_ref, o_ref):
    outs = []
    for kv in range(N_KV_HEADS):
        kt = kt_ref[kv]
        v = v_ref[kv]
        for g in range(GROUP):
            h = kv * GROUP + g
            q = q_ref[:, h * HEAD_DIM:(h + 1) * HEAD_DIM]
            s = _dot(q, kt)
            p = jnp.exp(s - jnp.max(s, axis=-1, keepdims=True))
            l = jnp.sum(p, axis=-1, keepdims=True)
            outs.append(_dot(p.astype(BF16), v) / l)
    o_ref[...] = jnp.concatenate(outs, axis=-1).astype(BF16)


def _attention(q, kt, v, tq):
    bsz, t, _ = q.shape
    s = kt.shape[-1]
    return pl.pallas_call(
        _attn_kernel,
        grid=(bsz, t // tq),
        in_specs=[pl.BlockSpec((None, tq, ATT_WIDTH), lambda b, i: (b, i, 0)),
                  pl.BlockSpec((None, N_KV_HEADS, HEAD_DIM, s), lambda b, i: (b, 0, 0, 0)),
                  pl.BlockSpec((None, N_KV_HEADS, s, HEAD_DIM), lambda b, i: (b, 0, 0, 0))],
        out_specs=pl.BlockSpec((None, tq, ATT_WIDTH), lambda b, i: (b, i, 0)),
        out_shape=jax.ShapeDtypeStruct((bsz, t, ATT_WIDTH), BF16),
        compiler_params=_params(2),
        name="attn",
    )(q, kt, v)


def _split3(g):
    g1 = g.astype(BF16)
    r1 = g - g1.astype(F32)
    g2 = r1.astype(BF16)
    g3 = (r1 - g2.astype(F32)).astype(BF16)
    return g1, g2, g3


def _hgrn_chunk(fraw, lb, v, q, st, reverse):
    c = fraw.shape[0]
    f = lb + (1.0 - lb) * _sigmoid(fraw)
    k = 1.0 - f
    g = jnp.log(f)
    row = lax.broadcasted_iota(jnp.int32, (c, c), 0)
    col = lax.broadcasted_iota(jnp.int32, (c, c), 1)
    causal = (col >= row) if reverse else (col <= row)
    tri = jnp.where(causal, 1.0, 0.0).astype(BF16)
    g1, g2, g3 = _split3(g)
    dk = g.shape[1]
    bb = _dot(tri, jnp.concatenate([g1, g2, g3], axis=1))
    b = bb[:, :dk] + bb[:, dk:2 * dk] + bb[:, 2 * dk:]
    last, mid = (0, c // 2) if reverse else (c - 1, c // 2 - 1)
    b_last = b[last:last + 1, :]
    out = None
    if q is not None:
        b_mid = b[mid:mid + 1, :]
        qd = (q * jnp.exp(b - b_mid)).astype(BF16)
        kd = (k * jnp.exp(b_mid - b)).astype(BF16)
        a = lax.dot_general(qd, kd, NT_DIMS, preferred_element_type=F32)
        a = jnp.where(causal, a, 0.0).astype(BF16)
        qe = (q * jnp.exp(b)).astype(BF16)
        out = _dot(a, v) + lax.dot_general(qe, st.astype(BF16), NT_DIMS, preferred_element_type=F32)
    ke = (k * jnp.exp(b_last - b)).astype(BF16)
    ds_t = lax.dot_general(v, ke, TN_DIMS, preferred_element_type=F32)
    return out, st * jnp.exp(b_last) + ds_t


def _hgrn_kernel(lbraw_ref, gain_ref, hq_ref, hv_ref, ff_ref, fb_ref, og_ref, cv_ref, cff_ref, cfb_ref,
                 o_ref, acc_ref, *, n_lat, n_ctx):
    raw = [lbraw_ref[s] for s in range(lbraw_ref.shape[0])]
    top = functools.reduce(jnp.maximum, raw)
    e = [jnp.exp(r - top) for r in raw]
    lbs = e[0] / functools.reduce(jnp.add, e)

    def run(reverse):
        d = 1 if reverse else 0
        lb = lbs[d:d + 1, :]
        c_f, x_f = (cfb_ref, fb_ref) if reverse else (cff_ref, ff_ref)

        def start(i, n):
            j = (n - 1 - i) if reverse else i
            return pl.multiple_of(j * CHUNK, CHUNK)

        def ctx_step(i, st):
            sl = pl.ds(start(i, n_ctx), CHUNK)
            return _hgrn_chunk(c_f[sl, :], lb, cv_ref[sl, :], None, st, reverse)[1]

        def lat_step(i, st):
            sl = pl.ds(start(i, n_lat), CHUNK)
            out, st = _hgrn_chunk(x_f[sl, :], lb, hv_ref[sl, :], hq_ref[sl, :], st, reverse)
            if reverse:
                tot = acc_ref[sl, :] + out
                o_ref[sl, :] = (_rms(tot, gain_ref[...]) * _silu(og_ref[sl, :])).astype(BF16)
            else:
                acc_ref[sl, :] = out
            return st

        st = lax.fori_loop(0, n_ctx, ctx_step, jnp.zeros((HG_DV, HG_DK), F32))
        lax.fori_loop(0, n_lat, lat_step, st)

    run(False)
    run(True)


def _hgrn(lb_raw, hg_gain, hq, hv, ff, og, cv, cff):
    bsz, t, _ = hq.shape
    tc = cv.shape[1]
    slots = lb_raw.shape[0]

    def col(tt, off=0):
        return pl.BlockSpec((None, tt, HG_DK), lambda b, h: (b, 0, h + off))

    return pl.pallas_call(
        functools.partial(_hgrn_kernel, n_lat=t // CHUNK, n_ctx=tc // CHUNK),
        grid=(bsz, HG_HEADS),
        in_specs=[pl.BlockSpec((slots, 2, HG_DK), lambda b, h: (0, 0, h)),
                  pl.BlockSpec((1, HG_DV), lambda b, h: (0, 0)),
                  col(t), col(t), col(t), col(t, HG_HEADS), col(t),
                  col(tc), col(tc), col(tc, HG_HEADS)],
        out_specs=col(t),
        out_shape=jax.ShapeDtypeStruct((bsz, t, HG_WIDTH), BF16),
        scratch_shapes=[pltpu.VMEM((t, HG_DV), F32)],
        compiler_params=_params(2),
        name="hgrn",
    )(lb_raw, hg_gain[None, :], hq, hv, ff, ff, og, cv, cff, cff)


def _merge_kernel(x_ref, m_ref, oa_ref, oh_ref, mg_ref, gpost_ref, wa_ref, wh_ref, wo_ref, o_ref, *, d):
    gate = m_ref[:, 5 * d:6 * d]
    y = (_sigmoid(mg_ref[:, :d]) * _dot(oa_ref[...], wa_ref[...])
         + _sigmoid(mg_ref[:, d:]) * _dot(oh_ref[...], wh_ref[...]))
    z = _dot(y.astype(BF16), wo_ref[...])
    o_ref[...] = x_ref[...] + gate * _rms(z, gpost_ref[...])


def _merge(x, mods, o_att, o_hg, mg, g_post, wa, wh, wo, tm):
    bsz, t, d = x.shape
    nm = mods.shape[-1]

    def tile(width):
        return pl.BlockSpec((None, tm, width), lambda b, i: (b, i, 0))

    return pl.pallas_call(
        functools.partial(_merge_kernel, d=d),
        grid=(bsz, t // tm),
        in_specs=[tile(d), pl.BlockSpec((None, 1, nm), lambda b, i: (b, 0, 0)),
                  tile(ATT_WIDTH), tile(HG_WIDTH), tile(2 * d), _const_spec((1, d)),
                  _const_spec(wa.shape), _const_spec(wh.shape), _const_spec(wo.shape)],
        out_specs=tile(d),
        out_shape=jax.ShapeDtypeStruct((bsz, t, d), F32),
        compiler_params=_params(2),
        name="merge",
    )(x, mods, o_att, o_hg, mg, g_post[None, :], wa, wh, wo)


def _rope_tables(t):
    pos = jnp.arange(t, dtype=jnp.int32)
    row = (pos // GRID_W).astype(F32)
    colp = (pos % GRID_W).astype(F32)
    inv_freq = ROPE_THETA ** (-jnp.arange(ROPE_PAIRS, dtype=F32) / ROPE_PAIRS)
    ang_r = row[:, None] * inv_freq
    ang_c = colp[:, None] * inv_freq
    ang = jnp.concatenate([ang_r, ang_r, ang_c, ang_c], axis=-1)
    cos, sin = jnp.cos(ang), jnp.sin(ang)
    first = (jnp.arange(HEAD_DIM) % (2 * ROPE_PAIRS)) < ROPE_PAIRS
    sin_lo = jnp.where(first, -sin, 0.0)
    sin_hi = jnp.where(first, 0.0, sin)
    two = lambda a: jnp.concatenate([a, a], axis=-1)
    return two(cos), two(sin_lo), two(sin_hi)


def kernel(x, c, ctx, c_ctx, w_mod, b_mod, norm_pre, norm_post, ffn_w_gate, ffn_w_up, ffn_w_down,
           w_in, q_norm, k_norm, hg_lower_bound, hg_norm, w_att_out, w_hg_out, w_o):
    assert w_in.shape[0] == 1, "single-layer block"
    bsz, t, d = x.shape
    tc = ctx.shape[1]
    assert t % GRID_W == 0 and t % CHUNK == 0 and tc % CHUNK == 0
    tm = min(256, t)
    tmc = min(256, tc)

    rows = -(-(bsz + 1) // 8) * 8
    cvec = jnp.concatenate([c, c_ctx[None, :], jnp.zeros((rows - bsz - 1, d), c.dtype)], axis=0)
    mods = _modulation(cvec, w_mod[0], b_mod[0])[:, None, :]
    lat_row = lambda b: b
    ctx_row = lambda b: bsz

    wg, wu, wd = ffn_w_gate[0].astype(BF16), ffn_w_up[0].astype(BF16), ffn_w_down[0].astype(BF16)
    w_in_b = w_in[0].astype(BF16)

    x1 = _ffn(x, mods, lat_row, 0, norm_pre[0, 0], norm_post[0, 0], wg[0], wu[0], wd[0], tm)
    h1 = _ffn(ctx, mods, ctx_row, 0, norm_pre[0, 0], norm_post[0, 0], wg[0], wu[0], wd[0], tmc)

    cos, slo, shi = _rope_tables(t)
    q_gain2 = jnp.concatenate([q_norm[0], q_norm[0]])[None, :]
    k_gain2 = jnp.concatenate([k_norm[0], k_norm[0]])[None, :]
    q, k, v, hq, hv, ff, og, mg = _inproj_latent(x1, mods, norm_pre[0, 1], w_in_b, q_gain2, k_gain2,
                                                  cos, slo, shi, tm)
    kv0 = ATT_WIDTH
    h0 = ATT_WIDTH + 2 * KV_WIDTH + HG_WIDTH
    ck, cv, chv, cff = _inproj_ctx(h1, mods, bsz, norm_pre[0, 1], w_in_b[:, kv0:kv0 + 2 * KV_WIDTH],
                                   w_in_b[:, h0:h0 + 3 * HG_WIDTH], k_gain2, tmc)

    k_all = jnp.concatenate([k, ck], axis=1).reshape(bsz, t + tc, N_KV_HEADS, HEAD_DIM)
    v_all = jnp.concatenate([v, cv], axis=1).reshape(bsz, t + tc, N_KV_HEADS, HEAD_DIM)
    kt = jnp.transpose(k_all, (0, 2, 3, 1))
    vh = jnp.transpose(v_all, (0, 2, 1, 3))
    o_att = _attention(q, kt, vh, tm)

    lb_raw = jnp.transpose(hg_lower_bound.astype(F32), (1, 0, 2))
    o_hg = _hgrn(lb_raw, hg_norm[0], hq, hv, ff, og, chv, cff)

    x2 = _merge(x1, mods, o_att, o_hg, mg, norm_post[0, 1], w_att_out[0].astype(BF16),
                w_hg_out[0].astype(BF16), w_o[0].astype(BF16), tm)
    return _ffn(x2, mods, lat_row, 6, norm_pre[0, 2], norm_post[0, 2], wg[1], wu[1], wd[1], tm)
```

```python
import functools

import jax
import jax.numpy as jnp
from jax import lax
from jax.experimental import pallas as pl
from jax.experimental.pallas import tpu as pltpu

EPS = 1e-6
N_MOD = 9
GRID_W = 64
ROPE_THETA = 10000.0
HEAD_DIM = 64
N_Q_HEADS = 8
N_KV_HEADS = 2
GROUP = N_Q_HEADS // N_KV_HEADS
ATT_WIDTH = N_Q_HEADS * HEAD_DIM
KV_WIDTH = N_KV_HEADS * HEAD_DIM
ROPE_PAIRS = HEAD_DIM // 4
ATT_SCALE = HEAD_DIM ** -0.5
HG_HEADS = 4
HG_DK = 128
HG_DV = 128
HG_WIDTH = HG_HEADS * HG_DK
HG_SCALE = HG_DK ** -0.5
CHUNK = 64
LANES = 128
VMEM_LIMIT = 56 * 1024 * 1024

BF16 = jnp.bfloat16
F32 = jnp.float32

NT_DIMS = (((1,), (1,)), ((), ()))
TN_DIMS = (((0,), (0,)), ((), ()))


def _dot(a, b):
    return jnp.dot(a, b, preferred_element_type=F32)


def _rms(x, gain):
    return x * lax.rsqrt(jnp.mean(x * x, axis=-1, keepdims=True) + EPS) * gain


def _sigmoid(x):
    return 1.0 / (1.0 + jnp.exp(-x))


def _silu(x):
    return x * _sigmoid(x)


def _params(n_grid):
    return pltpu.CompilerParams(dimension_semantics=("parallel",) * n_grid, vmem_limit_bytes=VMEM_LIMIT)


def _const_spec(shape):
    nd = len(shape)
    return pl.BlockSpec(shape, lambda *_: (0,) * nd, pipeline_mode=pl.Buffered(1))


def _mod_kernel(c_ref, w_ref, b_ref, o_ref):
    a = _silu(c_ref[...]).astype(BF16)
    o_ref[...] = _dot(a, w_ref[...].astype(BF16)) + b_ref[...]


def _modulation(cvec, w_mod, b_mod, tn=1024):
    rows, d = cvec.shape
    n = w_mod.shape[1]
    return pl.pallas_call(
        _mod_kernel,
        grid=(n // tn,),
        in_specs=[pl.BlockSpec((rows, d), lambda j: (0, 0)),
                  pl.BlockSpec((d, tn), lambda j: (0, j)),
                  pl.BlockSpec((1, tn), lambda j: (0, j))],
        out_specs=pl.BlockSpec((rows, tn), lambda j: (0, j)),
        out_shape=jax.ShapeDtypeStruct((rows, n), F32),
        compiler_params=_params(1),
        name="mod",
    )(cvec, w_mod, b_mod[None, :])


def _ffn_kernel(x_ref, m_ref, gpre_ref, gpost_ref, wg_ref, wu_ref, wd_ref, o_ref, *, mod0, d):
    x = x_ref[...]
    shift = m_ref[:, (mod0 + 0) * d:(mod0 + 1) * d]
    scale = m_ref[:, (mod0 + 1) * d:(mod0 + 2) * d]
    gate = m_ref[:, (mod0 + 2) * d:(mod0 + 3) * d]
    u = (_rms(x, gpre_ref[...]) * (1.0 + scale) + shift).astype(BF16)
    h = (_silu(_dot(u, wg_ref[...])) * _dot(u, wu_ref[...])).astype(BF16)
    y = _dot(h, wd_ref[...])
    o_ref[...] = x + 0.5 * (gate * _rms(y, gpost_ref[...]))


def _ffn(x, mods, mod_row, mod0, g_pre, g_post, wg, wu, wd, tm):
    bsz, t, d = x.shape
    f = wg.shape[1]
    nm = mods.shape[-1]
    return pl.pallas_call(
        functools.partial(_ffn_kernel, mod0=mod0, d=d),
        grid=(bsz, t // tm),
        in_specs=[pl.BlockSpec((None, tm, d), lambda b, i: (b, i, 0)),
                  pl.BlockSpec((None, 1, nm), lambda b, i: (mod_row(b), 0, 0)),
                  _const_spec((1, d)), _const_spec((1, d)),
                  _const_spec((d, f)), _const_spec((d, f)), _const_spec((f, d))],
        out_specs=pl.BlockSpec((None, tm, d), lambda b, i: (b, i, 0)),
        out_shape=jax.ShapeDtypeStruct((bsz, t, d), F32),
        compiler_params=_params(2),
        name="ffn",
    )(x, mods, g_pre[None, :], g_post[None, :], wg, wu, wd)


def _head_rms64(z, gain):
    lane = lax.broadcasted_iota(jnp.int32, (1, LANES), 1)
    first = lane < HEAD_DIM
    sq = z * z
    lo = jnp.sum(jnp.where(first, sq, 0.0), axis=-1, keepdims=True)
    hi = jnp.sum(jnp.where(first, 0.0, sq), axis=-1, keepdims=True)
    ms = jnp.where(first, lo, hi) * (1.0 / HEAD_DIM)
    return z * lax.rsqrt(ms + EPS) * gain


def _rope128(z, cos, sin_lo, sin_hi):
    q = ROPE_PAIRS
    return z * cos + pltpu.roll(z, LANES - q, 1) * sin_lo + pltpu.roll(z, q, 1) * sin_hi


def _inproj_latent_kernel(x_ref, m_ref, gpre_ref, w_ref, qg_ref, kg_ref, cos_ref, slo_ref, shi_ref,
                          q_ref, k_ref, v_ref, hq_ref, hv_ref, ff_ref, og_ref, mg_ref, *, d):
    x = x_ref[...]
    shift = m_ref[:, 3 * d:4 * d]
    scale = m_ref[:, 4 * d:5 * d]
    u = (_rms(x, gpre_ref[...]) * (1.0 + scale) + shift).astype(BF16)
    cos, slo, shi = cos_ref[...], slo_ref[...], shi_ref[...]

    def proj(lo, hi):
        return _dot(u, w_ref[:, lo:hi])

    c0 = 0
    pq = proj(c0, c0 + ATT_WIDTH)
    qs = []
    for j in range(ATT_WIDTH // LANES):
        z = _head_rms64(pq[:, j * LANES:(j + 1) * LANES], qg_ref[...])
        qs.append(_rope128(z, cos, slo, shi) * ATT_SCALE)
    q_ref[...] = jnp.concatenate(qs, axis=-1).astype(BF16)
    c0 += ATT_WIDTH
    pk = proj(c0, c0 + KV_WIDTH)
    k_ref[...] = _rope128(_head_rms64(pk, kg_ref[...]), cos, slo, shi).astype(BF16)
    c0 += KV_WIDTH
    v_ref[...] = proj(c0, c0 + KV_WIDTH).astype(BF16)
    c0 += KV_WIDTH
    hq_ref[...] = _silu(proj(c0, c0 + HG_WIDTH)) * HG_SCALE
    c0 += HG_WIDTH
    hv_ref[...] = proj(c0, c0 + HG_WIDTH).astype(BF16)
    c0 += HG_WIDTH
    ff_ref[...] = proj(c0, c0 + 2 * HG_WIDTH)
    c0 += 2 * HG_WIDTH
    og_ref[...] = proj(c0, c0 + HG_WIDTH)
    c0 += HG_WIDTH
    mg_ref[...] = proj(c0, c0 + 2 * d)


def _inproj_latent(x, mods, g_pre, w_in, q_gain2, k_gain2, cos, slo, shi, tm):
    bsz, t, d = x.shape
    nm = mods.shape[-1]
    n_in = w_in.shape[1]

    def tile(width, dtype):
        return (pl.BlockSpec((None, tm, width), lambda b, i: (b, i, 0)),
                jax.ShapeDtypeStruct((bsz, t, width), dtype))

    outs = [tile(ATT_WIDTH, BF16), tile(KV_WIDTH, BF16), tile(KV_WIDTH, BF16), tile(HG_WIDTH, F32),
            tile(HG_WIDTH, BF16), tile(2 * HG_WIDTH, F32), tile(HG_WIDTH, F32), tile(2 * d, F32)]
    rope_spec = pl.BlockSpec((tm, LANES), lambda b, i: (i, 0))
    return pl.pallas_call(
        functools.partial(_inproj_latent_kernel, d=d),
        grid=(bsz, t // tm),
        in_specs=[pl.BlockSpec((None, tm, d), lambda b, i: (b, i, 0)),
                  pl.BlockSpec((None, 1, nm), lambda b, i: (b, 0, 0)),
                  _const_spec((1, d)), _const_spec((d, n_in)),
                  _const_spec((1, LANES)), _const_spec((1, LANES)),
                  rope_spec, rope_spec, rope_spec],
        out_specs=[o[0] for o in outs],
        out_shape=[o[1] for o in outs],
        compiler_params=_params(2),
        name="inproj_latent",
    )(x, mods, g_pre[None, :], w_in, q_gain2, k_gain2, cos, slo, shi)


def _inproj_ctx_kernel(x_ref, m_ref, gpre_ref, wkv_ref, wh_ref, kg_ref, k_ref, v_ref, hv_ref, ff_ref, *, d):
    x = x_ref[...]
    shift = m_ref[:, 3 * d:4 * d]
    scale = m_ref[:, 4 * d:5 * d]
    u = (_rms(x, gpre_ref[...]) * (1.0 + scale) + shift).astype(BF16)
    k_ref[...] = _head_rms64(_dot(u, wkv_ref[:, :KV_WIDTH]), kg_ref[...]).astype(BF16)
    v_ref[...] = _dot(u, wkv_ref[:, KV_WIDTH:]).astype(BF16)
    hv_ref[...] = _dot(u, wh_ref[:, :HG_WIDTH]).astype(BF16)
    ff_ref[...] = _dot(u, wh_ref[:, HG_WIDTH:])


def _inproj_ctx(x, mods, ctx_row, g_pre, w_kv, w_h, k_gain2, tm):
    bsz, t, d = x.shape
    nm = mods.shape[-1]

    def tile(width, dtype):
        return (pl.BlockSpec((None, tm, width), lambda b, i: (b, i, 0)),
                jax.ShapeDtypeStruct((bsz, t, width), dtype))

    outs = [tile(KV_WIDTH, BF16), tile(KV_WIDTH, BF16), tile(HG_WIDTH, BF16), tile(2 * HG_WIDTH, F32)]
    return pl.pallas_call(
        functools.partial(_inproj_ctx_kernel, d=d),
        grid=(bsz, t // tm),
        in_specs=[pl.BlockSpec((None, tm, d), lambda b, i: (b, i, 0)),
                  pl.BlockSpec((None, 1, nm), lambda b, i: (ctx_row, 0, 0)),
                  _const_spec((1, d)), _const_spec(w_kv.shape), _const_spec(w_h.shape),
                  _const_spec((1, LANES))],
        out_specs=[o[0] for o in outs],
        out_shape=[o[1] for o in outs],
        compiler_params=_params(2),
        name="inproj_ctx",
    )(x, mods, g_pre[None, :], w_kv, w_h, k_gain2)


def _attn_kernel(q_ref, kt_ref, v_ref, o_ref):
    outs = []
    for kv in range(N_KV_HEADS):
        kt = kt_ref[kv]
        v = v_ref[kv]
        for g in range(GROUP):
            h = kv * GROUP + g
            q = q_ref[:, h * HEAD_DIM:(h + 1) * HEAD_DIM]
            s = _dot(q, kt)
            p = jnp.exp(s - jnp.max(s, axis=-1, keepdims=True))
            l = jnp.sum(p, axis=-1, keepdims=True)
            outs.append(_dot(p.astype(BF16), v) / l)
    o_ref[...] = jnp.concatenate(outs, axis=-1).astype(BF16)


def _attention(q, kt, v, tq):
    bsz, t, _ = q.shape
    s = kt.shape[-1]
    return pl.pallas_call(
        _attn_kernel,
        grid=(bsz, t // tq),
        in_specs=[pl.BlockSpec((None, tq, ATT_WIDTH), lambda b, i: (b, i, 0)),
                  pl.BlockSpec((None, N_KV_HEADS, HEAD_DIM, s), lambda b, i: (b, 0, 0, 0)),
                  pl.BlockSpec((None, N_KV_HEADS, s, HEAD_DIM), lambda b, i: (b, 0, 0, 0))],
        out_specs=pl.BlockSpec((None, tq, ATT_WIDTH), lambda b, i: (b, i, 0)),
        out_shape=jax.ShapeDtypeStruct((bsz, t, ATT_WIDTH), BF16),
        compiler_params=_params(2),
        name="attn",
    )(q, kt, v)


def _split3(g):
    g1 = g.astype(BF16)
    r1 = g - g1.astype(F32)
    g2 = r1.astype(BF16)
    g3 = (r1 - g2.astype(F32)).astype(BF16)
    return g1, g2, g3


def _hgrn_kernel(lbraw_ref, gain_ref, hq_ref, hv_ref, ff_ref, fb_ref, og_ref, cv_ref, cff_ref, cfb_ref,
                 o_ref, acc_ref, qe_ref, ds_ref, dec_ref, *, n_lat, n_ctx, cpb):
    c = CHUNK
    r = cpb * c
    raw = [lbraw_ref[s] for s in range(lbraw_ref.shape[0])]
    top = functools.reduce(jnp.maximum, raw)
    e = [jnp.exp(x - top) for x in raw]
    lbs = e[0] / functools.reduce(jnp.add, e)

    row = lax.broadcasted_iota(jnp.int32, (r, r), 0)
    col = lax.broadcasted_iota(jnp.int32, (r, r), 1)
    same_chunk = (row // c) == (col // c)
    rowc = lax.broadcasted_iota(jnp.int32, (c, c), 0)
    colc = lax.broadcasted_iota(jnp.int32, (c, c), 1)

    for reverse in (False, True):
        lb = lbs[1:2, :] if reverse else lbs[0:1, :]
        c_f, x_f = (cfb_ref, fb_ref) if reverse else (cff_ref, ff_ref)
        tri = jnp.where(same_chunk & ((col >= row) if reverse else (col <= row)), 1.0, 0.0).astype(BF16)
        causal = (colc >= rowc) if reverse else (colc <= rowc)
        last, mid = (0, c // 2) if reverse else (c - 1, c // 2 - 1)

        def block(i, f_ref, v_ref, q_ref, chunk0):
            r0 = pl.multiple_of(i * r, r)
            f = lb + (1.0 - lb) * _sigmoid(f_ref[pl.ds(r0, r), :])
            k = 1.0 - f
            g1, g2, g3 = _split3(jnp.log(f))
            bb = _dot(tri, jnp.concatenate([g1, g2, g3], axis=1))
            b = bb[:, :HG_DK] + bb[:, HG_DK:2 * HG_DK] + bb[:, 2 * HG_DK:]
            for j in range(cpb):
                rows = pl.ds(r0 + j * c, c)
                bj, kj, vj = b[j * c:(j + 1) * c], k[j * c:(j + 1) * c], v_ref[rows, :]
                b_last = bj[last:last + 1, :]
                ke = (kj * jnp.exp(b_last - bj)).astype(BF16)
                ds_ref[chunk0 + i * cpb + j] = lax.dot_general(vj, ke, TN_DIMS, preferred_element_type=F32)
                dec_ref[chunk0 + i * cpb + j] = jnp.exp(b_last)
                if q_ref is None:
                    continue
                qj = q_ref[rows, :]
                b_mid = bj[mid:mid + 1, :]
                qd = (qj * jnp.exp(bj - b_mid)).astype(BF16)
                kd = (kj * jnp.exp(b_mid - bj)).astype(BF16)
                a = lax.dot_general(qd, kd, NT_DIMS, preferred_element_type=F32)
                intra = _dot(jnp.where(causal, a, 0.0).astype(BF16), vj)
                qe_ref[rows, :] = (qj * jnp.exp(bj)).astype(BF16)
                if reverse:
                    acc_ref[rows, :] += intra
                else:
                    acc_ref[rows, :] = intra

        def ctx_block(i, carry):
            block(i, c_f, cv_ref, None, 0)
            return carry

        def lat_block(i, carry):
            block(i, x_f, hv_ref, hq_ref, n_ctx)
            return carry

        lax.fori_loop(0, n_ctx // cpb, ctx_block, 0)
        lax.fori_loop(0, n_lat // cpb, lat_block, 0)

        def ctx_step(i, st):
            ch = (n_ctx - 1 - i) if reverse else i
            return st * dec_ref[ch] + ds_ref[ch]

        def lat_step(i, st):
            j = (n_lat - 1 - i) if reverse else i
            rows = pl.ds(pl.multiple_of(j * c, c), c)
            inter = lax.dot_general(qe_ref[rows, :], st.astype(BF16), NT_DIMS, preferred_element_type=F32)
            if reverse:
                tot = acc_ref[rows, :] + inter
                o_ref[rows, :] = (_rms(tot, gain_ref[...]) * _silu(og_ref[rows, :])).astype(BF16)
            else:
                acc_ref[rows, :] += inter
            return st * dec_ref[n_ctx + j] + ds_ref[n_ctx + j]

        st = lax.fori_loop(0, n_ctx, ctx_step, jnp.zeros((HG_DV, HG_DK), F32), unroll=True)
        lax.fori_loop(0, n_lat, lat_step, st, unroll=4)


def _hgrn_bidir_kernel(lbraw_ref, gain_ref, hq_ref, hv_ref, ff_ref, fb_ref, og_ref, cv_ref, cff_ref, cfb_ref,
                       o_ref, acc_ref, qe_ref, ds_ref, dec_ref, st_ref, *, n_lat, n_ctx, cpb):
    c = CHUNK
    r = cpb * c
    dk = HG_DK
    raw = [lbraw_ref[s] for s in range(lbraw_ref.shape[0])]
    top = functools.reduce(jnp.maximum, raw)
    e = [jnp.exp(x - top) for x in raw]
    lbs = e[0] / functools.reduce(jnp.add, e)

    row = lax.broadcasted_iota(jnp.int32, (r, r), 0)
    col = lax.broadcasted_iota(jnp.int32, (r, r), 1)
    same_chunk = (row // c) == (col // c)
    masks = (same_chunk & (col <= row), same_chunk & (col >= row))
    tris = tuple(jnp.where(m, 1.0, 0.0).astype(BF16) for m in masks)
    row_chunk = lax.broadcasted_iota(jnp.int32, (r, 1), 0) // c
    last = (c - 1, 0)
    mid = (c // 2 - 1, c // 2)

    def per_chunk_rows(x, off):
        return jnp.concatenate([jnp.broadcast_to(x[j * c + off:j * c + off + 1, :], (c, x.shape[1]))
                                for j in range(cpb)], axis=0)

    def block_diag(x):
        return jnp.concatenate([jnp.where(row_chunk == j, x, 0.0) for j in range(cpb)], axis=1).astype(BF16)

    def pass1(i, f_refs, v_ref, q_ref, chunk0):
        r0 = pl.multiple_of(i * r, r)
        rows = pl.ds(r0, r)
        v = v_ref[rows, :]
        q = None if q_ref is None else q_ref[rows, :]
        kes, amat = [], None
        for d in range(2):
            lb = lbs[d:d + 1, :]
            f = lb + (1.0 - lb) * _sigmoid(f_refs[d][rows, :])
            k = 1.0 - f
            g1, g2, g3 = _split3(jnp.log(f))
            bb = _dot(tris[d], jnp.concatenate([g1, g2, g3], axis=1))
            b = bb[:, :dk] + bb[:, dk:2 * dk] + bb[:, 2 * dk:]
            b_last = per_chunk_rows(b, last[d])
            kes.append(block_diag(k * jnp.exp(b_last - b)))
            for j in range(cpb):
                dec_ref[d, chunk0 + i * cpb + j] = jnp.exp(b[j * c + last[d]:j * c + last[d] + 1, :])
            if q is None:
                continue
            b_mid = per_chunk_rows(b, mid[d])
            qd = (q * jnp.exp(b - b_mid)).astype(BF16)
            kd = (k * jnp.exp(b_mid - b)).astype(BF16)
            a = jnp.where(masks[d], lax.dot_general(qd, kd, NT_DIMS, preferred_element_type=F32), 0.0)
            amat = a if amat is None else amat + a
            qe_ref[rows, d * cpb * dk:(d + 1) * cpb * dk] = block_diag(q * jnp.exp(b))
        ds = lax.dot_general(v, jnp.concatenate(kes, axis=1), TN_DIMS, preferred_element_type=F32)
        for d in range(2):
            for j in range(cpb):
                lo = (d * cpb + j) * dk
                ds_ref[d, chunk0 + i * cpb + j] = ds[:, lo:lo + dk]
        if q is not None:
            acc_ref[rows, :] = _dot(amat.astype(BF16), v)

    def ctx_block(i, carry):
        pass1(i, (cff_ref, cfb_ref), cv_ref, None, 0)
        return carry

    def lat_block(i, carry):
        pass1(i, (ff_ref, fb_ref), hv_ref, hq_ref, n_ctx)
        return carry

    lax.fori_loop(0, n_ctx // cpb, ctx_block, 0)
    lax.fori_loop(0, n_lat // cpb, lat_block, 0, unroll=2)

    for d in range(2):
        def ctx_step(i, st):
            ch = (n_ctx - 1 - i) if d else i
            return st * dec_ref[d, ch] + ds_ref[d, ch]

        def lat_steps(i, st):
            blk = (n_lat // cpb - 1 - i) if d else i
            for jj in range(cpb):
                j = (cpb - 1 - jj) if d else jj
                st_ref[blk, :, (d * cpb + j) * dk:(d * cpb + j + 1) * dk] = st.astype(BF16)
                ch = n_ctx + blk * cpb + j
                st = st * dec_ref[d, ch] + ds_ref[d, ch]
            return st

        st = lax.fori_loop(0, n_ctx, ctx_step, jnp.zeros((HG_DV, dk), F32), unroll=True)
        lax.fori_loop(0, n_lat // cpb, lat_steps, st)

    def pass3(i, carry):
        rows = pl.ds(pl.multiple_of(i * r, r), r)
        inter = lax.dot_general(qe_ref[rows, :], st_ref[i], NT_DIMS, preferred_element_type=F32)
        tot = acc_ref[rows, :] + inter
        o_ref[rows, :] = (_rms(tot, gain_ref[...]) * _silu(og_ref[rows, :])).astype(BF16)
        return carry

    lax.fori_loop(0, n_lat // cpb, pass3, 0, unroll=4)


def _hgrn(lb_raw, hg_gain, hq, hv, ff, og, cv, cff):
    bsz, t, _ = hq.shape
    tc = cv.shape[1]
    slots = lb_raw.shape[0]

    def col(tt, off=0):
        return pl.BlockSpec((None, tt, HG_DK), lambda b, h: (b, 0, h + off))

    n_lat, n_ctx = t // CHUNK, tc // CHUNK
    cpb = next(n for n in (4, 2, 1) if n_lat % n == 0 and n_ctx % n == 0)
    return pl.pallas_call(
        functools.partial(_hgrn_bidir_kernel, n_lat=n_lat, n_ctx=n_ctx, cpb=cpb),
        grid=(bsz, HG_HEADS),
        in_specs=[pl.BlockSpec((slots, 2, HG_DK), lambda b, h: (0, 0, h)),
                  pl.BlockSpec((1, HG_DV), lambda b, h: (0, 0)),
                  col(t), col(t), col(t), col(t, HG_HEADS), col(t),
                  col(tc), col(tc), col(tc, HG_HEADS)],
        out_specs=col(t),
        out_shape=jax.ShapeDtypeStruct((bsz, t, HG_WIDTH), BF16),
        scratch_shapes=[pltpu.VMEM((t, HG_DV), F32),
                        pltpu.VMEM((t, 2 * cpb * HG_DK), BF16),
                        pltpu.VMEM((2, n_ctx + n_lat, HG_DV, HG_DK), F32),
                        pltpu.VMEM((2, n_ctx + n_lat, 1, HG_DK), F32),
                        pltpu.VMEM((n_lat // cpb, HG_DV, 2 * cpb * HG_DK), BF16)],
        compiler_params=_params(2),
        name="hgrn",
    )(lb_raw, hg_gain[None, :], hq, hv, ff, ff, og, cv, cff, cff)


def _merge_kernel(x_ref, m_ref, oa_ref, oh_ref, mg_ref, gpost_ref, wa_ref, wh_ref, wo_ref, o_ref, *, d):
    gate = m_ref[:, 5 * d:6 * d]
    y = (_sigmoid(mg_ref[:, :d]) * _dot(oa_ref[...], wa_ref[...])
         + _sigmoid(mg_ref[:, d:]) * _dot(oh_ref[...], wh_ref[...]))
    z = _dot(y.astype(BF16), wo_ref[...])
    o_ref[...] = x_ref[...] + gate * _rms(z, gpost_ref[...])


def _merge(x, mods, o_att, o_hg, mg, g_post, wa, wh, wo, tm):
    bsz, t, d = x.shape
    nm = mods.shape[-1]

    def tile(width):
        return pl.BlockSpec((None, tm, width), lambda b, i: (b, i, 0))

    return pl.pallas_call(
        functools.partial(_merge_kernel, d=d),
        grid=(bsz, t // tm),
        in_specs=[tile(d), pl.BlockSpec((None, 1, nm), lambda b, i: (b, 0, 0)),
                  tile(ATT_WIDTH), tile(HG_WIDTH), tile(2 * d), _const_spec((1, d)),
                  _const_spec(wa.shape), _const_spec(wh.shape), _const_spec(wo.shape)],
        out_specs=tile(d),
        out_shape=jax.ShapeDtypeStruct((bsz, t, d), F32),
        compiler_params=_params(2),
        name="merge",
    )(x, mods, o_att, o_hg, mg, g_post[None, :], wa, wh, wo)


def _rope_tables(t):
    pos = jnp.arange(t, dtype=jnp.int32)
    row = (pos // GRID_W).astype(F32)
    colp = (pos % GRID_W).astype(F32)
    inv_freq = ROPE_THETA ** (-jnp.arange(ROPE_PAIRS, dtype=F32) / ROPE_PAIRS)
    ang_r = row[:, None] * inv_freq
    ang_c = colp[:, None] * inv_freq
    ang = jnp.concatenate([ang_r, ang_r, ang_c, ang_c], axis=-1)
    cos, sin = jnp.cos(ang), jnp.sin(ang)
    first = (jnp.arange(HEAD_DIM) % (2 * ROPE_PAIRS)) < ROPE_PAIRS
    sin_lo = jnp.where(first, -sin, 0.0)
    sin_hi = jnp.where(first, 0.0, sin)
    two = lambda a: jnp.concatenate([a, a], axis=-1)
    return two(cos), two(sin_lo), two(sin_hi)


def kernel(x, c, ctx, c_ctx, w_mod, b_mod, norm_pre, norm_post, ffn_w_gate, ffn_w_up, ffn_w_down,
           w_in, q_norm, k_norm, hg_lower_bound, hg_norm, w_att_out, w_hg_out, w_o):
    assert w_in.shape[0] == 1, "single-layer block"
    bsz, t, d = x.shape
    tc = ctx.shape[1]
    assert t % GRID_W == 0 and t % CHUNK == 0 and tc % CHUNK == 0
    tm = min(256, t)
    tmc = min(256, tc)

    rows = -(-(bsz + 1) // 8) * 8
    cvec = jnp.concatenate([c, c_ctx[None, :], jnp.zeros((rows - bsz - 1, d), c.dtype)], axis=0)
    mods = _modulation(cvec, w_mod[0], b_mod[0])[:, None, :]
    lat_row = lambda b: b
    ctx_row = lambda b: bsz

    wg, wu, wd = ffn_w_gate[0].astype(BF16), ffn_w_up[0].astype(BF16), ffn_w_down[0].astype(BF16)
    w_in_b = w_in[0].astype(BF16)

    x1 = _ffn(x, mods, lat_row, 0, norm_pre[0, 0], norm_post[0, 0], wg[0], wu[0], wd[0], tm)
    h1 = _ffn(ctx, mods, ctx_row, 0, norm_pre[0, 0], norm_post[0, 0], wg[0], wu[0], wd[0], tmc)

    cos, slo, shi = _rope_tables(t)
    q_gain2 = jnp.concatenate([q_norm[0], q_norm[0]])[None, :]
    k_gain2 = jnp.concatenate([k_norm[0], k_norm[0]])[None, :]
    q, k, v, hq, hv, ff, og, mg = _inproj_latent(x1, mods, norm_pre[0, 1], w_in_b, q_gain2, k_gain2,
                                                  cos, slo, shi, tm)
    kv0 = ATT_WIDTH
    h0 = ATT_WIDTH + 2 * KV_WIDTH + HG_WIDTH
    ck, cv, chv, cff = _inproj_ctx(h1, mods, bsz, norm_pre[0, 1], w_in_b[:, kv0:kv0 + 2 * KV_WIDTH],
                                   w_in_b[:, h0:h0 + 3 * HG_WIDTH], k_gain2, tmc)

    k_all = jnp.concatenate([k, ck], axis=1).reshape(bsz, t + tc, N_KV_HEADS, HEAD_DIM)
    v_all = jnp.concatenate([v, cv], axis=1).reshape(bsz, t + tc, N_KV_HEADS, HEAD_DIM)
    kt = jnp.transpose(k_all, (0, 2, 3, 1))
    vh = jnp.transpose(v_all, (0, 2, 1, 3))
    o_att = _attention(q, kt, vh, tm)

    lb_raw = jnp.transpose(hg_lower_bound.astype(F32), (1, 0, 2))
    o_hg = _hgrn(lb_raw, hg_norm[0], hq, hv, ff, og, chv, cff)

    x2 = _merge(x1, mods, o_att, o_hg, mg, norm_post[0, 1], w_att_out[0].astype(BF16),
                w_hg_out[0].astype(BF16), w_o[0].astype(BF16), tm)
    return _ffn(x2, mods, lat_row, 6, norm_pre[0, 2], norm_post[0, 2], wg[1], wu[1], wd[1], tm)
```

```python
import functools

import jax
import jax.numpy as jnp
from jax import lax
from jax.experimental import pallas as pl
from jax.experimental.pallas import tpu as pltpu

EPS = 1e-6
N_MOD = 9
GRID_W = 64
ROPE_THETA = 10000.0
HEAD_DIM = 64
N_Q_HEADS = 8
N_KV_HEADS = 2
GROUP = N_Q_HEADS // N_KV_HEADS
ATT_WIDTH = N_Q_HEADS * HEAD_DIM
KV_WIDTH = N_KV_HEADS * HEAD_DIM
ROPE_PAIRS = HEAD_DIM // 4
ATT_SCALE = HEAD_DIM ** -0.5
LOG2E = 1.4426950408889634
VT_PAD = 16
HG_HEADS = 4
HG_DK = 128
HG_DV = 128
HG_WIDTH = HG_HEADS * HG_DK
HG_SCALE = HG_DK ** -0.5
CHUNK = 64
LANES = 128
VMEM_LIMIT = 56 * 1024 * 1024

BF16 = jnp.bfloat16
F32 = jnp.float32

NT_DIMS = (((1,), (1,)), ((), ()))
TN_DIMS = (((0,), (0,)), ((), ()))


def _dot(a, b):
    return jnp.dot(a, b, preferred_element_type=F32)


def _rms(x, gain):
    return x * lax.rsqrt(jnp.mean(x * x, axis=-1, keepdims=True) + EPS) * gain


def _sigmoid(x):
    return 1.0 / (1.0 + jnp.exp(-x))


def _silu(x):
    return x * _sigmoid(x)


def _params(n_grid):
    return pltpu.CompilerParams(dimension_semantics=("parallel",) * n_grid, vmem_limit_bytes=VMEM_LIMIT)


def _const_spec(shape):
    nd = len(shape)
    return pl.BlockSpec(shape, lambda *_: (0,) * nd, pipeline_mode=pl.Buffered(1))


def _mod_kernel(c_ref, w_ref, b_ref, o_ref):
    a = _silu(c_ref[...]).astype(BF16)
    o_ref[...] = _dot(a, w_ref[...].astype(BF16)) + b_ref[...]


def _modulation(cvec, w_mod, b_mod, tn=1024):
    rows, d = cvec.shape
    n = w_mod.shape[1]
    return pl.pallas_call(
        _mod_kernel,
        grid=(n // tn,),
        in_specs=[pl.BlockSpec((rows, d), lambda j: (0, 0)),
                  pl.BlockSpec((d, tn), lambda j: (0, j)),
                  pl.BlockSpec((1, tn), lambda j: (0, j))],
        out_specs=pl.BlockSpec((rows, tn), lambda j: (0, j)),
        out_shape=jax.ShapeDtypeStruct((rows, n), F32),
        compiler_params=_params(1),
        name="mod",
    )(cvec, w_mod, b_mod[None, :])


def _ffn_half_step(x, m_ref, gpre_ref, gpost_ref, wg_ref, wu_ref, wd_ref, mod0, d):
    shift = m_ref[:, (mod0 + 0) * d:(mod0 + 1) * d]
    scale = m_ref[:, (mod0 + 1) * d:(mod0 + 2) * d]
    gate = m_ref[:, (mod0 + 2) * d:(mod0 + 3) * d]
    u = (_rms(x, gpre_ref[...]) * (1.0 + scale) + shift).astype(BF16)
    h = (_silu(_dot(u, wg_ref[...])) * _dot(u, wu_ref[...])).astype(BF16)
    y = _dot(h, wd_ref[...])
    return x + 0.5 * (gate * _rms(y, gpost_ref[...]))


def _ffn_kernel(x_ref, m_ref, gpre_ref, gpost_ref, wg_ref, wu_ref, wd_ref, o_ref, *, mod0, d):
    o_ref[...] = _ffn_half_step(x_ref[...], m_ref, gpre_ref, gpost_ref, wg_ref, wu_ref, wd_ref, mod0, d)


def _ffn(x, mods, mod_row, mod0, g_pre, g_post, wg, wu, wd, tm):
    bsz, t, d = x.shape
    f = wg.shape[1]
    nm = mods.shape[-1]
    return pl.pallas_call(
        functools.partial(_ffn_kernel, mod0=mod0, d=d),
        grid=(bsz, t // tm),
        in_specs=[pl.BlockSpec((None, tm, d), lambda b, i: (b, i, 0)),
                  pl.BlockSpec((None, 1, nm), lambda b, i: (mod_row(b), 0, 0)),
                  _const_spec((1, d)), _const_spec((1, d)),
                  _const_spec((d, f)), _const_spec((d, f)), _const_spec((f, d))],
        out_specs=pl.BlockSpec((None, tm, d), lambda b, i: (b, i, 0)),
        out_shape=jax.ShapeDtypeStruct((bsz, t, d), F32),
        compiler_params=_params(2),
        name="ffn",
    )(x, mods, g_pre[None, :], g_post[None, :], wg, wu, wd)


def _head_rms64(z, gain):
    lane = lax.broadcasted_iota(jnp.int32, (1, LANES), 1)
    first = lane < HEAD_DIM
    sq = z * z
    lo = jnp.sum(jnp.where(first, sq, 0.0), axis=-1, keepdims=True)
    hi = jnp.sum(jnp.where(first, 0.0, sq), axis=-1, keepdims=True)
    ms = jnp.where(first, lo, hi) * (1.0 / HEAD_DIM)
    return z * lax.rsqrt(ms + EPS) * gain


def _rope128(z, cos, sin_lo, sin_hi):
    q = ROPE_PAIRS
    return z * cos + pltpu.roll(z, LANES - q, 1) * sin_lo + pltpu.roll(z, q, 1) * sin_hi


def _inproj_latent_kernel(x_ref, m_ref, gpre_ref, w_ref, qg_ref, kg_ref, cos_ref, slo_ref, shi_ref,
                          q_ref, k_ref, v_ref, hq_ref, hv_ref, ff_ref, og_ref, mg_ref, *, d):
    x = x_ref[...]
    shift = m_ref[:, 3 * d:4 * d]
    scale = m_ref[:, 4 * d:5 * d]
    u = (_rms(x, gpre_ref[...]) * (1.0 + scale) + shift).astype(BF16)
    cos, slo, shi = cos_ref[...], slo_ref[...], shi_ref[...]

    def proj(lo, hi):
        return _dot(u, w_ref[:, lo:hi])

    c0 = 0
    pq = proj(c0, c0 + ATT_WIDTH)
    qs = []
    for j in range(ATT_WIDTH // LANES):
        z = _head_rms64(pq[:, j * LANES:(j + 1) * LANES], qg_ref[...])
        qs.append(_rope128(z, cos, slo, shi) * (ATT_SCALE * LOG2E))
    q_ref[...] = jnp.concatenate(qs, axis=-1).astype(BF16)
    c0 += ATT_WIDTH
    pk = proj(c0, c0 + KV_WIDTH)
    k_ref[...] = _rope128(_head_rms64(pk, kg_ref[...]), cos, slo, shi).astype(BF16)
    c0 += KV_WIDTH
    v_ref[...] = proj(c0, c0 + KV_WIDTH).astype(BF16)
    c0 += KV_WIDTH
    hq_ref[...] = _silu(proj(c0, c0 + HG_WIDTH)) * HG_SCALE
    c0 += HG_WIDTH
    hv_ref[...] = proj(c0, c0 + HG_WIDTH).astype(BF16)
    c0 += HG_WIDTH
    ff_ref[...] = proj(c0, c0 + 2 * HG_WIDTH)
    c0 += 2 * HG_WIDTH
    og_ref[...] = proj(c0, c0 + HG_WIDTH)
    c0 += HG_WIDTH
    mg_ref[...] = proj(c0, c0 + 2 * d)


def _inproj_latent(x, mods, g_pre, w_in, q_gain2, k_gain2, cos, slo, shi, tm):
    bsz, t, d = x.shape
    nm = mods.shape[-1]
    n_in = w_in.shape[1]

    def tile(width, dtype):
        return (pl.BlockSpec((None, tm, width), lambda b, i: (b, i, 0)),
                jax.ShapeDtypeStruct((bsz, t, width), dtype))

    outs = [tile(ATT_WIDTH, BF16), tile(KV_WIDTH, BF16), tile(KV_WIDTH, BF16), tile(HG_WIDTH, F32),
            tile(HG_WIDTH, BF16), tile(2 * HG_WIDTH, F32), tile(HG_WIDTH, F32), tile(2 * d, F32)]
    rope_spec = pl.BlockSpec((tm, LANES), lambda b, i: (i, 0))
    return pl.pallas_call(
        functools.partial(_inproj_latent_kernel, d=d),
        grid=(bsz, t // tm),
        in_specs=[pl.BlockSpec((None, tm, d), lambda b, i: (b, i, 0)),
                  pl.BlockSpec((None, 1, nm), lambda b, i: (b, 0, 0)),
                  _const_spec((1, d)), _const_spec((d, n_in)),
                  _const_spec((1, LANES)), _const_spec((1, LANES)),
                  rope_spec, rope_spec, rope_spec],
        out_specs=[o[0] for o in outs],
        out_shape=[o[1] for o in outs],
        compiler_params=_params(2),
        name="inproj_latent",
    )(x, mods, g_pre[None, :], w_in, q_gain2, k_gain2, cos, slo, shi)


def _inproj_ctx_kernel(x_ref, m_ref, gpre_ref, wkv_ref, wh_ref, kg_ref, k_ref, v_ref, hv_ref, ff_ref, *, d):
    x = x_ref[...]
    shift = m_ref[:, 3 * d:4 * d]
    scale = m_ref[:, 4 * d:5 * d]
    u = (_rms(x, gpre_ref[...]) * (1.0 + scale) + shift).astype(BF16)
    k_ref[...] = _head_rms64(_dot(u, wkv_ref[:, :KV_WIDTH]), kg_ref[...]).astype(BF16)
    v_ref[...] = _dot(u, wkv_ref[:, KV_WIDTH:]).astype(BF16)
    hv_ref[...] = _dot(u, wh_ref[:, :HG_WIDTH]).astype(BF16)
    ff_ref[...] = _dot(u, wh_ref[:, HG_WIDTH:])


def _inproj_ctx(x, mods, ctx_row, g_pre, w_kv, w_h, k_gain2, tm):
    bsz, t, d = x.shape
    nm = mods.shape[-1]

    def tile(width, dtype):
        return (pl.BlockSpec((None, tm, width), lambda b, i: (b, i, 0)),
                jax.ShapeDtypeStruct((bsz, t, width), dtype))

    outs = [tile(KV_WIDTH, BF16), tile(KV_WIDTH, BF16), tile(HG_WIDTH, BF16), tile(2 * HG_WIDTH, F32)]
    return pl.pallas_call(
        functools.partial(_inproj_ctx_kernel, d=d),
        grid=(bsz, t // tm),
        in_specs=[pl.BlockSpec((None, tm, d), lambda b, i: (b, i, 0)),
                  pl.BlockSpec((None, 1, nm), lambda b, i: (ctx_row, 0, 0)),
                  _const_spec((1, d)), _const_spec(w_kv.shape), _const_spec(w_h.shape),
                  _const_spec((1, LANES))],
        out_specs=[o[0] for o in outs],
        out_shape=[o[1] for o in outs],
        compiler_params=_params(2),
        name="inproj_ctx",
    )(x, mods, g_pre[None, :], w_kv, w_h, k_gain2)


def _attn_kernel(q_ref, k_ref, vt_ref, o_ref, st_ref, pt_ref):
    tq = st_ref.shape[2]
    n_items = (q_ref.shape[0] // tq) * N_Q_HEADS

    def scores(i):
        r, h = divmod(i, N_Q_HEADS)
        q = q_ref[r * tq:(r + 1) * tq, h * HEAD_DIM:(h + 1) * HEAD_DIM]
        st = lax.dot_general(k_ref[h // GROUP], q, NT_DIMS, preferred_element_type=F32)
        st_ref[i % st_ref.shape[0]] = st
        return jnp.max(st, axis=0, keepdims=True)

    pairs = []

    def values(i):
        h = i % N_Q_HEADS
        ot = _dot(vt_ref[h // GROUP], pt_ref[i % 2])
        pairs.append(ot[:HEAD_DIM, :] / ot[HEAD_DIM:HEAD_DIM + 1, :])

    ahead = st_ref.shape[0] - 1
    ms = [scores(i) for i in range(ahead)]
    for i in range(n_items):
        if i + ahead < n_items:
            ms.append(scores(i + ahead))
        pt_ref[i % 2] = jnp.exp2(st_ref[i % (ahead + 1)] - ms[i]).astype(BF16)
        if i:
            values(i - 1)
    values(n_items - 1)
    for r in range(n_items // N_Q_HEADS):
        heads = pairs[r * N_Q_HEADS:(r + 1) * N_Q_HEADS]
        outs = [jnp.concatenate(heads[j:j + 2], axis=0).T for j in range(0, N_Q_HEADS, 2)]
        o_ref[r * tq:(r + 1) * tq, :] = jnp.concatenate(outs, axis=-1).astype(BF16)


def _attention(q, k, vt, tq, tstep):
    bsz, t, _ = q.shape
    s = k.shape[2]
    vrows = vt.shape[2]
    return pl.pallas_call(
        _attn_kernel,
        grid=(bsz, t // tstep),
        in_specs=[pl.BlockSpec((None, tstep, ATT_WIDTH), lambda b, i: (b, i, 0)),
                  pl.BlockSpec((None, N_KV_HEADS, s, HEAD_DIM), lambda b, i: (b, 0, 0, 0)),
                  pl.BlockSpec((None, N_KV_HEADS, vrows, s), lambda b, i: (b, 0, 0, 0))],
        out_specs=pl.BlockSpec((None, tstep, ATT_WIDTH), lambda b, i: (b, i, 0)),
        out_shape=jax.ShapeDtypeStruct((bsz, t, ATT_WIDTH), BF16),
        scratch_shapes=[pltpu.VMEM((3, s, tq), F32), pltpu.VMEM((2, s, tq), BF16)],
        compiler_params=_params(2),
        name="attn",
    )(q, k, vt)


def _split3(g):
    g1 = g.astype(BF16)
    r1 = g - g1.astype(F32)
    g2 = r1.astype(BF16)
    g3 = (r1 - g2.astype(F32)).astype(BF16)
    return g1, g2, g3


def _hgrn_bidir_kernel(lbraw_ref, gain_ref, hq_ref, hv_ref, ff_ref, fb_ref, og_ref, cv_ref, cff_ref, cfb_ref,
                       o_ref, acc_ref, qe_ref, ds_ref, dec_ref, st_ref, *, n_lat, n_ctx, cpb):
    c = CHUNK
    r = cpb * c
    dk = HG_DK
    raw = [lbraw_ref[s] for s in range(lbraw_ref.shape[0])]
    top = functools.reduce(jnp.maximum, raw)
    e = [jnp.exp(x - top) for x in raw]
    lbs = e[0] / functools.reduce(jnp.add, e)

    row = lax.broadcasted_iota(jnp.int32, (r, r), 0)
    col = lax.broadcasted_iota(jnp.int32, (r, r), 1)
    same_chunk = (row // c) == (col // c)
    masks = (same_chunk & (col <= row), same_chunk & (col >= row))
    tris = tuple(jnp.where(m, 1.0, 0.0).astype(BF16) for m in masks)
    row_chunk = lax.broadcasted_iota(jnp.int32, (r, 1), 0) // c
    last = (c - 1, 0)
    mid = (c // 2 - 1, c // 2)

    def per_chunk_rows(x, off):
        return jnp.concatenate([jnp.broadcast_to(x[j * c + off:j * c + off + 1, :], (c, x.shape[1]))
                                for j in range(cpb)], axis=0)

    def block_diag(x):
        return jnp.concatenate([jnp.where(row_chunk == j, x, 0.0) for j in range(cpb)], axis=1).astype(BF16)

    def pass1(i, f_refs, v_ref, q_ref, chunk0):
        r0 = pl.multiple_of(i * r, r)
        rows = pl.ds(r0, r)
        v = v_ref[rows, :]
        q = None if q_ref is None else q_ref[rows, :]
        kes, amat = [], None
        for d in range(2):
            lb = lbs[d:d + 1, :]
            f = lb + (1.0 - lb) * _sigmoid(f_refs[d][rows, :])
            k = 1.0 - f
            g1, g2, g3 = _split3(jnp.log(f))
            bb = _dot(tris[d], jnp.concatenate([g1, g2, g3], axis=1))
            b = bb[:, :dk] + bb[:, dk:2 * dk] + bb[:, 2 * dk:]
            b_last = per_chunk_rows(b, last[d])
            kes.append(block_diag(k * jnp.exp(b_last - b)))
            for j in range(cpb):
                dec_ref[d, chunk0 + i * cpb + j] = jnp.exp(b[j * c + last[d]:j * c + last[d] + 1, :])
            if q is None:
                continue
            b_mid = per_chunk_rows(b, mid[d])
            qd = (q * jnp.exp(b - b_mid)).astype(BF16)
            kd = (k * jnp.exp(b_mid - b)).astype(BF16)
            a = jnp.where(masks[d], lax.dot_general(qd, kd, NT_DIMS, preferred_element_type=F32), 0.0)
            amat = a if amat is None else amat + a
            qe_ref[rows, d * cpb * dk:(d + 1) * cpb * dk] = block_diag(q * jnp.exp(b))
        ds = lax.dot_general(v, jnp.concatenate(kes, axis=1), TN_DIMS, preferred_element_type=F32)
        for d in range(2):
            for j in range(cpb):
                lo = (d * cpb + j) * dk
                ds_ref[d, chunk0 + i * cpb + j] = ds[:, lo:lo + dk]
        if q is not None:
            acc_ref[rows, :] = _dot(amat.astype(BF16), v)

    def ctx_block(i, carry):
        pass1(i, (cff_ref, cfb_ref), cv_ref, None, 0)
        return carry

    def lat_block(i, carry):
        pass1(i, (ff_ref, fb_ref), hv_ref, hq_ref, n_ctx)
        return carry

    lax.fori_loop(0, n_ctx // cpb, ctx_block, 0)
    lax.fori_loop(0, n_lat // cpb, lat_block, 0, unroll=2)

    for d in range(2):
        def ctx_step(i, st):
            ch = (n_ctx - 1 - i) if d else i
            return st * dec_ref[d, ch] + ds_ref[d, ch]

        def lat_steps(i, st):
            blk = (n_lat // cpb - 1 - i) if d else i
            for jj in range(cpb):
                j = (cpb - 1 - jj) if d else jj
                st_ref[blk, :, (d * cpb + j) * dk:(d * cpb + j + 1) * dk] = st.astype(BF16)
                ch = n_ctx + blk * cpb + j
                st = st * dec_ref[d, ch] + ds_ref[d, ch]
            return st

        st = lax.fori_loop(0, n_ctx, ctx_step, jnp.zeros((HG_DV, dk), F32), unroll=True)
        lax.fori_loop(0, n_lat // cpb, lat_steps, st)

    def pass3(i, carry):
        rows = pl.ds(pl.multiple_of(i * r, r), r)
        inter = lax.dot_general(qe_ref[rows, :], st_ref[i], NT_DIMS, preferred_element_type=F32)
        tot = acc_ref[rows, :] + inter
        o_ref[rows, :] = (_rms(tot, gain_ref[...]) * _silu(og_ref[rows, :])).astype(BF16)
        return carry

    lax.fori_loop(0, n_lat // cpb, pass3, 0, unroll=4)


def _hgrn(lb_raw, hg_gain, hq, hv, ff, og, cv, cff):
    bsz, t, _ = hq.shape
    tc = cv.shape[1]
    slots = lb_raw.shape[0]

    def col(tt, off=0):
        return pl.BlockSpec((None, tt, HG_DK), lambda b, h: (b, 0, h + off))

    n_lat, n_ctx = t // CHUNK, tc // CHUNK
    cpb = next(n for n in (4, 2, 1) if n_lat % n == 0 and n_ctx % n == 0)
    return pl.pallas_call(
        functools.partial(_hgrn_bidir_kernel, n_lat=n_lat, n_ctx=n_ctx, cpb=cpb),
        grid=(bsz, HG_HEADS),
        in_specs=[pl.BlockSpec((slots, 2, HG_DK), lambda b, h: (0, 0, h)),
                  pl.BlockSpec((1, HG_DV), lambda b, h: (0, 0)),
                  col(t), col(t), col(t), col(t, HG_HEADS), col(t),
                  col(tc), col(tc), col(tc, HG_HEADS)],
        out_specs=col(t),
        out_shape=jax.ShapeDtypeStruct((bsz, t, HG_WIDTH), BF16),
        scratch_shapes=[pltpu.VMEM((t, HG_DV), F32),
                        pltpu.VMEM((t, 2 * cpb * HG_DK), BF16),
                        pltpu.VMEM((2, n_ctx + n_lat, HG_DV, HG_DK), F32),
                        pltpu.VMEM((2, n_ctx + n_lat, 1, HG_DK), F32),
                        pltpu.VMEM((n_lat // cpb, HG_DV, 2 * cpb * HG_DK), BF16)],
        compiler_params=_params(2),
        name="hgrn",
    )(lb_raw, hg_gain[None, :], hq, hv, ff, ff, og, cv, cff, cff)


def _merge_ffn_kernel(x_ref, m_ref, oa_ref, oh_ref, mg_ref, gpost1_ref, wa_ref, wh_ref, wo_ref,
                      gpre2_ref, gpost2_ref, wg_ref, wu_ref, wd_ref, o_ref, *, d):
    gate = m_ref[:, 5 * d:6 * d]
    y = (_sigmoid(mg_ref[:, :d]) * _dot(oa_ref[...], wa_ref[...])
         + _sigmoid(mg_ref[:, d:]) * _dot(oh_ref[...], wh_ref[...]))
    z = _dot(y.astype(BF16), wo_ref[...])
    x2 = x_ref[...] + gate * _rms(z, gpost1_ref[...])
    o_ref[...] = _ffn_half_step(x2, m_ref, gpre2_ref, gpost2_ref, wg_ref, wu_ref, wd_ref, 6, d)


def _merge_ffn(x, mods, o_att, o_hg, mg, g_post1, wa, wh, wo, g_pre2, g_post2, wg, wu, wd, tm):
    bsz, t, d = x.shape
    nm = mods.shape[-1]

    def tile(width):
        return pl.BlockSpec((None, tm, width), lambda b, i: (b, i, 0))

    return pl.pallas_call(
        functools.partial(_merge_ffn_kernel, d=d),
        grid=(bsz, t // tm),
        in_specs=[tile(d), pl.BlockSpec((None, 1, nm), lambda b, i: (b, 0, 0)),
                  tile(ATT_WIDTH), tile(HG_WIDTH), tile(2 * d), _const_spec((1, d)),
                  _const_spec(wa.shape), _const_spec(wh.shape), _const_spec(wo.shape),
                  _const_spec((1, d)), _const_spec((1, d)),
                  _const_spec(wg.shape), _const_spec(wu.shape), _const_spec(wd.shape)],
        out_specs=tile(d),
        out_shape=jax.ShapeDtypeStruct((bsz, t, d), F32),
        compiler_params=_params(2),
        name="merge_ffn",
    )(x, mods, o_att, o_hg, mg, g_post1[None, :], wa, wh, wo, g_pre2[None, :], g_post2[None, :], wg, wu, wd)


def _rope_tables(t):
    pos = jnp.arange(t, dtype=jnp.int32)
    row = (pos // GRID_W).astype(F32)
    colp = (pos % GRID_W).astype(F32)
    inv_freq = ROPE_THETA ** (-jnp.arange(ROPE_PAIRS, dtype=F32) / ROPE_PAIRS)
    ang_r = row[:, None] * inv_freq
    ang_c = colp[:, None] * inv_freq
    ang = jnp.concatenate([ang_r, ang_r, ang_c, ang_c], axis=-1)
    cos, sin = jnp.cos(ang), jnp.sin(ang)
    first = (jnp.arange(HEAD_DIM) % (2 * ROPE_PAIRS)) < ROPE_PAIRS
    sin_lo = jnp.where(first, -sin, 0.0)
    sin_hi = jnp.where(first, 0.0, sin)
    two = lambda a: jnp.concatenate([a, a], axis=-1)
    return two(cos), two(sin_lo), two(sin_hi)


def kernel(x, c, ctx, c_ctx, w_mod, b_mod, norm_pre, norm_post, ffn_w_gate, ffn_w_up, ffn_w_down,
           w_in, q_norm, k_norm, hg_lower_bound, hg_norm, w_att_out, w_hg_out, w_o):
    assert w_in.shape[0] == 1, "single-layer block"
    bsz, t, d = x.shape
    tc = ctx.shape[1]
    assert t % GRID_W == 0 and t % CHUNK == 0 and tc % CHUNK == 0
    tm = min(256, t)
    tmc = min(256, tc)
    tmf = 512 if t % 512 == 0 else tm

    rows = -(-(bsz + 1) // 8) * 8
    cvec = jnp.concatenate([c, c_ctx[None, :], jnp.zeros((rows - bsz - 1, d), c.dtype)], axis=0)
    mods = _modulation(cvec, w_mod[0], b_mod[0])[:, None, :]
    lat_row = lambda b: b
    ctx_row = lambda b: bsz

    wg, wu, wd = ffn_w_gate[0].astype(BF16), ffn_w_up[0].astype(BF16), ffn_w_down[0].astype(BF16)
    w_in_b = w_in[0].astype(BF16)

    x1 = _ffn(x, mods, lat_row, 0, norm_pre[0, 0], norm_post[0, 0], wg[0], wu[0], wd[0], tmf)
    h1 = _ffn(ctx, mods, ctx_row, 0, norm_pre[0, 0], norm_post[0, 0], wg[0], wu[0], wd[0], tmc)

    cos, slo, shi = _rope_tables(t)
    q_gain2 = jnp.concatenate([q_norm[0], q_norm[0]])[None, :]
    k_gain2 = jnp.concatenate([k_norm[0], k_norm[0]])[None, :]
    q, k, v, hq, hv, ff, og, mg = _inproj_latent(x1, mods, norm_pre[0, 1], w_in_b, q_gain2, k_gain2,
                                                  cos, slo, shi, tm)
    kv0 = ATT_WIDTH
    h0 = ATT_WIDTH + 2 * KV_WIDTH + HG_WIDTH
    ck, cv, chv, cff = _inproj_ctx(h1, mods, bsz, norm_pre[0, 1], w_in_b[:, kv0:kv0 + 2 * KV_WIDTH],
                                   w_in_b[:, h0:h0 + 3 * HG_WIDTH], k_gain2, tmc)

    k_all = jnp.concatenate([k, ck], axis=1).reshape(bsz, t + tc, N_KV_HEADS, HEAD_DIM)
    v_all = jnp.concatenate([v, cv], axis=1).reshape(bsz, t + tc, N_KV_HEADS, HEAD_DIM)
    kh = jnp.transpose(k_all, (0, 2, 1, 3))
    vt = jnp.transpose(v_all, (0, 2, 3, 1))
    pad_rows = jnp.zeros((bsz, N_KV_HEADS, VT_PAD, t + tc), BF16).at[:, :, 0, :].set(1.0)
    o_att = _attention(q, kh, jnp.concatenate([vt, pad_rows], axis=2), tm, tmf)

    lb_raw = jnp.transpose(hg_lower_bound.astype(F32), (1, 0, 2))
    o_hg = _hgrn(lb_raw, hg_norm[0], hq, hv, ff, og, chv, cff)

    return _merge_ffn(x1, mods, o_att, o_hg, mg, norm_post[0, 1], w_att_out[0].astype(BF16),
                      w_hg_out[0].astype(BF16), w_o[0].astype(BF16),
                      norm_pre[0, 2], norm_post[0, 2], wg[1], wu[1], wd[1], tm)
```

```python
import functools

import jax
import jax.numpy as jnp
from jax import lax
from jax.experimental import pallas as pl
from jax.experimental.pallas import tpu as pltpu

EPS = 1e-6
N_MOD = 9
GRID_W = 64
ROPE_THETA = 10000.0
HEAD_DIM = 64
N_Q_HEADS = 8
N_KV_HEADS = 2
GROUP = N_Q_HEADS // N_KV_HEADS
ATT_WIDTH = N_Q_HEADS * HEAD_DIM
KV_WIDTH = N_KV_HEADS * HEAD_DIM
ROPE_PAIRS = HEAD_DIM // 4
ATT_SCALE = HEAD_DIM ** -0.5
LOG2E = 1.4426950408889634
VT_PAD = 16
HG_HEADS = 4
HG_DK = 128
HG_DV = 128
HG_WIDTH = HG_HEADS * HG_DK
HG_SCALE = HG_DK ** -0.5
CHUNK = 64
LANES = 128
VMEM_LIMIT = 56 * 1024 * 1024

BF16 = jnp.bfloat16
F32 = jnp.float32

NT_DIMS = (((1,), (1,)), ((), ()))
TN_DIMS = (((0,), (0,)), ((), ()))


def _dot(a, b):
    return jnp.dot(a, b, preferred_element_type=F32)


def _rms(x, gain):
    return x * lax.rsqrt(jnp.mean(x * x, axis=-1, keepdims=True) + EPS) * gain


def _sigmoid(x):
    return 1.0 / (1.0 + jnp.exp(-x))


def _silu(x):
    return x * _sigmoid(x)


def _params(n_grid):
    return pltpu.CompilerParams(dimension_semantics=("parallel",) * n_grid, vmem_limit_bytes=VMEM_LIMIT)


def _const_spec(shape):
    nd = len(shape)
    return pl.BlockSpec(shape, lambda *_: (0,) * nd, pipeline_mode=pl.Buffered(1))


def _mod_kernel(c_ref, w_ref, b_ref, o_ref):
    a = _silu(c_ref[...]).astype(BF16)
    o_ref[...] = _dot(a, w_ref[...].astype(BF16)) + b_ref[...]


def _modulation(cvec, w_mod, b_mod, tn=1024):
    rows, d = cvec.shape
    n = w_mod.shape[1]
    return pl.pallas_call(
        _mod_kernel,
        grid=(n // tn,),
        in_specs=[pl.BlockSpec((rows, d), lambda j: (0, 0)),
                  pl.BlockSpec((d, tn), lambda j: (0, j)),
                  pl.BlockSpec((1, tn), lambda j: (0, j))],
        out_specs=pl.BlockSpec((rows, tn), lambda j: (0, j)),
        out_shape=jax.ShapeDtypeStruct((rows, n), F32),
        compiler_params=_params(1),
        name="mod",
    )(cvec, w_mod, b_mod[None, :])


def _ffn_half_step(x, m_ref, gpre_ref, gpost_ref, wg_ref, wu_ref, wd_ref, mod0, d):
    shift = m_ref[:, (mod0 + 0) * d:(mod0 + 1) * d]
    scale = m_ref[:, (mod0 + 1) * d:(mod0 + 2) * d]
    gate = m_ref[:, (mod0 + 2) * d:(mod0 + 3) * d]
    u = (_rms(x, gpre_ref[...]) * (1.0 + scale) + shift).astype(BF16)
    h = (_silu(_dot(u, wg_ref[...])) * _dot(u, wu_ref[...])).astype(BF16)
    y = _dot(h, wd_ref[...])
    return x + 0.5 * (gate * _rms(y, gpost_ref[...]))


def _ffn_kernel(x_ref, m_ref, gpre_ref, gpost_ref, wg_ref, wu_ref, wd_ref, o_ref, *, mod0, d):
    o_ref[...] = _ffn_half_step(x_ref[...], m_ref, gpre_ref, gpost_ref, wg_ref, wu_ref, wd_ref, mod0, d)


def _ffn(x, mods, mod_row, mod0, g_pre, g_post, wg, wu, wd, tm):
    bsz, t, d = x.shape
    f = wg.shape[1]
    nm = mods.shape[-1]
    return pl.pallas_call(
        functools.partial(_ffn_kernel, mod0=mod0, d=d),
        grid=(bsz, t // tm),
        in_specs=[pl.BlockSpec((None, tm, d), lambda b, i: (b, i, 0)),
                  pl.BlockSpec((None, 1, nm), lambda b, i: (mod_row(b), 0, 0)),
                  _const_spec((1, d)), _const_spec((1, d)),
                  _const_spec((d, f)), _const_spec((d, f)), _const_spec((f, d))],
        out_specs=pl.BlockSpec((None, tm, d), lambda b, i: (b, i, 0)),
        out_shape=jax.ShapeDtypeStruct((bsz, t, d), F32),
        compiler_params=_params(2),
        name="ffn",
    )(x, mods, g_pre[None, :], g_post[None, :], wg, wu, wd)


def _log2_forget(raw, lbraw_ref):
    slots = [lbraw_ref[s:s + 1, :] for s in range(lbraw_ref.shape[0])]
    top = functools.reduce(jnp.maximum, slots)
    e = [jnp.exp(s - top) for s in slots]
    lb = e[0] / functools.reduce(jnp.add, e)
    return jnp.log2(lb + (1.0 - lb) * _sigmoid(raw))


def _head_rms64(z, gain):
    lane = lax.broadcasted_iota(jnp.int32, (1, LANES), 1)
    first = lane < HEAD_DIM
    sq = z * z
    lo = jnp.sum(jnp.where(first, sq, 0.0), axis=-1, keepdims=True)
    hi = jnp.sum(jnp.where(first, 0.0, sq), axis=-1, keepdims=True)
    ms = jnp.where(first, lo, hi) * (1.0 / HEAD_DIM)
    return z * lax.rsqrt(ms + EPS) * gain


def _rope128(z, cos, sin_lo, sin_hi):
    q = ROPE_PAIRS
    return z * cos + pltpu.roll(z, LANES - q, 1) * sin_lo + pltpu.roll(z, q, 1) * sin_hi


def _inproj_latent_kernel(x_ref, m_ref, gpre_ref, w_ref, qg_ref, kg_ref, lbraw_ref, cos_ref, slo_ref, shi_ref,
                          q_ref, k_ref, v_ref, hq_ref, hv_ref, ff_ref, og_ref, mg_ref, *, d):
    x = x_ref[...]
    shift = m_ref[:, 3 * d:4 * d]
    scale = m_ref[:, 4 * d:5 * d]
    u = (_rms(x, gpre_ref[...]) * (1.0 + scale) + shift).astype(BF16)
    cos, slo, shi = cos_ref[...], slo_ref[...], shi_ref[...]

    def proj(lo, hi):
        return _dot(u, w_ref[:, lo:hi])

    c0 = 0
    pq = proj(c0, c0 + ATT_WIDTH)
    qs = []
    for j in range(ATT_WIDTH // LANES):
        z = _head_rms64(pq[:, j * LANES:(j + 1) * LANES], qg_ref[...])
        qs.append(_rope128(z, cos, slo, shi) * (ATT_SCALE * LOG2E))
    q_ref[...] = jnp.concatenate(qs, axis=-1).astype(BF16)
    c0 += ATT_WIDTH
    pk = proj(c0, c0 + KV_WIDTH)
    k_ref[...] = _rope128(_head_rms64(pk, kg_ref[...]), cos, slo, shi).astype(BF16)
    c0 += KV_WIDTH
    v_ref[...] = proj(c0, c0 + KV_WIDTH).astype(BF16)
    c0 += KV_WIDTH
    hq_ref[...] = _silu(proj(c0, c0 + HG_WIDTH)) * HG_SCALE
    c0 += HG_WIDTH
    hv_ref[...] = proj(c0, c0 + HG_WIDTH).astype(BF16)
    c0 += HG_WIDTH
    ff_ref[...] = _log2_forget(proj(c0, c0 + 2 * HG_WIDTH), lbraw_ref)
    c0 += 2 * HG_WIDTH
    og_ref[...] = proj(c0, c0 + HG_WIDTH)
    c0 += HG_WIDTH
    mg_ref[...] = proj(c0, c0 + 2 * d)


def _inproj_latent(x, mods, g_pre, w_in, q_gain2, k_gain2, lb_raw, cos, slo, shi, tm):
    bsz, t, d = x.shape
    nm = mods.shape[-1]
    n_in = w_in.shape[1]

    def tile(width, dtype):
        return (pl.BlockSpec((None, tm, width), lambda b, i: (b, i, 0)),
                jax.ShapeDtypeStruct((bsz, t, width), dtype))

    outs = [tile(ATT_WIDTH, BF16), tile(KV_WIDTH, BF16), tile(KV_WIDTH, BF16), tile(HG_WIDTH, F32),
            tile(HG_WIDTH, BF16), tile(2 * HG_WIDTH, F32), tile(HG_WIDTH, F32), tile(2 * d, F32)]
    rope_spec = pl.BlockSpec((tm, LANES), lambda b, i: (i, 0))
    return pl.pallas_call(
        functools.partial(_inproj_latent_kernel, d=d),
        grid=(bsz, t // tm),
        in_specs=[pl.BlockSpec((None, tm, d), lambda b, i: (b, i, 0)),
                  pl.BlockSpec((None, 1, nm), lambda b, i: (b, 0, 0)),
                  _const_spec((1, d)), _const_spec((d, n_in)),
                  _const_spec((1, LANES)), _const_spec((1, LANES)), _const_spec(lb_raw.shape),
                  rope_spec, rope_spec, rope_spec],
        out_specs=[o[0] for o in outs],
        out_shape=[o[1] for o in outs],
        compiler_params=_params(2),
        name="inproj_latent",
    )(x, mods, g_pre[None, :], w_in, q_gain2, k_gain2, lb_raw, cos, slo, shi)


def _inproj_ctx_kernel(x_ref, m_ref, gpre_ref, wkv_ref, wh_ref, kg_ref, lbraw_ref,
                       k_ref, v_ref, hv_ref, ff_ref, *, d):
    x = x_ref[...]
    shift = m_ref[:, 3 * d:4 * d]
    scale = m_ref[:, 4 * d:5 * d]
    u = (_rms(x, gpre_ref[...]) * (1.0 + scale) + shift).astype(BF16)
    k_ref[...] = _head_rms64(_dot(u, wkv_ref[:, :KV_WIDTH]), kg_ref[...]).astype(BF16)
    v_ref[...] = _dot(u, wkv_ref[:, KV_WIDTH:]).astype(BF16)
    hv_ref[...] = _dot(u, wh_ref[:, :HG_WIDTH]).astype(BF16)
    ff_ref[...] = _log2_forget(_dot(u, wh_ref[:, HG_WIDTH:]), lbraw_ref)


def _inproj_ctx(x, mods, ctx_row, g_pre, w_kv, w_h, k_gain2, lb_raw, tm):
    bsz, t, d = x.shape
    nm = mods.shape[-1]

    def tile(width, dtype):
        return (pl.BlockSpec((None, tm, width), lambda b, i: (b, i, 0)),
                jax.ShapeDtypeStruct((bsz, t, width), dtype))

    outs = [tile(KV_WIDTH, BF16), tile(KV_WIDTH, BF16), tile(HG_WIDTH, BF16), tile(2 * HG_WIDTH, F32)]
    return pl.pallas_call(
        functools.partial(_inproj_ctx_kernel, d=d),
        grid=(bsz, t // tm),
        in_specs=[pl.BlockSpec((None, tm, d), lambda b, i: (b, i, 0)),
                  pl.BlockSpec((None, 1, nm), lambda b, i: (ctx_row, 0, 0)),
                  _const_spec((1, d)), _const_spec(w_kv.shape), _const_spec(w_h.shape),
                  _const_spec((1, LANES)), _const_spec(lb_raw.shape)],
        out_specs=[o[0] for o in outs],
        out_shape=[o[1] for o in outs],
        compiler_params=_params(2),
        name="inproj_ctx",
    )(x, mods, g_pre[None, :], w_kv, w_h, k_gain2, lb_raw)


def _attn_kernel(q_ref, k_ref, vt_ref, o_ref, st_ref, pt_ref):
    tq = st_ref.shape[2]
    n_items = (q_ref.shape[0] // tq) * N_Q_HEADS

    def scores(i):
        r, h = divmod(i, N_Q_HEADS)
        q = q_ref[r * tq:(r + 1) * tq, h * HEAD_DIM:(h + 1) * HEAD_DIM]
        st = lax.dot_general(k_ref[h // GROUP], q, NT_DIMS, preferred_element_type=F32)
        st_ref[i % st_ref.shape[0]] = st
        return jnp.max(st, axis=0, keepdims=True)

    pairs = []

    def values(i):
        h = i % N_Q_HEADS
        ot = _dot(vt_ref[h // GROUP], pt_ref[i % 2])
        pairs.append(ot[:HEAD_DIM, :] / ot[HEAD_DIM:HEAD_DIM + 1, :])

    ahead = st_ref.shape[0] - 1
    ms = [scores(i) for i in range(ahead)]
    for i in range(n_items):
        if i + ahead < n_items:
            ms.append(scores(i + ahead))
        pt_ref[i % 2] = jnp.exp2(st_ref[i % (ahead + 1)] - ms[i]).astype(BF16)
        if i:
            values(i - 1)
    values(n_items - 1)
    for r in range(n_items // N_Q_HEADS):
        heads = pairs[r * N_Q_HEADS:(r + 1) * N_Q_HEADS]
        outs = [jnp.concatenate(heads[j:j + 2], axis=0).T for j in range(0, N_Q_HEADS, 2)]
        o_ref[r * tq:(r + 1) * tq, :] = jnp.concatenate(outs, axis=-1).astype(BF16)


def _attention(q, k, vt, tq, tstep):
    bsz, t, _ = q.shape
    s = k.shape[2]
    vrows = vt.shape[2]
    return pl.pallas_call(
        _attn_kernel,
        grid=(bsz, t // tstep),
        in_specs=[pl.BlockSpec((None, tstep, ATT_WIDTH), lambda b, i: (b, i, 0)),
                  pl.BlockSpec((None, N_KV_HEADS, s, HEAD_DIM), lambda b, i: (b, 0, 0, 0)),
                  pl.BlockSpec((None, N_KV_HEADS, vrows, s), lambda b, i: (b, 0, 0, 0))],
        out_specs=pl.BlockSpec((None, tstep, ATT_WIDTH), lambda b, i: (b, i, 0)),
        out_shape=jax.ShapeDtypeStruct((bsz, t, ATT_WIDTH), BF16),
        scratch_shapes=[pltpu.VMEM((3, s, tq), F32), pltpu.VMEM((2, s, tq), BF16)],
        compiler_params=_params(2),
        name="attn",
    )(q, k, vt)


def _split3(g):
    g1 = g.astype(BF16)
    r1 = g - g1.astype(F32)
    g2 = r1.astype(BF16)
    g3 = (r1 - g2.astype(F32)).astype(BF16)
    return g1, g2, g3


def _hgrn_bidir_kernel(gain_ref, hq_ref, hv_ref, ff_ref, fb_ref, og_ref, cv_ref, cff_ref, cfb_ref,
                       o_ref, acc_ref, qe_ref, ds_ref, dec_ref, st_ref, bk_ref, ke_ref, a_ref,
                       *, n_lat, n_ctx, cpb):
    c = CHUNK
    r = cpb * c
    dk = HG_DK
    row = lax.broadcasted_iota(jnp.int32, (r, r), 0)
    col = lax.broadcasted_iota(jnp.int32, (r, r), 1)
    same_chunk = (row // c) == (col // c)
    masks = (same_chunk & (col <= row), same_chunk & (col >= row))
    tris = tuple(jnp.where(m, 1.0, 0.0).astype(BF16) for m in masks)
    last = (c - 1, 0)
    mid = (c // 2 - 1, c // 2)

    def per_chunk_rows(x, off):
        return jnp.concatenate([jnp.broadcast_to(x[j * c + off:j * c + off + 1, :], (c, x.shape[1]))
                                for j in range(cpb)], axis=0)

    def lane_block(d, j):
        return slice((d * cpb + j) * dk, (d * cpb + j + 1) * dk)

    ke_ref[...] = jnp.zeros(ke_ref.shape, BF16)
    qe_ref[...] = jnp.zeros(qe_ref.shape, BF16)

    blocks = ([((cff_ref, cfb_ref), cv_ref, None, i * r, i * cpb) for i in range(n_ctx // cpb)]
              + [((ff_ref, fb_ref), hv_ref, hq_ref, i * r, n_ctx + i * cpb) for i in range(n_lat // cpb)])

    def decays(n):
        f_refs, _, _, r0, _ = blocks[n]
        for d in range(2):
            lf = f_refs[d][r0:r0 + r, :]
            g1, g2, g3 = _split3(lf)
            bb = _dot(tris[d], jnp.concatenate([g1, g2, g3], axis=1))
            bk_ref[n % 2, d] = bb[:, :dk] + bb[:, dk:2 * dk] + bb[:, 2 * dk:]
            bk_ref[n % 2, 2 + d] = 1.0 - jnp.exp2(lf)

    def scores(n):
        _, _, q_ref, r0, ch0 = blocks[n]
        q = None if q_ref is None else q_ref[r0:r0 + r, :]
        amat = None
        for d in range(2):
            b, k = bk_ref[n % 2, d], bk_ref[n % 2, 2 + d]
            b_last = per_chunk_rows(b, last[d])
            ke = (k * jnp.exp2(b_last - b)).astype(BF16)
            for j in range(cpb):
                ke_ref[n % 2, j * c:(j + 1) * c, lane_block(d, j)] = ke[j * c:(j + 1) * c, :]
                dec_ref[d, ch0 + j] = jnp.exp2(b[j * c + last[d]:j * c + last[d] + 1, :])
            if q is None:
                continue
            b_mid = per_chunk_rows(b, mid[d])
            qd = (q * jnp.exp2(b - b_mid)).astype(BF16)
            kd = (k * jnp.exp2(b_mid - b)).astype(BF16)
            a = jnp.where(masks[d], lax.dot_general(qd, kd, NT_DIMS, preferred_element_type=F32), 0.0)
            amat = a if amat is None else amat + a
            qe = (q * jnp.exp2(b)).astype(BF16)
            for j in range(cpb):
                qe_ref[r0 + j * c:r0 + (j + 1) * c, lane_block(d, j)] = qe[j * c:(j + 1) * c, :]
        if q is not None:
            a_ref[n % 2] = amat.astype(BF16)

    def products(n):
        _, v_ref, q_ref, r0, ch0 = blocks[n]
        v = v_ref[r0:r0 + r, :]
        ds = lax.dot_general(v, ke_ref[n % 2], TN_DIMS, preferred_element_type=F32)
        for d in range(2):
            for j in range(cpb):
                lo = (d * cpb + j) * dk
                ds_ref[d, ch0 + j] = ds[:, lo:lo + dk]
        if q_ref is not None:
            acc_ref[r0:r0 + r, :] = _dot(a_ref[n % 2], v)

    decays(0)
    for n in range(len(blocks)):
        if n + 1 < len(blocks):
            decays(n + 1)
        scores(n)
        if n:
            products(n - 1)
    products(len(blocks) - 1)

    for d in range(2):
        def ctx_step(i, st):
            ch = (n_ctx - 1 - i) if d else i
            return st * dec_ref[d, ch] + ds_ref[d, ch]

        def lat_steps(i, st):
            blk = (n_lat // cpb - 1 - i) if d else i
            for jj in range(cpb):
                j = (cpb - 1 - jj) if d else jj
                st_ref[blk, :, (d * cpb + j) * dk:(d * cpb + j + 1) * dk] = st.astype(BF16)
                ch = n_ctx + blk * cpb + j
                st = st * dec_ref[d, ch] + ds_ref[d, ch]
            return st

        st = lax.fori_loop(0, n_ctx, ctx_step, jnp.zeros((HG_DV, dk), F32), unroll=True)
        lax.fori_loop(0, n_lat // cpb, lat_steps, st)

    def pass3(i, carry):
        rows = pl.ds(pl.multiple_of(i * r, r), r)
        inter = lax.dot_general(qe_ref[rows, :], st_ref[i], NT_DIMS, preferred_element_type=F32)
        tot = acc_ref[rows, :] + inter
        o_ref[rows, :] = (_rms(tot, gain_ref[...]) * _silu(og_ref[rows, :])).astype(BF16)
        return carry

    lax.fori_loop(0, n_lat // cpb, pass3, 0, unroll=4)


def _hgrn(hg_gain, hq, hv, ff, og, cv, cff):
    bsz, t, _ = hq.shape
    tc = cv.shape[1]

    def col(tt, off=0):
        return pl.BlockSpec((None, tt, HG_DK), lambda b, h: (b, 0, h + off))

    n_lat, n_ctx = t // CHUNK, tc // CHUNK
    cpb = next(n for n in (4, 2, 1) if n_lat % n == 0 and n_ctx % n == 0)
    return pl.pallas_call(
        functools.partial(_hgrn_bidir_kernel, n_lat=n_lat, n_ctx=n_ctx, cpb=cpb),
        grid=(bsz, HG_HEADS),
        in_specs=[pl.BlockSpec((1, HG_DV), lambda b, h: (0, 0)),
                  col(t), col(t), col(t), col(t, HG_HEADS), col(t),
                  col(tc), col(tc), col(tc, HG_HEADS)],
        out_specs=col(t),
        out_shape=jax.ShapeDtypeStruct((bsz, t, HG_WIDTH), BF16),
        scratch_shapes=[pltpu.VMEM((t, HG_DV), F32),
                        pltpu.VMEM((t, 2 * cpb * HG_DK), BF16),
                        pltpu.VMEM((2, n_ctx + n_lat, HG_DV, HG_DK), F32),
                        pltpu.VMEM((2, n_ctx + n_lat, 1, HG_DK), F32),
                        pltpu.VMEM((n_lat // cpb, HG_DV, 2 * cpb * HG_DK), BF16),
                        pltpu.VMEM((2, 4, cpb * CHUNK, HG_DK), F32),
                        pltpu.VMEM((2, cpb * CHUNK, 2 * cpb * HG_DK), BF16),
                        pltpu.VMEM((2, cpb * CHUNK, cpb * CHUNK), BF16)],
        compiler_params=_params(2),
        name="hgrn",
    )(hg_gain[None, :], hq, hv, ff, ff, og, cv, cff, cff)


def _merge_ffn_kernel(x_ref, m_ref, oa_ref, oh_ref, mg_ref, gpost1_ref, wa_ref, wh_ref, wo_ref,
                      gpre2_ref, gpost2_ref, wg_ref, wu_ref, wd_ref, o_ref, *, d):
    gate = m_ref[:, 5 * d:6 * d]
    y = (_sigmoid(mg_ref[:, :d]) * _dot(oa_ref[...], wa_ref[...])
         + _sigmoid(mg_ref[:, d:]) * _dot(oh_ref[...], wh_ref[...]))
    z = _dot(y.astype(BF16), wo_ref[...])
    x2 = x_ref[...] + gate * _rms(z, gpost1_ref[...])
    o_ref[...] = _ffn_half_step(x2, m_ref, gpre2_ref, gpost2_ref, wg_ref, wu_ref, wd_ref, 6, d)


def _merge_ffn(x, mods, o_att, o_hg, mg, g_post1, wa, wh, wo, g_pre2, g_post2, wg, wu, wd, tm):
    bsz, t, d = x.shape
    nm = mods.shape[-1]

    def tile(width):
        return pl.BlockSpec((None, tm, width), lambda b, i: (b, i, 0))

    return pl.pallas_call(
        functools.partial(_merge_ffn_kernel, d=d),
        grid=(bsz, t // tm),
        in_specs=[tile(d), pl.BlockSpec((None, 1, nm), lambda b, i: (b, 0, 0)),
                  tile(ATT_WIDTH), tile(HG_WIDTH), tile(2 * d), _const_spec((1, d)),
                  _const_spec(wa.shape), _const_spec(wh.shape), _const_spec(wo.shape),
                  _const_spec((1, d)), _const_spec((1, d)),
                  _const_spec(wg.shape), _const_spec(wu.shape), _const_spec(wd.shape)],
        out_specs=tile(d),
        out_shape=jax.ShapeDtypeStruct((bsz, t, d), F32),
        compiler_params=_params(2),
        name="merge_ffn",
    )(x, mods, o_att, o_hg, mg, g_post1[None, :], wa, wh, wo, g_pre2[None, :], g_post2[None, :], wg, wu, wd)


def _rope_tables(t):
    pos = jnp.arange(t, dtype=jnp.int32)
    row = (pos // GRID_W).astype(F32)
    colp = (pos % GRID_W).astype(F32)
    inv_freq = ROPE_THETA ** (-jnp.arange(ROPE_PAIRS, dtype=F32) / ROPE_PAIRS)
    ang_r = row[:, None] * inv_freq
    ang_c = colp[:, None] * inv_freq
    ang = jnp.concatenate([ang_r, ang_r, ang_c, ang_c], axis=-1)
    cos, sin = jnp.cos(ang), jnp.sin(ang)
    first = (jnp.arange(HEAD_DIM) % (2 * ROPE_PAIRS)) < ROPE_PAIRS
    sin_lo = jnp.where(first, -sin, 0.0)
    sin_hi = jnp.where(first, 0.0, sin)
    two = lambda a: jnp.concatenate([a, a], axis=-1)
    return two(cos), two(sin_lo), two(sin_hi)


def kernel(x, c, ctx, c_ctx, w_mod, b_mod, norm_pre, norm_post, ffn_w_gate, ffn_w_up, ffn_w_down,
           w_in, q_norm, k_norm, hg_lower_bound, hg_norm, w_att_out, w_hg_out, w_o):
    assert w_in.shape[0] == 1, "single-layer block"
    bsz, t, d = x.shape
    tc = ctx.shape[1]
    assert t % GRID_W == 0 and t % CHUNK == 0 and tc % CHUNK == 0
    tm = min(256, t)
    tmc = min(256, tc)
    tmf = 512 if t % 512 == 0 else tm

    rows = -(-(bsz + 1) // 8) * 8
    cvec = jnp.concatenate([c, c_ctx[None, :], jnp.zeros((rows - bsz - 1, d), c.dtype)], axis=0)
    mods = _modulation(cvec, w_mod[0], b_mod[0])[:, None, :]
    lat_row = lambda b: b
    ctx_row = lambda b: bsz

    wg, wu, wd = ffn_w_gate[0].astype(BF16), ffn_w_up[0].astype(BF16), ffn_w_down[0].astype(BF16)
    w_in_b = w_in[0].astype(BF16)

    x1 = _ffn(x, mods, lat_row, 0, norm_pre[0, 0], norm_post[0, 0], wg[0], wu[0], wd[0], tmf)
    h1 = _ffn(ctx, mods, ctx_row, 0, norm_pre[0, 0], norm_post[0, 0], wg[0], wu[0], wd[0], tmc)

    cos, slo, shi = _rope_tables(t)
    q_gain2 = jnp.concatenate([q_norm[0], q_norm[0]])[None, :]
    k_gain2 = jnp.concatenate([k_norm[0], k_norm[0]])[None, :]
    slots = hg_lower_bound.shape[1]
    lb_raw = jnp.transpose(hg_lower_bound.astype(F32), (1, 0, 2)).reshape(slots, 2 * HG_WIDTH)
    q, k, v, hq, hv, ff, og, mg = _inproj_latent(x1, mods, norm_pre[0, 1], w_in_b, q_gain2, k_gain2, lb_raw,
                                                  cos, slo, shi, tm)
    kv0 = ATT_WIDTH
    h0 = ATT_WIDTH + 2 * KV_WIDTH + HG_WIDTH
    ck, cv, chv, cff = _inproj_ctx(h1, mods, bsz, norm_pre[0, 1], w_in_b[:, kv0:kv0 + 2 * KV_WIDTH],
                                   w_in_b[:, h0:h0 + 3 * HG_WIDTH], k_gain2, lb_raw, tmc)

    k_all = jnp.concatenate([k, ck], axis=1).reshape(bsz, t + tc, N_KV_HEADS, HEAD_DIM)
    v_all = jnp.concatenate([v, cv], axis=1).reshape(bsz, t + tc, N_KV_HEADS, HEAD_DIM)
    kh = jnp.transpose(k_all, (0, 2, 1, 3))
    vt = jnp.transpose(v_all, (0, 2, 3, 1))
    pad_rows = jnp.zeros((bsz, N_KV_HEADS, VT_PAD, t + tc), BF16).at[:, :, 0, :].set(1.0)
    o_att = _attention(q, kh, jnp.concatenate([vt, pad_rows], axis=2), tm, tmf)

    o_hg = _hgrn(hg_norm[0], hq, hv, ff, og, chv, cff)

    return _merge_ffn(x1, mods, o_att, o_hg, mg, norm_post[0, 1], w_att_out[0].astype(BF16),
                      w_hg_out[0].astype(BF16), w_o[0].astype(BF16),
                      norm_pre[0, 2], norm_post[0, 2], wg[1], wu[1], wd[1], tm)
```

```python
import functools

import jax
import jax.numpy as jnp
from jax import lax
from jax.experimental import pallas as pl
from jax.experimental.pallas import tpu as pltpu

EPS = 1e-6
N_MOD = 9
GRID_W = 64
ROPE_THETA = 10000.0
HEAD_DIM = 64
N_Q_HEADS = 8
N_KV_HEADS = 2
GROUP = N_Q_HEADS // N_KV_HEADS
ATT_WIDTH = N_Q_HEADS * HEAD_DIM
KV_WIDTH = N_KV_HEADS * HEAD_DIM
ROPE_PAIRS = HEAD_DIM // 4
ATT_SCALE = HEAD_DIM ** -0.5
LOG2E = 1.4426950408889634
VT_PAD = 16
HG_HEADS = 4
HG_DK = 128
HG_DV = 128
HG_WIDTH = HG_HEADS * HG_DK
HG_SCALE = HG_DK ** -0.5
CHUNK = 64
LANES = 128
VMEM_LIMIT = 56 * 1024 * 1024

BF16 = jnp.bfloat16
F32 = jnp.float32

NT_DIMS = (((1,), (1,)), ((), ()))
TN_DIMS = (((0,), (0,)), ((), ()))


def _dot(a, b):
    return jnp.dot(a, b, preferred_element_type=F32)


def _rms(x, gain):
    return x * lax.rsqrt(jnp.mean(x * x, axis=-1, keepdims=True) + EPS) * gain


def _sigmoid(x):
    return 1.0 / (1.0 + jnp.exp(-x))


def _silu(x):
    return x * _sigmoid(x)


def _params(n_grid):
    return pltpu.CompilerParams(dimension_semantics=("parallel",) * n_grid, vmem_limit_bytes=VMEM_LIMIT)


def _const_spec(shape):
    nd = len(shape)
    return pl.BlockSpec(shape, lambda *_: (0,) * nd, pipeline_mode=pl.Buffered(1))


def _mod_kernel(c_ref, w_ref, b_ref, o_ref):
    a = _silu(c_ref[...]).astype(BF16)
    o_ref[...] = _dot(a, w_ref[...].astype(BF16)) + b_ref[...]


def _modulation(cvec, w_mod, b_mod, tn=1024):
    rows, d = cvec.shape
    n = w_mod.shape[1]
    return pl.pallas_call(
        _mod_kernel,
        grid=(n // tn,),
        in_specs=[pl.BlockSpec((rows, d), lambda j: (0, 0)),
                  pl.BlockSpec((d, tn), lambda j: (0, j)),
                  pl.BlockSpec((1, tn), lambda j: (0, j))],
        out_specs=pl.BlockSpec((rows, tn), lambda j: (0, j)),
        out_shape=jax.ShapeDtypeStruct((rows, n), F32),
        compiler_params=_params(1),
        name="mod",
    )(cvec, w_mod, b_mod[None, :])


def _ffn_half_step(x, m_ref, gpre_ref, gpost_ref, wg_ref, wu_ref, wd_ref, mod0, d):
    shift = m_ref[:, (mod0 + 0) * d:(mod0 + 1) * d]
    scale = m_ref[:, (mod0 + 1) * d:(mod0 + 2) * d]
    gate = m_ref[:, (mod0 + 2) * d:(mod0 + 3) * d]
    u = (_rms(x, gpre_ref[...]) * (1.0 + scale) + shift).astype(BF16)
    h = (_silu(_dot(u, wg_ref[...])) * _dot(u, wu_ref[...])).astype(BF16)
    y = _dot(h, wd_ref[...])
    return x + 0.5 * (gate * _rms(y, gpost_ref[...]))


def _ffn_kernel(x_ref, m_ref, gpre_ref, gpost_ref, wg_ref, wu_ref, wd_ref, o_ref, *, mod0, d):
    o_ref[...] = _ffn_half_step(x_ref[...], m_ref, gpre_ref, gpost_ref, wg_ref, wu_ref, wd_ref, mod0, d)


def _ffn(x, mods, mod_row, mod0, g_pre, g_post, wg, wu, wd, tm):
    bsz, t, d = x.shape
    f = wg.shape[1]
    nm = mods.shape[-1]
    return pl.pallas_call(
        functools.partial(_ffn_kernel, mod0=mod0, d=d),
        grid=(bsz, t // tm),
        in_specs=[pl.BlockSpec((None, tm, d), lambda b, i: (b, i, 0)),
                  pl.BlockSpec((None, 1, nm), lambda b, i: (mod_row(b), 0, 0)),
                  _const_spec((1, d)), _const_spec((1, d)),
                  _const_spec((d, f)), _const_spec((d, f)), _const_spec((f, d))],
        out_specs=pl.BlockSpec((None, tm, d), lambda b, i: (b, i, 0)),
        out_shape=jax.ShapeDtypeStruct((bsz, t, d), F32),
        compiler_params=_params(2),
        name="ffn",
    )(x, mods, g_pre[None, :], g_post[None, :], wg, wu, wd)


def _log2_forget(raw, lbraw_ref):
    slots = [lbraw_ref[s:s + 1, :] for s in range(lbraw_ref.shape[0])]
    top = functools.reduce(jnp.maximum, slots)
    e = [jnp.exp(s - top) for s in slots]
    lb = e[0] / functools.reduce(jnp.add, e)
    return jnp.log2(lb + (1.0 - lb) * _sigmoid(raw))


def _head_rms64(z, gain):
    lane = lax.broadcasted_iota(jnp.int32, (1, LANES), 1)
    first = lane < HEAD_DIM
    sq = z * z
    lo = jnp.sum(jnp.where(first, sq, 0.0), axis=-1, keepdims=True)
    hi = jnp.sum(jnp.where(first, 0.0, sq), axis=-1, keepdims=True)
    ms = jnp.where(first, lo, hi) * (1.0 / HEAD_DIM)
    return z * lax.rsqrt(ms + EPS) * gain


def _rope128(z, cos, sin_lo, sin_hi):
    q = ROPE_PAIRS
    return z * cos + pltpu.roll(z, LANES - q, 1) * sin_lo + pltpu.roll(z, q, 1) * sin_hi


def _inproj_latent_kernel(x_ref, m_ref, gpre_ref, w_ref, qg_ref, kg_ref, lbraw_ref, cos_ref, slo_ref, shi_ref,
                          q_ref, k_ref, v_ref, hq_ref, hv_ref, ff_ref, og_ref, mg_ref, *, d):
    x = x_ref[...]
    shift = m_ref[:, 3 * d:4 * d]
    scale = m_ref[:, 4 * d:5 * d]
    u = (_rms(x, gpre_ref[...]) * (1.0 + scale) + shift).astype(BF16)
    cos, slo, shi = cos_ref[...], slo_ref[...], shi_ref[...]

    def proj(lo, hi):
        return _dot(u, w_ref[:, lo:hi])

    c0 = 0
    pq = proj(c0, c0 + ATT_WIDTH)
    qs = []
    for j in range(ATT_WIDTH // LANES):
        z = _head_rms64(pq[:, j * LANES:(j + 1) * LANES], qg_ref[...])
        qs.append(_rope128(z, cos, slo, shi) * (ATT_SCALE * LOG2E))
    q_ref[...] = jnp.concatenate(qs, axis=-1).astype(BF16)
    c0 += ATT_WIDTH
    pkv = proj(c0, c0 + 2 * KV_WIDTH)
    k_ref[...] = _rope128(_head_rms64(pkv[:, :KV_WIDTH], kg_ref[...]), cos, slo, shi).astype(BF16)
    v_ref[...] = pkv[:, KV_WIDTH:].astype(BF16)
    c0 += 2 * KV_WIDTH
    hq_ref[...] = _silu(proj(c0, c0 + HG_WIDTH)) * HG_SCALE
    c0 += HG_WIDTH
    hv_ref[...] = proj(c0, c0 + HG_WIDTH).astype(BF16)
    c0 += HG_WIDTH
    ff_ref[...] = _log2_forget(proj(c0, c0 + 2 * HG_WIDTH), lbraw_ref)
    c0 += 2 * HG_WIDTH
    og_ref[...] = proj(c0, c0 + HG_WIDTH)
    c0 += HG_WIDTH
    mg_ref[...] = proj(c0, c0 + 2 * d)


def _inproj_latent(x, mods, g_pre, w_in, q_gain2, k_gain2, lb_raw, cos, slo, shi, tm):
    bsz, t, d = x.shape
    nm = mods.shape[-1]
    n_in = w_in.shape[1]

    def tile(width, dtype):
        return (pl.BlockSpec((None, tm, width), lambda b, i: (b, i, 0)),
                jax.ShapeDtypeStruct((bsz, t, width), dtype))

    outs = [tile(ATT_WIDTH, BF16), tile(KV_WIDTH, BF16), tile(KV_WIDTH, BF16), tile(HG_WIDTH, F32),
            tile(HG_WIDTH, BF16), tile(2 * HG_WIDTH, F32), tile(HG_WIDTH, F32), tile(2 * d, F32)]
    rope_spec = pl.BlockSpec((tm, LANES), lambda b, i: (i, 0))
    return pl.pallas_call(
        functools.partial(_inproj_latent_kernel, d=d),
        grid=(bsz, t // tm),
        in_specs=[pl.BlockSpec((None, tm, d), lambda b, i: (b, i, 0)),
                  pl.BlockSpec((None, 1, nm), lambda b, i: (b, 0, 0)),
                  _const_spec((1, d)), _const_spec((d, n_in)),
                  _const_spec((1, LANES)), _const_spec((1, LANES)), _const_spec(lb_raw.shape),
                  rope_spec, rope_spec, rope_spec],
        out_specs=[o[0] for o in outs],
        out_shape=[o[1] for o in outs],
        compiler_params=_params(2),
        name="inproj_latent",
    )(x, mods, g_pre[None, :], w_in, q_gain2, k_gain2, lb_raw, cos, slo, shi)


def _inproj_ctx_kernel(x_ref, m_ref, gpre_ref, wkv_ref, wh_ref, kg_ref, lbraw_ref,
                       k_ref, v_ref, hv_ref, ff_ref, *, d):
    x = x_ref[...]
    shift = m_ref[:, 3 * d:4 * d]
    scale = m_ref[:, 4 * d:5 * d]
    u = (_rms(x, gpre_ref[...]) * (1.0 + scale) + shift).astype(BF16)
    k_ref[...] = _head_rms64(_dot(u, wkv_ref[:, :KV_WIDTH]), kg_ref[...]).astype(BF16)
    v_ref[...] = _dot(u, wkv_ref[:, KV_WIDTH:]).astype(BF16)
    hv_ref[...] = _dot(u, wh_ref[:, :HG_WIDTH]).astype(BF16)
    ff_ref[...] = _log2_forget(_dot(u, wh_ref[:, HG_WIDTH:]), lbraw_ref)


def _inproj_ctx(x, mods, ctx_row, g_pre, w_kv, w_h, k_gain2, lb_raw, tm):
    bsz, t, d = x.shape
    nm = mods.shape[-1]

    def tile(width, dtype):
        return (pl.BlockSpec((None, tm, width), lambda b, i: (b, i, 0)),
                jax.ShapeDtypeStruct((bsz, t, width), dtype))

    outs = [tile(KV_WIDTH, BF16), tile(KV_WIDTH, BF16), tile(HG_WIDTH, BF16), tile(2 * HG_WIDTH, F32)]
    return pl.pallas_call(
        functools.partial(_inproj_ctx_kernel, d=d),
        grid=(bsz, t // tm),
        in_specs=[pl.BlockSpec((None, tm, d), lambda b, i: (b, i, 0)),
                  pl.BlockSpec((None, 1, nm), lambda b, i: (ctx_row, 0, 0)),
                  _const_spec((1, d)), _const_spec(w_kv.shape), _const_spec(w_h.shape),
                  _const_spec((1, LANES)), _const_spec(lb_raw.shape)],
        out_specs=[o[0] for o in outs],
        out_shape=[o[1] for o in outs],
        compiler_params=_params(2),
        name="inproj_ctx",
    )(x, mods, g_pre[None, :], w_kv, w_h, k_gain2, lb_raw)


def _attn_kernel(q_ref, k_ref, vt_ref, o_ref, st_ref, pt_ref):
    tq = st_ref.shape[2]
    n_items = (q_ref.shape[0] // tq) * N_Q_HEADS

    def scores(i):
        r, h = divmod(i, N_Q_HEADS)
        q = q_ref[r * tq:(r + 1) * tq, h * HEAD_DIM:(h + 1) * HEAD_DIM]
        q = jnp.concatenate([q, jnp.zeros_like(q)], axis=1)
        st = lax.dot_general(k_ref[h // GROUP], q, NT_DIMS, preferred_element_type=F32)
        st_ref[i % st_ref.shape[0]] = st
        return jnp.max(st, axis=0, keepdims=True)

    pairs = []

    def values(i):
        h = i % N_Q_HEADS
        ot = _dot(vt_ref[h // GROUP], pt_ref[i % 2])
        pairs.append(ot[:HEAD_DIM, :] / ot[HEAD_DIM:HEAD_DIM + 1, :])

    ahead = st_ref.shape[0] - 1
    ms = [scores(i) for i in range(ahead)]
    for i in range(n_items):
        if i + ahead < n_items:
            ms.append(scores(i + ahead))
        pt_ref[i % 2] = jnp.exp2(st_ref[i % (ahead + 1)] - ms[i]).astype(BF16)
        if i:
            values(i - 1)
    values(n_items - 1)
    for r in range(n_items // N_Q_HEADS):
        heads = pairs[r * N_Q_HEADS:(r + 1) * N_Q_HEADS]
        outs = [jnp.concatenate(heads[j:j + 2], axis=0).T for j in range(0, N_Q_HEADS, 2)]
        o_ref[r * tq:(r + 1) * tq, :] = jnp.concatenate(outs, axis=-1).astype(BF16)


def _attention(q, k, vt, tq, tstep):
    bsz, t, _ = q.shape
    s = k.shape[2]
    vrows = vt.shape[2]
    return pl.pallas_call(
        _attn_kernel,
        grid=(bsz, t // tstep),
        in_specs=[pl.BlockSpec((None, tstep, ATT_WIDTH), lambda b, i: (b, i, 0)),
                  pl.BlockSpec((None, N_KV_HEADS, s, LANES), lambda b, i: (b, 0, 0, 0)),
                  pl.BlockSpec((None, N_KV_HEADS, vrows, s), lambda b, i: (b, 0, 0, 0))],
        out_specs=pl.BlockSpec((None, tstep, ATT_WIDTH), lambda b, i: (b, i, 0)),
        out_shape=jax.ShapeDtypeStruct((bsz, t, ATT_WIDTH), BF16),
        scratch_shapes=[pltpu.VMEM((3, s, tq), F32), pltpu.VMEM((2, s, tq), BF16)],
        compiler_params=_params(2),
        name="attn",
    )(q, k, vt)


def _split3(g):
    g1 = g.astype(BF16)
    r1 = g - g1.astype(F32)
    g2 = r1.astype(BF16)
    g3 = (r1 - g2.astype(F32)).astype(BF16)
    return g1, g2, g3


def _hgrn_bidir_kernel(gain_ref, hq_ref, hv_ref, ff_ref, fb_ref, og_ref, cv_ref, cff_ref, cfb_ref,
                       o_ref, acc_ref, qe_ref, ds_ref, dec_ref, st_ref, bk_ref, ke_ref, a_ref,
                       *, n_lat, n_ctx, cpb):
    c = CHUNK
    r = cpb * c
    dk = HG_DK
    row = lax.broadcasted_iota(jnp.int32, (r, r), 0)
    col = lax.broadcasted_iota(jnp.int32, (r, r), 1)
    same_chunk = (row // c) == (col // c)
    masks = (same_chunk & (col <= row), same_chunk & (col >= row))
    tris = tuple(jnp.where(m, 1.0, 0.0).astype(BF16) for m in masks)
    last = (c - 1, 0)
    mid = (c // 2 - 1, c // 2)

    def per_chunk_rows(x, off):
        return jnp.concatenate([jnp.broadcast_to(x[j * c + off:j * c + off + 1, :], (c, x.shape[1]))
                                for j in range(cpb)], axis=0)

    def lane_block(d, j):
        return slice((d * cpb + j) * dk, (d * cpb + j + 1) * dk)

    ke_ref[...] = jnp.zeros(ke_ref.shape, BF16)
    qe_ref[...] = jnp.zeros(qe_ref.shape, BF16)

    blocks = ([((cff_ref, cfb_ref), cv_ref, None, i * r, i * cpb) for i in range(n_ctx // cpb)]
              + [((ff_ref, fb_ref), hv_ref, hq_ref, i * r, n_ctx + i * cpb) for i in range(n_lat // cpb)])

    def decays(n):
        f_refs, _, _, r0, _ = blocks[n]
        for d in range(2):
            lf = f_refs[d][r0:r0 + r, :]
            g1, g2, g3 = _split3(lf)
            bb = _dot(tris[d], jnp.concatenate([g1, g2, g3], axis=1))
            bk_ref[n % 2, d] = bb[:, :dk] + bb[:, dk:2 * dk] + bb[:, 2 * dk:]
            bk_ref[n % 2, 2 + d] = 1.0 - jnp.exp2(lf)

    def scores(n):
        _, _, q_ref, r0, ch0 = blocks[n]
        q = None if q_ref is None else q_ref[r0:r0 + r, :]
        amat = None
        for d in range(2):
            b, k = bk_ref[n % 2, d], bk_ref[n % 2, 2 + d]
            b_last = per_chunk_rows(b, last[d])
            ke = (k * jnp.exp2(b_last - b)).astype(BF16)
            for j in range(cpb):
                ke_ref[n % 2, j * c:(j + 1) * c, lane_block(d, j)] = ke[j * c:(j + 1) * c, :]
                dec_ref[d, ch0 + j] = jnp.exp2(b[j * c + last[d]:j * c + last[d] + 1, :])
            if q is None:
                continue
            b_mid = per_chunk_rows(b, mid[d])
            qd = (q * jnp.exp2(b - b_mid)).astype(BF16)
            kd = (k * jnp.exp2(b_mid - b)).astype(BF16)
            a = jnp.where(masks[d], lax.dot_general(qd, kd, NT_DIMS, preferred_element_type=F32), 0.0)
            amat = a if amat is None else amat + a
            qe = (q * jnp.exp2(b)).astype(BF16)
            for j in range(cpb):
                qe_ref[r0 + j * c:r0 + (j + 1) * c, lane_block(d, j)] = qe[j * c:(j + 1) * c, :]
        if q is not None:
            a_ref[n % 2] = amat.astype(BF16)

    def products(n):
        _, v_ref, q_ref, r0, ch0 = blocks[n]
        v = v_ref[r0:r0 + r, :]
        ds = lax.dot_general(v, ke_ref[n % 2], TN_DIMS, preferred_element_type=F32)
        for d in range(2):
            for j in range(cpb):
                lo = (d * cpb + j) * dk
                ds_ref[d, ch0 + j] = ds[:, lo:lo + dk]
        if q_ref is not None:
            acc_ref[r0:r0 + r, :] = _dot(a_ref[n % 2], v)

    decays(0)
    for n in range(len(blocks)):
        if n + 1 < len(blocks):
            decays(n + 1)
        scores(n)
        if n:
            products(n - 1)
    products(len(blocks) - 1)

    for d in range(2):
        def ctx_step(i, st):
            ch = (n_ctx - 1 - i) if d else i
            return st * dec_ref[d, ch] + ds_ref[d, ch]

        def lat_steps(i, st):
            blk = (n_lat // cpb - 1 - i) if d else i
            for jj in range(cpb):
                j = (cpb - 1 - jj) if d else jj
                st_ref[blk, :, (d * cpb + j) * dk:(d * cpb + j + 1) * dk] = st.astype(BF16)
                ch = n_ctx + blk * cpb + j
                st = st * dec_ref[d, ch] + ds_ref[d, ch]
            return st

        st = lax.fori_loop(0, n_ctx, ctx_step, jnp.zeros((HG_DV, dk), F32), unroll=True)
        lax.fori_loop(0, n_lat // cpb, lat_steps, st)

    def pass3(i, carry):
        rows = pl.ds(pl.multiple_of(i * r, r), r)
        inter = lax.dot_general(qe_ref[rows, :], st_ref[i], NT_DIMS, preferred_element_type=F32)
        tot = acc_ref[rows, :] + inter
        o_ref[rows, :] = (_rms(tot, gain_ref[...]) * _silu(og_ref[rows, :])).astype(BF16)
        return carry

    lax.fori_loop(0, n_lat // cpb, pass3, 0, unroll=4)


def _hgrn(hg_gain, hq, hv, ff, og, cv, cff):
    bsz, t, _ = hq.shape
    tc = cv.shape[1]

    def col(tt, off=0):
        return pl.BlockSpec((None, tt, HG_DK), lambda b, h: (b, 0, h + off))

    n_lat, n_ctx = t // CHUNK, tc // CHUNK
    cpb = next(n for n in (4, 2, 1) if n_lat % n == 0 and n_ctx % n == 0)
    return pl.pallas_call(
        functools.partial(_hgrn_bidir_kernel, n_lat=n_lat, n_ctx=n_ctx, cpb=cpb),
        grid=(bsz, HG_HEADS),
        in_specs=[pl.BlockSpec((1, HG_DV), lambda b, h: (0, 0)),
                  col(t), col(t), col(t), col(t, HG_HEADS), col(t),
                  col(tc), col(tc), col(tc, HG_HEADS)],
        out_specs=col(t),
        out_shape=jax.ShapeDtypeStruct((bsz, t, HG_WIDTH), BF16),
        scratch_shapes=[pltpu.VMEM((t, HG_DV), F32),
                        pltpu.VMEM((t, 2 * cpb * HG_DK), BF16),
                        pltpu.VMEM((2, n_ctx + n_lat, HG_DV, HG_DK), F32),
                        pltpu.VMEM((2, n_ctx + n_lat, 1, HG_DK), F32),
                        pltpu.VMEM((n_lat // cpb, HG_DV, 2 * cpb * HG_DK), BF16),
                        pltpu.VMEM((2, 4, cpb * CHUNK, HG_DK), F32),
                        pltpu.VMEM((2, cpb * CHUNK, 2 * cpb * HG_DK), BF16),
                        pltpu.VMEM((2, cpb * CHUNK, cpb * CHUNK), BF16)],
        compiler_params=_params(2),
        name="hgrn",
    )(hg_gain[None, :], hq, hv, ff, ff, og, cv, cff, cff)


def _merge_ffn_kernel(x_ref, m_ref, oa_ref, oh_ref, mg_ref, gpost1_ref, wa_ref, wh_ref, wo_ref,
                      gpre2_ref, gpost2_ref, wg_ref, wu_ref, wd_ref, o_ref, *, d):
    gate = m_ref[:, 5 * d:6 * d]
    y = (_sigmoid(mg_ref[:, :d]) * _dot(oa_ref[...], wa_ref[...])
         + _sigmoid(mg_ref[:, d:]) * _dot(oh_ref[...], wh_ref[...]))
    z = _dot(y.astype(BF16), wo_ref[...])
    x2 = x_ref[...] + gate * _rms(z, gpost1_ref[...])
    o_ref[...] = _ffn_half_step(x2, m_ref, gpre2_ref, gpost2_ref, wg_ref, wu_ref, wd_ref, 6, d)


def _merge_ffn(x, mods, o_att, o_hg, mg, g_post1, wa, wh, wo, g_pre2, g_post2, wg, wu, wd, tm):
    bsz, t, d = x.shape
    nm = mods.shape[-1]

    def tile(width):
        return pl.BlockSpec((None, tm, width), lambda b, i: (b, i, 0))

    return pl.pallas_call(
        functools.partial(_merge_ffn_kernel, d=d),
        grid=(bsz, t // tm),
        in_specs=[tile(d), pl.BlockSpec((None, 1, nm), lambda b, i: (b, 0, 0)),
                  tile(ATT_WIDTH), tile(HG_WIDTH), tile(2 * d), _const_spec((1, d)),
                  _const_spec(wa.shape), _const_spec(wh.shape), _const_spec(wo.shape),
                  _const_spec((1, d)), _const_spec((1, d)),
                  _const_spec(wg.shape), _const_spec(wu.shape), _const_spec(wd.shape)],
        out_specs=tile(d),
        out_shape=jax.ShapeDtypeStruct((bsz, t, d), F32),
        compiler_params=_params(2),
        name="merge_ffn",
    )(x, mods, o_att, o_hg, mg, g_post1[None, :], wa, wh, wo, g_pre2[None, :], g_post2[None, :], wg, wu, wd)


def _rope_tables(t):
    pos = jnp.arange(t, dtype=jnp.int32)
    row = (pos // GRID_W).astype(F32)
    colp = (pos % GRID_W).astype(F32)
    inv_freq = ROPE_THETA ** (-jnp.arange(ROPE_PAIRS, dtype=F32) / ROPE_PAIRS)
    ang_r = row[:, None] * inv_freq
    ang_c = colp[:, None] * inv_freq
    ang = jnp.concatenate([ang_r, ang_r, ang_c, ang_c], axis=-1)
    cos, sin = jnp.cos(ang), jnp.sin(ang)
    first = (jnp.arange(HEAD_DIM) % (2 * ROPE_PAIRS)) < ROPE_PAIRS
    sin_lo = jnp.where(first, -sin, 0.0)
    sin_hi = jnp.where(first, 0.0, sin)
    two = lambda a: jnp.concatenate([a, a], axis=-1)
    return two(cos), two(sin_lo), two(sin_hi)


def kernel(x, c, ctx, c_ctx, w_mod, b_mod, norm_pre, norm_post, ffn_w_gate, ffn_w_up, ffn_w_down,
           w_in, q_norm, k_norm, hg_lower_bound, hg_norm, w_att_out, w_hg_out, w_o):
    assert w_in.shape[0] == 1, "single-layer block"
    bsz, t, d = x.shape
    tc = ctx.shape[1]
    assert t % GRID_W == 0 and t % CHUNK == 0 and tc % CHUNK == 0
    tm = min(256, t)
    tmc = min(256, tc)
    tmf = 512 if t % 512 == 0 else tm

    rows = -(-(bsz + 1) // 8) * 8
    cvec = jnp.concatenate([c, c_ctx[None, :], jnp.zeros((rows - bsz - 1, d), c.dtype)], axis=0)
    mods = _modulation(cvec, w_mod[0], b_mod[0])[:, None, :]
    lat_row = lambda b: b
    ctx_row = lambda b: bsz

    wg, wu, wd = ffn_w_gate[0].astype(BF16), ffn_w_up[0].astype(BF16), ffn_w_down[0].astype(BF16)
    w_in_b = w_in[0].astype(BF16)

    x1 = _ffn(x, mods, lat_row, 0, norm_pre[0, 0], norm_post[0, 0], wg[0], wu[0], wd[0], tmf)
    h1 = _ffn(ctx, mods, ctx_row, 0, norm_pre[0, 0], norm_post[0, 0], wg[0], wu[0], wd[0], tmc)

    cos, slo, shi = _rope_tables(t)
    q_gain2 = jnp.concatenate([q_norm[0], q_norm[0]])[None, :]
    k_gain2 = jnp.concatenate([k_norm[0], k_norm[0]])[None, :]
    slots = hg_lower_bound.shape[1]
    lb_raw = jnp.transpose(hg_lower_bound.astype(F32), (1, 0, 2)).reshape(slots, 2 * HG_WIDTH)
    q, k, v, hq, hv, ff, og, mg = _inproj_latent(x1, mods, norm_pre[0, 1], w_in_b, q_gain2, k_gain2, lb_raw,
                                                  cos, slo, shi, tmf)
    kv0 = ATT_WIDTH
    h0 = ATT_WIDTH + 2 * KV_WIDTH + HG_WIDTH
    ck, cv, chv, cff = _inproj_ctx(h1, mods, bsz, norm_pre[0, 1], w_in_b[:, kv0:kv0 + 2 * KV_WIDTH],
                                   w_in_b[:, h0:h0 + 3 * HG_WIDTH], k_gain2, lb_raw, tmc)

    k_all = jnp.concatenate([k, ck], axis=1).reshape(bsz, t + tc, N_KV_HEADS, HEAD_DIM)
    v_all = jnp.concatenate([v, cv], axis=1).reshape(bsz, t + tc, N_KV_HEADS, HEAD_DIM)
    kh = jnp.transpose(k_all, (0, 2, 1, 3))
    kh = jnp.concatenate([kh, jnp.zeros_like(kh)], axis=-1)
    vt = jnp.transpose(v_all, (0, 2, 3, 1))
    pad_rows = jnp.zeros((bsz, N_KV_HEADS, VT_PAD, t + tc), BF16).at[:, :, 0, :].set(1.0)
    o_att = _attention(q, kh, jnp.concatenate([vt, pad_rows], axis=2), tm, tmf)

    o_hg = _hgrn(hg_norm[0], hq, hv, ff, og, chv, cff)

    return _merge_ffn(x1, mods, o_att, o_hg, mg, norm_post[0, 1], w_att_out[0].astype(BF16),
                      w_hg_out[0].astype(BF16), w_o[0].astype(BF16),
                      norm_pre[0, 2], norm_post[0, 2], wg[1], wu[1], wd[1], tm)
```

```python
import functools

import jax
import jax.numpy as jnp
from jax import lax
from jax.experimental import pallas as pl
from jax.experimental.pallas import tpu as pltpu

EPS = 1e-6
N_MOD = 9
GRID_W = 64
ROPE_THETA = 10000.0
HEAD_DIM = 64
N_Q_HEADS = 8
N_KV_HEADS = 2
GROUP = N_Q_HEADS // N_KV_HEADS
ATT_WIDTH = N_Q_HEADS * HEAD_DIM
KV_WIDTH = N_KV_HEADS * HEAD_DIM
ROPE_PAIRS = HEAD_DIM // 4
ATT_SCALE = HEAD_DIM ** -0.5
LOG2E = 1.4426950408889634
VT_PAD = 16
HG_HEADS = 4
HG_DK = 128
HG_DV = 128
HG_WIDTH = HG_HEADS * HG_DK
HG_SCALE = HG_DK ** -0.5
CHUNK = 64
LANES = 128
SUB_TILE = 256
VMEM_LIMIT = 56 * 1024 * 1024

BF16 = jnp.bfloat16
F32 = jnp.float32

NT_DIMS = (((1,), (1,)), ((), ()))
TN_DIMS = (((0,), (0,)), ((), ()))


def _dot(a, b):
    return jnp.dot(a, b, preferred_element_type=F32)


def _rms(x, gain):
    return x * lax.rsqrt(jnp.mean(x * x, axis=-1, keepdims=True) + EPS) * gain


def _sigmoid(x):
    return 1.0 / (1.0 + jnp.exp(-x))


def _silu(x):
    return x * _sigmoid(x)


def _params(n_grid):
    return pltpu.CompilerParams(dimension_semantics=("parallel",) * n_grid, vmem_limit_bytes=VMEM_LIMIT)


def _const_spec(shape):
    nd = len(shape)
    return pl.BlockSpec(shape, lambda *_: (0,) * nd, pipeline_mode=pl.Buffered(1))


def _mod_kernel(c_ref, w_ref, b_ref, o_ref):
    a = _silu(c_ref[...]).astype(BF16)
    o_ref[...] = _dot(a, w_ref[...].astype(BF16)) + b_ref[...]


def _modulation(cvec, w_mod, b_mod, tn=1024):
    rows, d = cvec.shape
    n = w_mod.shape[1]
    return pl.pallas_call(
        _mod_kernel,
        grid=(n // tn,),
        in_specs=[pl.BlockSpec((rows, d), lambda j: (0, 0)),
                  pl.BlockSpec((d, tn), lambda j: (0, j)),
                  pl.BlockSpec((1, tn), lambda j: (0, j))],
        out_specs=pl.BlockSpec((rows, tn), lambda j: (0, j)),
        out_shape=jax.ShapeDtypeStruct((rows, n), F32),
        compiler_params=_params(1),
        name="mod",
    )(cvec, w_mod, b_mod[None, :])


def _ffn_half_step(x, m_ref, gpre_ref, gpost_ref, wg_ref, wu_ref, wd_ref, mod0, d):
    shift = m_ref[:, (mod0 + 0) * d:(mod0 + 1) * d]
    scale = m_ref[:, (mod0 + 1) * d:(mod0 + 2) * d]
    gate = m_ref[:, (mod0 + 2) * d:(mod0 + 3) * d]
    u = (_rms(x, gpre_ref[...]) * (1.0 + scale) + shift).astype(BF16)
    h = (_silu(_dot(u, wg_ref[...])) * _dot(u, wu_ref[...])).astype(BF16)
    y = _dot(h, wd_ref[...])
    return x + 0.5 * (gate * _rms(y, gpost_ref[...]))


def _sub_tiles(rows):
    sub = SUB_TILE if rows % SUB_TILE == 0 else rows
    return [slice(r, r + sub) for r in range(0, rows, sub)]


def _ffn_kernel(x_ref, m_ref, gpre_ref, gpost_ref, wg_ref, wu_ref, wd_ref, o_ref, *, mod0, d):
    for rows in _sub_tiles(x_ref.shape[0]):
        o_ref[rows, :] = _ffn_half_step(x_ref[rows, :], m_ref, gpre_ref, gpost_ref, wg_ref, wu_ref, wd_ref,
                                        mod0, d)


def _ffn(x, mods, mod_row, mod0, g_pre, g_post, wg, wu, wd, tm):
    bsz, t, d = x.shape
    f = wg.shape[1]
    nm = mods.shape[-1]
    return pl.pallas_call(
        functools.partial(_ffn_kernel, mod0=mod0, d=d),
        grid=(bsz, t // tm),
        in_specs=[pl.BlockSpec((None, tm, d), lambda b, i: (b, i, 0)),
                  pl.BlockSpec((None, 1, nm), lambda b, i: (mod_row(b), 0, 0)),
                  _const_spec((1, d)), _const_spec((1, d)),
                  _const_spec((d, f)), _const_spec((d, f)), _const_spec((f, d))],
        out_specs=pl.BlockSpec((None, tm, d), lambda b, i: (b, i, 0)),
        out_shape=jax.ShapeDtypeStruct((bsz, t, d), F32),
        compiler_params=_params(2),
        name="ffn",
    )(x, mods, g_pre[None, :], g_post[None, :], wg, wu, wd)


def _log2_forget(raw, lbraw_ref):
    slots = [lbraw_ref[s:s + 1, :] for s in range(lbraw_ref.shape[0])]
    top = functools.reduce(jnp.maximum, slots)
    e = [jnp.exp(s - top) for s in slots]
    lb = e[0] / functools.reduce(jnp.add, e)
    return jnp.log2(lb + (1.0 - lb) * _sigmoid(raw))


def _head_rms64(z, gain):
    lane = lax.broadcasted_iota(jnp.int32, (1, LANES), 1)
    first = lane < HEAD_DIM
    sq = z * z
    lo = jnp.sum(jnp.where(first, sq, 0.0), axis=-1, keepdims=True)
    hi = jnp.sum(jnp.where(first, 0.0, sq), axis=-1, keepdims=True)
    ms = jnp.where(first, lo, hi) * (1.0 / HEAD_DIM)
    return z * lax.rsqrt(ms + EPS) * gain


def _rope128(z, cos, sin_lo, sin_hi):
    q = ROPE_PAIRS
    return z * cos + pltpu.roll(z, LANES - q, 1) * sin_lo + pltpu.roll(z, q, 1) * sin_hi


def _inproj_latent_kernel(x_ref, m_ref, gpre_ref, w_ref, qg_ref, kg_ref, lbraw_ref, cos_ref, slo_ref, shi_ref,
                          q_ref, k_ref, v_ref, hq_ref, hv_ref, ff_ref, og_ref, mg_ref, *, d):
    shift = m_ref[:, 3 * d:4 * d]
    scale = m_ref[:, 4 * d:5 * d]
    for rows in _sub_tiles(x_ref.shape[0]):
        u = (_rms(x_ref[rows, :], gpre_ref[...]) * (1.0 + scale) + shift).astype(BF16)
        cos, slo, shi = cos_ref[rows, :], slo_ref[rows, :], shi_ref[rows, :]

        def proj(lo, hi):
            return _dot(u, w_ref[:, lo:hi])

        c0 = 0
        pq = proj(c0, c0 + ATT_WIDTH)
        qs = []
        for j in range(ATT_WIDTH // LANES):
            z = _head_rms64(pq[:, j * LANES:(j + 1) * LANES], qg_ref[...])
            qs.append(_rope128(z, cos, slo, shi) * (ATT_SCALE * LOG2E))
        q_ref[rows, :] = jnp.concatenate(qs, axis=-1).astype(BF16)
        c0 += ATT_WIDTH
        pkv = proj(c0, c0 + 2 * KV_WIDTH)
        k_ref[rows, :] = _rope128(_head_rms64(pkv[:, :KV_WIDTH], kg_ref[...]), cos, slo, shi).astype(BF16)
        v_ref[rows, :] = pkv[:, KV_WIDTH:].astype(BF16)
        c0 += 2 * KV_WIDTH
        hq_ref[rows, :] = _silu(proj(c0, c0 + HG_WIDTH)) * HG_SCALE
        c0 += HG_WIDTH
        hv_ref[rows, :] = proj(c0, c0 + HG_WIDTH).astype(BF16)
        c0 += HG_WIDTH
        ff_ref[rows, :] = _log2_forget(proj(c0, c0 + 2 * HG_WIDTH), lbraw_ref)
        c0 += 2 * HG_WIDTH
        og_ref[rows, :] = proj(c0, c0 + HG_WIDTH)
        c0 += HG_WIDTH
        mg_ref[rows, :] = proj(c0, c0 + 2 * d)


def _inproj_latent(x, mods, g_pre, w_in, q_gain2, k_gain2, lb_raw, cos, slo, shi, tm):
    bsz, t, d = x.shape
    nm = mods.shape[-1]
    n_in = w_in.shape[1]

    def tile(width, dtype):
        return (pl.BlockSpec((None, tm, width), lambda b, i: (b, i, 0)),
                jax.ShapeDtypeStruct((bsz, t, width), dtype))

    outs = [tile(ATT_WIDTH, BF16), tile(KV_WIDTH, BF16), tile(KV_WIDTH, BF16), tile(HG_WIDTH, F32),
            tile(HG_WIDTH, BF16), tile(2 * HG_WIDTH, F32), tile(HG_WIDTH, F32), tile(2 * d, F32)]
    rope_spec = pl.BlockSpec((tm, LANES), lambda b, i: (i, 0))
    return pl.pallas_call(
        functools.partial(_inproj_latent_kernel, d=d),
        grid=(bsz, t // tm),
        in_specs=[pl.BlockSpec((None, tm, d), lambda b, i: (b, i, 0)),
                  pl.BlockSpec((None, 1, nm), lambda b, i: (b, 0, 0)),
                  _const_spec((1, d)), _const_spec((d, n_in)),
                  _const_spec((1, LANES)), _const_spec((1, LANES)), _const_spec(lb_raw.shape),
                  rope_spec, rope_spec, rope_spec],
        out_specs=[o[0] for o in outs],
        out_shape=[o[1] for o in outs],
        compiler_params=_params(2),
        name="inproj_latent",
    )(x, mods, g_pre[None, :], w_in, q_gain2, k_gain2, lb_raw, cos, slo, shi)


def _inproj_ctx_kernel(x_ref, m_ref, gpre_ref, wkv_ref, wh_ref, kg_ref, lbraw_ref,
                       k_ref, v_ref, hv_ref, ff_ref, *, d):
    x = x_ref[...]
    shift = m_ref[:, 3 * d:4 * d]
    scale = m_ref[:, 4 * d:5 * d]
    u = (_rms(x, gpre_ref[...]) * (1.0 + scale) + shift).astype(BF16)
    k_ref[...] = _head_rms64(_dot(u, wkv_ref[:, :KV_WIDTH]), kg_ref[...]).astype(BF16)
    v_ref[...] = _dot(u, wkv_ref[:, KV_WIDTH:]).astype(BF16)
    hv_ref[...] = _dot(u, wh_ref[:, :HG_WIDTH]).astype(BF16)
    ff_ref[...] = _log2_forget(_dot(u, wh_ref[:, HG_WIDTH:]), lbraw_ref)


def _inproj_ctx(x, mods, ctx_row, g_pre, w_kv, w_h, k_gain2, lb_raw, tm):
    bsz, t, d = x.shape
    nm = mods.shape[-1]

    def tile(width, dtype):
        return (pl.BlockSpec((None, tm, width), lambda b, i: (b, i, 0)),
                jax.ShapeDtypeStruct((bsz, t, width), dtype))

    outs = [tile(KV_WIDTH, BF16), tile(KV_WIDTH, BF16), tile(HG_WIDTH, BF16), tile(2 * HG_WIDTH, F32)]
    return pl.pallas_call(
        functools.partial(_inproj_ctx_kernel, d=d),
        grid=(bsz, t // tm),
        in_specs=[pl.BlockSpec((None, tm, d), lambda b, i: (b, i, 0)),
                  pl.BlockSpec((None, 1, nm), lambda b, i: (ctx_row, 0, 0)),
                  _const_spec((1, d)), _const_spec(w_kv.shape), _const_spec(w_h.shape),
                  _const_spec((1, LANES)), _const_spec(lb_raw.shape)],
        out_specs=[o[0] for o in outs],
        out_shape=[o[1] for o in outs],
        compiler_params=_params(2),
        name="inproj_ctx",
    )(x, mods, g_pre[None, :], w_kv, w_h, k_gain2, lb_raw)


def _attn_kernel(q_ref, k_ref, vt_ref, o_ref, st_ref, pt_ref):
    tq = st_ref.shape[2]
    n_items = (q_ref.shape[0] // tq) * N_Q_HEADS

    def scores(i):
        r, h = divmod(i, N_Q_HEADS)
        q = q_ref[r * tq:(r + 1) * tq, h * HEAD_DIM:(h + 1) * HEAD_DIM]
        st = lax.dot_general(k_ref[h // GROUP], q, NT_DIMS, preferred_element_type=F32)
        st_ref[i % st_ref.shape[0]] = st
        return jnp.max(st, axis=0, keepdims=True)

    pairs = []

    def values(i):
        h = i % N_Q_HEADS
        ot = _dot(vt_ref[h // GROUP], pt_ref[i % 2])
        pairs.append(ot[:HEAD_DIM, :] / ot[HEAD_DIM:HEAD_DIM + 1, :])

    ahead = st_ref.shape[0] - 1
    ms = [scores(i) for i in range(ahead)]
    for i in range(n_items):
        if i + ahead < n_items:
            ms.append(scores(i + ahead))
        pt_ref[i % 2] = jnp.exp2(st_ref[i % (ahead + 1)] - ms[i]).astype(BF16)
        if i:
            values(i - 1)
    values(n_items - 1)
    for r in range(n_items // N_Q_HEADS):
        heads = pairs[r * N_Q_HEADS:(r + 1) * N_Q_HEADS]
        outs = [jnp.concatenate(heads[j:j + 2], axis=0).T for j in range(0, N_Q_HEADS, 2)]
        o_ref[r * tq:(r + 1) * tq, :] = jnp.concatenate(outs, axis=-1).astype(BF16)


def _attention(q, k, vt, tq, tstep):
    bsz, t, _ = q.shape
    s = k.shape[2]
    vrows = vt.shape[2]
    return pl.pallas_call(
        _attn_kernel,
        grid=(bsz, t // tstep),
        in_specs=[pl.BlockSpec((None, tstep, ATT_WIDTH), lambda b, i: (b, i, 0)),
                  pl.BlockSpec((None, N_KV_HEADS, s, HEAD_DIM), lambda b, i: (b, 0, 0, 0)),
                  pl.BlockSpec((None, N_KV_HEADS, vrows, s), lambda b, i: (b, 0, 0, 0))],
        out_specs=pl.BlockSpec((None, tstep, ATT_WIDTH), lambda b, i: (b, i, 0)),
        out_shape=jax.ShapeDtypeStruct((bsz, t, ATT_WIDTH), BF16),
        scratch_shapes=[pltpu.VMEM((3, s, tq), F32), pltpu.VMEM((2, s, tq), BF16)],
        compiler_params=_params(2),
        name="attn",
    )(q, k, vt)


def _split3(g):
    g1 = g.astype(BF16)
    r1 = g - g1.astype(F32)
    g2 = r1.astype(BF16)
    g3 = (r1 - g2.astype(F32)).astype(BF16)
    return g1, g2, g3


def _hgrn_bidir_kernel(gain_ref, hq_ref, hv_ref, ff_ref, fb_ref, og_ref, cv_ref, cff_ref, cfb_ref,
                       o_ref, acc_ref, qe_ref, ds_ref, dec_ref, st_ref, bk_ref, ke_ref, a_ref,
                       *, n_lat, n_ctx, cpb):
    c = CHUNK
    r = cpb * c
    dk = HG_DK
    row = lax.broadcasted_iota(jnp.int32, (r, r), 0)
    col = lax.broadcasted_iota(jnp.int32, (r, r), 1)
    same_chunk = (row // c) == (col // c)
    masks = (same_chunk & (col <= row), same_chunk & (col >= row))
    tris = tuple(jnp.where(m, 1.0, 0.0).astype(BF16) for m in masks)
    last = (c - 1, 0)
    mid = (c // 2 - 1, c // 2)

    def per_chunk_rows(x, off):
        return jnp.concatenate([jnp.broadcast_to(x[j * c + off:j * c + off + 1, :], (c, x.shape[1]))
                                for j in range(cpb)], axis=0)

    def lane_block(d, j):
        return slice((d * cpb + j) * dk, (d * cpb + j + 1) * dk)

    ke_ref[...] = jnp.zeros(ke_ref.shape, BF16)
    qe_ref[...] = jnp.zeros(qe_ref.shape, BF16)

    blocks = ([((cff_ref, cfb_ref), cv_ref, None, i * r, i * cpb) for i in range(n_ctx // cpb)]
              + [((ff_ref, fb_ref), hv_ref, hq_ref, i * r, n_ctx + i * cpb) for i in range(n_lat // cpb)])

    def decays(n):
        f_refs, _, _, r0, _ = blocks[n]
        for d in range(2):
            lf = f_refs[d][r0:r0 + r, :]
            g1, g2, g3 = _split3(lf)
            bb = _dot(tris[d], jnp.concatenate([g1, g2, g3], axis=1))
            bk_ref[n % 2, d] = bb[:, :dk] + bb[:, dk:2 * dk] + bb[:, 2 * dk:]
            bk_ref[n % 2, 2 + d] = 1.0 - jnp.exp2(lf)

    def scores(n):
        _, _, q_ref, r0, ch0 = blocks[n]
        q = None if q_ref is None else q_ref[r0:r0 + r, :]
        amat = None
        for d in range(2):
            b, k = bk_ref[n % 2, d], bk_ref[n % 2, 2 + d]
            b_last = per_chunk_rows(b, last[d])
            ke = (k * jnp.exp2(b_last - b)).astype(BF16)
            for j in range(cpb):
                ke_ref[n % 2, j * c:(j + 1) * c, lane_block(d, j)] = ke[j * c:(j + 1) * c, :]
                dec_ref[d, ch0 + j] = jnp.exp2(b[j * c + last[d]:j * c + last[d] + 1, :])
            if q is None:
                continue
            b_mid = per_chunk_rows(b, mid[d])
            qd = (q * jnp.exp2(b - b_mid)).astype(BF16)
            kd = (k * jnp.exp2(b_mid - b)).astype(BF16)
            a = jnp.where(masks[d], lax.dot_general(qd, kd, NT_DIMS, preferred_element_type=F32), 0.0)
            amat = a if amat is None else amat + a
            qe = (q * jnp.exp2(b)).astype(BF16)
            for j in range(cpb):
                qe_ref[r0 + j * c:r0 + (j + 1) * c, lane_block(d, j)] = qe[j * c:(j + 1) * c, :]
        if q is not None:
            a_ref[n % 2] = amat.astype(BF16)

    def products(n):
        _, v_ref, q_ref, r0, ch0 = blocks[n]
        v = v_ref[r0:r0 + r, :]
        ds = lax.dot_general(v, ke_ref[n % 2], TN_DIMS, preferred_element_type=F32)
        for d in range(2):
            for j in range(cpb):
                lo = (d * cpb + j) * dk
                ds_ref[d, ch0 + j] = ds[:, lo:lo + dk]
        if q_ref is not None:
            acc_ref[r0:r0 + r, :] = _dot(a_ref[n % 2], v)

    decays(0)
    for n in range(len(blocks)):
        if n + 1 < len(blocks):
            decays(n + 1)
        scores(n)
        if n:
            products(n - 1)
    products(len(blocks) - 1)

    for d in range(2):
        def ctx_step(i, st):
            ch = (n_ctx - 1 - i) if d else i
            return st * dec_ref[d, ch] + ds_ref[d, ch]

        def lat_steps(i, st):
            blk = (n_lat // cpb - 1 - i) if d else i
            for jj in range(cpb):
                j = (cpb - 1 - jj) if d else jj
                st_ref[blk, :, (d * cpb + j) * dk:(d * cpb + j + 1) * dk] = st.astype(BF16)
                ch = n_ctx + blk * cpb + j
                st = st * dec_ref[d, ch] + ds_ref[d, ch]
            return st

        st = lax.fori_loop(0, n_ctx, ctx_step, jnp.zeros((HG_DV, dk), F32), unroll=True)
        lax.fori_loop(0, n_lat // cpb, lat_steps, st)

    def pass3(i, carry):
        rows = pl.ds(pl.multiple_of(i * r, r), r)
        inter = lax.dot_general(qe_ref[rows, :], st_ref[i], NT_DIMS, preferred_element_type=F32)
        tot = acc_ref[rows, :] + inter
        o_ref[rows, :] = (_rms(tot, gain_ref[...]) * _silu(og_ref[rows, :])).astype(BF16)
        return carry

    lax.fori_loop(0, n_lat // cpb, pass3, 0, unroll=4)


def _hgrn(hg_gain, hq, hv, ff, og, cv, cff):
    bsz, t, _ = hq.shape
    tc = cv.shape[1]

    def col(tt, off=0):
        return pl.BlockSpec((None, tt, HG_DK), lambda b, h: (b, 0, h + off))

    n_lat, n_ctx = t // CHUNK, tc // CHUNK
    cpb = next(n for n in (4, 2, 1) if n_lat % n == 0 and n_ctx % n == 0)
    return pl.pallas_call(
        functools.partial(_hgrn_bidir_kernel, n_lat=n_lat, n_ctx=n_ctx, cpb=cpb),
        grid=(bsz, HG_HEADS),
        in_specs=[pl.BlockSpec((1, HG_DV), lambda b, h: (0, 0)),
                  col(t), col(t), col(t), col(t, HG_HEADS), col(t),
                  col(tc), col(tc), col(tc, HG_HEADS)],
        out_specs=col(t),
        out_shape=jax.ShapeDtypeStruct((bsz, t, HG_WIDTH), BF16),
        scratch_shapes=[pltpu.VMEM((t, HG_DV), F32),
                        pltpu.VMEM((t, 2 * cpb * HG_DK), BF16),
                        pltpu.VMEM((2, n_ctx + n_lat, HG_DV, HG_DK), F32),
                        pltpu.VMEM((2, n_ctx + n_lat, 1, HG_DK), F32),
                        pltpu.VMEM((n_lat // cpb, HG_DV, 2 * cpb * HG_DK), BF16),
                        pltpu.VMEM((2, 4, cpb * CHUNK, HG_DK), F32),
                        pltpu.VMEM((2, cpb * CHUNK, 2 * cpb * HG_DK), BF16),
                        pltpu.VMEM((2, cpb * CHUNK, cpb * CHUNK), BF16)],
        compiler_params=_params(2),
        name="hgrn",
    )(hg_gain[None, :], hq, hv, ff, ff, og, cv, cff, cff)


def _merge_ffn_kernel(x_ref, m_ref, oa_ref, oh_ref, mg_ref, gpost1_ref, wa_ref, wh_ref, wo_ref,
                      gpre2_ref, gpost2_ref, wg_ref, wu_ref, wd_ref, o_ref, *, d):
    gate = m_ref[:, 5 * d:6 * d]
    for rows in _sub_tiles(x_ref.shape[0]):
        y = (_sigmoid(mg_ref[rows, :d]) * _dot(oa_ref[rows, :], wa_ref[...])
             + _sigmoid(mg_ref[rows, d:]) * _dot(oh_ref[rows, :], wh_ref[...]))
        z = _dot(y.astype(BF16), wo_ref[...])
        x2 = x_ref[rows, :] + gate * _rms(z, gpost1_ref[...])
        o_ref[rows, :] = _ffn_half_step(x2, m_ref, gpre2_ref, gpost2_ref, wg_ref, wu_ref, wd_ref, 6, d)


def _merge_ffn(x, mods, o_att, o_hg, mg, g_post1, wa, wh, wo, g_pre2, g_post2, wg, wu, wd, tm):
    bsz, t, d = x.shape
    nm = mods.shape[-1]

    def tile(width):
        return pl.BlockSpec((None, tm, width), lambda b, i: (b, i, 0))

    return pl.pallas_call(
        functools.partial(_merge_ffn_kernel, d=d),
        grid=(bsz, t // tm),
        in_specs=[tile(d), pl.BlockSpec((None, 1, nm), lambda b, i: (b, 0, 0)),
                  tile(ATT_WIDTH), tile(HG_WIDTH), tile(2 * d), _const_spec((1, d)),
                  _const_spec(wa.shape), _const_spec(wh.shape), _const_spec(wo.shape),
                  _const_spec((1, d)), _const_spec((1, d)),
                  _const_spec(wg.shape), _const_spec(wu.shape), _const_spec(wd.shape)],
        out_specs=tile(d),
        out_shape=jax.ShapeDtypeStruct((bsz, t, d), F32),
        compiler_params=_params(2),
        name="merge_ffn",
    )(x, mods, o_att, o_hg, mg, g_post1[None, :], wa, wh, wo, g_pre2[None, :], g_post2[None, :], wg, wu, wd)


def _rope_tables(t):
    pos = jnp.arange(t, dtype=jnp.int32)
    row = (pos // GRID_W).astype(F32)
    colp = (pos % GRID_W).astype(F32)
    inv_freq = ROPE_THETA ** (-jnp.arange(ROPE_PAIRS, dtype=F32) / ROPE_PAIRS)
    ang_r = row[:, None] * inv_freq
    ang_c = colp[:, None] * inv_freq
    ang = jnp.concatenate([ang_r, ang_r, ang_c, ang_c], axis=-1)
    cos, sin = jnp.cos(ang), jnp.sin(ang)
    first = (jnp.arange(HEAD_DIM) % (2 * ROPE_PAIRS)) < ROPE_PAIRS
    sin_lo = jnp.where(first, -sin, 0.0)
    sin_hi = jnp.where(first, 0.0, sin)
    two = lambda a: jnp.concatenate([a, a], axis=-1)
    return two(cos), two(sin_lo), two(sin_hi)


def kernel(x, c, ctx, c_ctx, w_mod, b_mod, norm_pre, norm_post, ffn_w_gate, ffn_w_up, ffn_w_down,
           w_in, q_norm, k_norm, hg_lower_bound, hg_norm, w_att_out, w_hg_out, w_o):
    assert w_in.shape[0] == 1, "single-layer block"
    bsz, t, d = x.shape
    tc = ctx.shape[1]
    assert t % GRID_W == 0 and t % CHUNK == 0 and tc % CHUNK == 0
    tm = min(256, t)
    tmc = min(256, tc)
    tmf = 512 if t % 512 == 0 else tm

    rows = -(-(bsz + 1) // 8) * 8
    cvec = jnp.concatenate([c, c_ctx[None, :], jnp.zeros((rows - bsz - 1, d), c.dtype)], axis=0)
    mods = _modulation(cvec, w_mod[0], b_mod[0])[:, None, :]
    lat_row = lambda b: b
    ctx_row = lambda b: bsz

    wg, wu, wd = ffn_w_gate[0].astype(BF16), ffn_w_up[0].astype(BF16), ffn_w_down[0].astype(BF16)
    w_in_b = w_in[0].astype(BF16)

    tml = 1024 if t % 1024 == 0 else tmf
    x1 = _ffn(x, mods, lat_row, 0, norm_pre[0, 0], norm_post[0, 0], wg[0], wu[0], wd[0], tml)
    h1 = _ffn(ctx, mods, ctx_row, 0, norm_pre[0, 0], norm_post[0, 0], wg[0], wu[0], wd[0], tmc)

    cos, slo, shi = _rope_tables(t)
    q_gain2 = jnp.concatenate([q_norm[0], q_norm[0]])[None, :]
    k_gain2 = jnp.concatenate([k_norm[0], k_norm[0]])[None, :]
    slots = hg_lower_bound.shape[1]
    lb_raw = jnp.transpose(hg_lower_bound.astype(F32), (1, 0, 2)).reshape(slots, 2 * HG_WIDTH)
    q, k, v, hq, hv, ff, og, mg = _inproj_latent(x1, mods, norm_pre[0, 1], w_in_b, q_gain2, k_gain2, lb_raw,
                                                  cos, slo, shi, tmf)
    kv0 = ATT_WIDTH
    h0 = ATT_WIDTH + 2 * KV_WIDTH + HG_WIDTH
    ck, cv, chv, cff = _inproj_ctx(h1, mods, bsz, norm_pre[0, 1], w_in_b[:, kv0:kv0 + 2 * KV_WIDTH],
                                   w_in_b[:, h0:h0 + 3 * HG_WIDTH], k_gain2, lb_raw, tmc)

    k_all = jnp.concatenate([k, ck], axis=1).reshape(bsz, t + tc, N_KV_HEADS, HEAD_DIM)
    v_all = jnp.concatenate([v, cv], axis=1).reshape(bsz, t + tc, N_KV_HEADS, HEAD_DIM)
    kh = jnp.transpose(k_all, (0, 2, 1, 3))
    vt = jnp.transpose(v_all, (0, 2, 3, 1))
    pad_rows = jnp.zeros((bsz, N_KV_HEADS, VT_PAD, t + tc), BF16).at[:, :, 0, :].set(1.0)
    o_att = _attention(q, kh, jnp.concatenate([vt, pad_rows], axis=2), tm, tmf)

    o_hg = _hgrn(hg_norm[0], hq, hv, ff, og, chv, cff)

    return _merge_ffn(x1, mods, o_att, o_hg, mg, norm_post[0, 1], w_att_out[0].astype(BF16),
                      w_hg_out[0].astype(BF16), w_o[0].astype(BF16),
                      norm_pre[0, 2], norm_post[0, 2], wg[1], wu[1], wd[1], tmf)
```

```python
import functools

import jax
import jax.numpy as jnp
from jax import lax
from jax.experimental import pallas as pl
from jax.experimental.pallas import tpu as pltpu

EPS = 1e-6
N_MOD = 9
GRID_W = 64
ROPE_THETA = 10000.0
HEAD_DIM = 64
N_Q_HEADS = 8
N_KV_HEADS = 2
GROUP = N_Q_HEADS // N_KV_HEADS
ATT_WIDTH = N_Q_HEADS * HEAD_DIM
KV_WIDTH = N_KV_HEADS * HEAD_DIM
ROPE_PAIRS = HEAD_DIM // 4
ATT_SCALE = HEAD_DIM ** -0.5
LOG2E = 1.4426950408889634
HG_HEADS = 4
HG_DK = 128
HG_DV = 128
HG_WIDTH = HG_HEADS * HG_DK
HG_SCALE = HG_DK ** -0.5
CHUNK = 64
LANES = 128
SUB_TILE = 256
VMEM_LIMIT = 56 * 1024 * 1024

BF16 = jnp.bfloat16
F32 = jnp.float32

NT_DIMS = (((1,), (1,)), ((), ()))
TN_DIMS = (((0,), (0,)), ((), ()))


def _dot(a, b):
    return jnp.dot(a, b, preferred_element_type=F32)


def _rms(x, gain):
    return x * lax.rsqrt(jnp.mean(x * x, axis=-1, keepdims=True) + EPS) * gain


def _sigmoid(x):
    return 1.0 / (1.0 + jnp.exp(-x))


def _silu(x):
    return x * _sigmoid(x)


def _params(n_grid):
    return pltpu.CompilerParams(dimension_semantics=("parallel",) * n_grid, vmem_limit_bytes=VMEM_LIMIT)


def _const_spec(shape):
    nd = len(shape)
    return pl.BlockSpec(shape, lambda *_: (0,) * nd, pipeline_mode=pl.Buffered(1))


def _mod_kernel(c_ref, w_ref, b_ref, o_ref):
    a = _silu(c_ref[...]).astype(BF16)
    o_ref[...] = _dot(a, w_ref[...].astype(BF16)) + b_ref[...]


def _modulation(cvec, w_mod, b_mod, tn=1024):
    rows, d = cvec.shape
    n = w_mod.shape[1]
    return pl.pallas_call(
        _mod_kernel,
        grid=(n // tn,),
        in_specs=[pl.BlockSpec((rows, d), lambda j: (0, 0)),
                  pl.BlockSpec((d, tn), lambda j: (0, j)),
                  pl.BlockSpec((1, tn), lambda j: (0, j))],
        out_specs=pl.BlockSpec((rows, tn), lambda j: (0, j)),
        out_shape=jax.ShapeDtypeStruct((rows, n), F32),
        compiler_params=_params(1),
        name="mod",
    )(cvec, w_mod, b_mod[None, :])


def _ffn_half_step(x, m_ref, gpre_ref, gpost_ref, wg_ref, wu_ref, wd_ref, mod0, d):
    shift = m_ref[:, (mod0 + 0) * d:(mod0 + 1) * d]
    scale = m_ref[:, (mod0 + 1) * d:(mod0 + 2) * d]
    gate = m_ref[:, (mod0 + 2) * d:(mod0 + 3) * d]
    u = (_rms(x, gpre_ref[...]) * (1.0 + scale) + shift).astype(BF16)
    h = (_silu(_dot(u, wg_ref[...])) * _dot(u, wu_ref[...])).astype(BF16)
    y = _dot(h, wd_ref[...])
    return x + 0.5 * (gate * _rms(y, gpost_ref[...]))


def _sub_tiles(rows):
    sub = SUB_TILE if rows % SUB_TILE == 0 else rows
    return [slice(r, r + sub) for r in range(0, rows, sub)]


def _ffn_kernel(x_ref, m_ref, gpre_ref, gpost_ref, wg_ref, wu_ref, wd_ref, o_ref, *, mod0, d):
    for rows in _sub_tiles(x_ref.shape[0]):
        o_ref[rows, :] = _ffn_half_step(x_ref[rows, :], m_ref, gpre_ref, gpost_ref, wg_ref, wu_ref, wd_ref,
                                        mod0, d)


def _ffn(x, mods, mod_row, mod0, g_pre, g_post, wg, wu, wd, tm):
    bsz, t, d = x.shape
    f = wg.shape[1]
    nm = mods.shape[-1]
    return pl.pallas_call(
        functools.partial(_ffn_kernel, mod0=mod0, d=d),
        grid=(bsz, t // tm),
        in_specs=[pl.BlockSpec((None, tm, d), lambda b, i: (b, i, 0)),
                  pl.BlockSpec((None, 1, nm), lambda b, i: (mod_row(b), 0, 0)),
                  _const_spec((1, d)), _const_spec((1, d)),
                  _const_spec((d, f)), _const_spec((d, f)), _const_spec((f, d))],
        out_specs=pl.BlockSpec((None, tm, d), lambda b, i: (b, i, 0)),
        out_shape=jax.ShapeDtypeStruct((bsz, t, d), F32),
        compiler_params=_params(2),
        name="ffn",
    )(x, mods, g_pre[None, :], g_post[None, :], wg, wu, wd)


def _log2_forget(raw, lbraw_ref):
    slots = [lbraw_ref[s:s + 1, :] for s in range(lbraw_ref.shape[0])]
    top = functools.reduce(jnp.maximum, slots)
    e = [jnp.exp(s - top) for s in slots]
    lb = e[0] / functools.reduce(jnp.add, e)
    return jnp.log2(lb + (1.0 - lb) * _sigmoid(raw))


def _head_rms64(z, gain):
    lane = lax.broadcasted_iota(jnp.int32, (1, LANES), 1)
    first = lane < HEAD_DIM
    sq = z * z
    lo = jnp.sum(jnp.where(first, sq, 0.0), axis=-1, keepdims=True)
    hi = jnp.sum(jnp.where(first, 0.0, sq), axis=-1, keepdims=True)
    ms = jnp.where(first, lo, hi) * (1.0 / HEAD_DIM)
    return z * lax.rsqrt(ms + EPS) * gain


def _rope128(z, cos, sin_lo, sin_hi):
    q = ROPE_PAIRS
    return z * cos + pltpu.roll(z, LANES - q, 1) * sin_lo + pltpu.roll(z, q, 1) * sin_hi


def _inproj_latent_kernel(x_ref, m_ref, gpre_ref, w_ref, qg_ref, kg_ref, lbraw_ref, cos_ref, slo_ref, shi_ref,
                          q_ref, k_ref, v_ref, hq_ref, hv_ref, ff_ref, og_ref, mg_ref, *, d):
    shift = m_ref[:, 3 * d:4 * d]
    scale = m_ref[:, 4 * d:5 * d]
    for rows in _sub_tiles(x_ref.shape[0]):
        u = (_rms(x_ref[rows, :], gpre_ref[...]) * (1.0 + scale) + shift).astype(BF16)
        cos, slo, shi = cos_ref[rows, :], slo_ref[rows, :], shi_ref[rows, :]

        def proj(lo, hi):
            return _dot(u, w_ref[:, lo:hi])

        c0 = 0
        pq = proj(c0, c0 + ATT_WIDTH)
        qs = []
        for j in range(ATT_WIDTH // LANES):
            z = _head_rms64(pq[:, j * LANES:(j + 1) * LANES], qg_ref[...])
            qs.append(_rope128(z, cos, slo, shi) * (ATT_SCALE * LOG2E))
        q_ref[rows, :] = jnp.concatenate(qs, axis=-1).astype(BF16)
        c0 += ATT_WIDTH
        pkv = proj(c0, c0 + 2 * KV_WIDTH)
        k_ref[rows, :] = _rope128(_head_rms64(pkv[:, :KV_WIDTH], kg_ref[...]), cos, slo, shi).astype(BF16)
        v_ref[rows, :] = pkv[:, KV_WIDTH:].astype(BF16)
        c0 += 2 * KV_WIDTH
        hq_ref[rows, :] = _silu(proj(c0, c0 + HG_WIDTH)) * HG_SCALE
        c0 += HG_WIDTH
        hv_ref[rows, :] = proj(c0, c0 + HG_WIDTH).astype(BF16)
        c0 += HG_WIDTH
        ff_ref[rows, :] = _log2_forget(proj(c0, c0 + 2 * HG_WIDTH), lbraw_ref)
        c0 += 2 * HG_WIDTH
        og_ref[rows, :] = proj(c0, c0 + HG_WIDTH)
        c0 += HG_WIDTH
        mg_ref[rows, :] = proj(c0, c0 + 2 * d)


def _inproj_latent(x, mods, g_pre, w_in, q_gain2, k_gain2, lb_raw, cos, slo, shi, tm):
    bsz, t, d = x.shape
    nm = mods.shape[-1]
    n_in = w_in.shape[1]

    def tile(width, dtype):
        return (pl.BlockSpec((None, tm, width), lambda b, i: (b, i, 0)),
                jax.ShapeDtypeStruct((bsz, t, width), dtype))

    outs = [tile(ATT_WIDTH, BF16), tile(KV_WIDTH, BF16), tile(KV_WIDTH, BF16), tile(HG_WIDTH, F32),
            tile(HG_WIDTH, BF16), tile(2 * HG_WIDTH, F32), tile(HG_WIDTH, F32), tile(2 * d, F32)]
    rope_spec = pl.BlockSpec((tm, LANES), lambda b, i: (i, 0))
    return pl.pallas_call(
        functools.partial(_inproj_latent_kernel, d=d),
        grid=(bsz, t // tm),
        in_specs=[pl.BlockSpec((None, tm, d), lambda b, i: (b, i, 0)),
                  pl.BlockSpec((None, 1, nm), lambda b, i: (b, 0, 0)),
                  _const_spec((1, d)), _const_spec((d, n_in)),
                  _const_spec((1, LANES)), _const_spec((1, LANES)), _const_spec(lb_raw.shape),
                  rope_spec, rope_spec, rope_spec],
        out_specs=[o[0] for o in outs],
        out_shape=[o[1] for o in outs],
        compiler_params=_params(2),
        name="inproj_latent",
    )(x, mods, g_pre[None, :], w_in, q_gain2, k_gain2, lb_raw, cos, slo, shi)


def _inproj_ctx_kernel(x_ref, m_ref, gpre_ref, w_ref, kg_ref, lbraw_ref, k_ref, v_ref, hv_ref, ff_ref, *, d):
    x = x_ref[...]
    shift = m_ref[:, 3 * d:4 * d]
    scale = m_ref[:, 4 * d:5 * d]
    u = (_rms(x, gpre_ref[...]) * (1.0 + scale) + shift).astype(BF16)
    kv0 = ATT_WIDTH
    h0 = ATT_WIDTH + 2 * KV_WIDTH + HG_WIDTH
    pkv = _dot(u, w_ref[:, kv0:kv0 + 2 * KV_WIDTH])
    k_ref[...] = _head_rms64(pkv[:, :KV_WIDTH], kg_ref[...]).astype(BF16)
    v_ref[...] = pkv[:, KV_WIDTH:].astype(BF16)
    hv_ref[...] = _dot(u, w_ref[:, h0:h0 + HG_WIDTH]).astype(BF16)
    ff_ref[...] = _log2_forget(_dot(u, w_ref[:, h0 + HG_WIDTH:h0 + 3 * HG_WIDTH]), lbraw_ref)


def _inproj_ctx(x, mods, ctx_row, g_pre, w_in, k_gain2, lb_raw, tm):
    bsz, t, d = x.shape
    nm = mods.shape[-1]

    def tile(width, dtype):
        return (pl.BlockSpec((None, tm, width), lambda b, i: (b, i, 0)),
                jax.ShapeDtypeStruct((bsz, t, width), dtype))

    outs = [tile(KV_WIDTH, BF16), tile(KV_WIDTH, BF16), tile(HG_WIDTH, BF16), tile(2 * HG_WIDTH, F32)]
    return pl.pallas_call(
        functools.partial(_inproj_ctx_kernel, d=d),
        grid=(bsz, t // tm),
        in_specs=[pl.BlockSpec((None, tm, d), lambda b, i: (b, i, 0)),
                  pl.BlockSpec((None, 1, nm), lambda b, i: (ctx_row, 0, 0)),
                  _const_spec((1, d)), _const_spec(w_in.shape),
                  _const_spec((1, LANES)), _const_spec(lb_raw.shape)],
        out_specs=[o[0] for o in outs],
        out_shape=[o[1] for o in outs],
        compiler_params=_params(2),
        name="inproj_ctx",
    )(x, mods, g_pre[None, :], w_in, k_gain2, lb_raw)


def _attn_kernel(q_ref, kl_ref, kc_ref, vl_ref, vc_ref, o_ref, st_ref, pt_ref, vt_ref):
    tq = st_ref.shape[2]
    t_lat = kl_ref.shape[0]
    n_items = (q_ref.shape[0] // tq) * N_Q_HEADS

    @pl.when(pl.program_id(1) == 0)
    def _():
        vt_ref[:, :t_lat] = vl_ref[...].astype(F32).T.astype(BF16)
        vt_ref[:, t_lat:] = vc_ref[...].astype(F32).T.astype(BF16)

    def scores(i):
        r, h = divmod(i, N_Q_HEADS)
        q = q_ref[r * tq:(r + 1) * tq, h * HEAD_DIM:(h + 1) * HEAD_DIM]
        halves = [q, jnp.zeros_like(q)] if h // GROUP == 0 else [jnp.zeros_like(q), q]
        qp = jnp.concatenate(halves, axis=1)
        s_lat = lax.dot_general(kl_ref[...], qp, NT_DIMS, preferred_element_type=F32)
        s_ctx = lax.dot_general(kc_ref[...], qp, NT_DIMS, preferred_element_type=F32)
        buf = i % st_ref.shape[0]
        st_ref[buf, :t_lat, :] = s_lat
        st_ref[buf, t_lat:, :] = s_ctx
        return jnp.maximum(jnp.max(s_lat, axis=0, keepdims=True), jnp.max(s_ctx, axis=0, keepdims=True))

    pairs, sums = [], []

    def values(i):
        kv = (i % N_Q_HEADS) // GROUP
        ot = _dot(vt_ref[kv * HEAD_DIM:(kv + 1) * HEAD_DIM, :], pt_ref[i % 2])
        pairs.append(ot / sums[i])

    ahead = st_ref.shape[0] - 1
    ms = [scores(i) for i in range(ahead)]
    for i in range(n_items):
        if i + ahead < n_items:
            ms.append(scores(i + ahead))
        p = jnp.exp2(st_ref[i % (ahead + 1)] - ms[i])
        sums.append(jnp.sum(p, axis=0, keepdims=True))
        pt_ref[i % 2] = p.astype(BF16)
        if i:
            values(i - 1)
    values(n_items - 1)
    for r in range(n_items // N_Q_HEADS):
        heads = pairs[r * N_Q_HEADS:(r + 1) * N_Q_HEADS]
        outs = [jnp.concatenate(heads[j:j + 2], axis=0).T for j in range(0, N_Q_HEADS, 2)]
        o_ref[r * tq:(r + 1) * tq, :] = jnp.concatenate(outs, axis=-1).astype(BF16)


def _attention(q, k_lat, k_ctx, v_lat, v_ctx, tq, tstep):
    bsz, t, _ = q.shape
    tc = k_ctx.shape[1]
    s = t + tc

    def whole(tt):
        return pl.BlockSpec((None, tt, KV_WIDTH), lambda b, i: (b, 0, 0))

    return pl.pallas_call(
        _attn_kernel,
        grid=(bsz, t // tstep),
        in_specs=[pl.BlockSpec((None, tstep, ATT_WIDTH), lambda b, i: (b, i, 0)),
                  whole(t), whole(tc), whole(t), whole(tc)],
        out_specs=pl.BlockSpec((None, tstep, ATT_WIDTH), lambda b, i: (b, i, 0)),
        out_shape=jax.ShapeDtypeStruct((bsz, t, ATT_WIDTH), BF16),
        scratch_shapes=[pltpu.VMEM((3, s, tq), F32),
                        pltpu.VMEM((2, s, tq), BF16),
                        pltpu.VMEM((KV_WIDTH, s), BF16)],
        compiler_params=pltpu.CompilerParams(dimension_semantics=("parallel", "arbitrary"),
                                             vmem_limit_bytes=VMEM_LIMIT),
        name="attn",
    )(q, k_lat, k_ctx, v_lat, v_ctx)


def _split3(g):
    g1 = g.astype(BF16)
    r1 = g - g1.astype(F32)
    g2 = r1.astype(BF16)
    g3 = (r1 - g2.astype(F32)).astype(BF16)
    return g1, g2, g3


def _hgrn_bidir_kernel(gain_ref, hq_ref, hv_ref, ff_ref, fb_ref, og_ref, cv_ref, cff_ref, cfb_ref,
                       o_ref, acc_ref, qe_ref, ds_ref, dec_ref, st_ref, bk_ref, ke_ref, a_ref,
                       *, n_lat, n_ctx, cpb):
    c = CHUNK
    r = cpb * c
    dk = HG_DK
    row = lax.broadcasted_iota(jnp.int32, (r, r), 0)
    col = lax.broadcasted_iota(jnp.int32, (r, r), 1)
    same_chunk = (row // c) == (col // c)
    masks = (same_chunk & (col <= row), same_chunk & (col >= row))
    tris = tuple(jnp.where(m, 1.0, 0.0).astype(BF16) for m in masks)
    last = (c - 1, 0)
    mid = (c // 2 - 1, c // 2)

    def per_chunk_rows(x, off):
        return jnp.concatenate([jnp.broadcast_to(x[j * c + off:j * c + off + 1, :], (c, x.shape[1]))
                                for j in range(cpb)], axis=0)

    def lane_block(d, j):
        return slice((d * cpb + j) * dk, (d * cpb + j + 1) * dk)

    ke_ref[...] = jnp.zeros(ke_ref.shape, BF16)
    qe_ref[...] = jnp.zeros(qe_ref.shape, BF16)

    blocks = ([((cff_ref, cfb_ref), cv_ref, None, i * r, i * cpb) for i in range(n_ctx // cpb)]
              + [((ff_ref, fb_ref), hv_ref, hq_ref, i * r, n_ctx + i * cpb) for i in range(n_lat // cpb)])

    def decays(n):
        f_refs, _, _, r0, _ = blocks[n]
        for d in range(2):
            lf = f_refs[d][r0:r0 + r, :]
            g1, g2, g3 = _split3(lf)
            bb = _dot(tris[d], jnp.concatenate([g1, g2, g3], axis=1))
            bk_ref[n % 2, d] = bb[:, :dk] + bb[:, dk:2 * dk] + bb[:, 2 * dk:]
            bk_ref[n % 2, 2 + d] = 1.0 - jnp.exp2(lf)

    def scores(n):
        _, _, q_ref, r0, ch0 = blocks[n]
        q = None if q_ref is None else q_ref[r0:r0 + r, :]
        amat = None
        for d in range(2):
            b, k = bk_ref[n % 2, d], bk_ref[n % 2, 2 + d]
            b_last = per_chunk_rows(b, last[d])
            ke = (k * jnp.exp2(b_last - b)).astype(BF16)
            for j in range(cpb):
                ke_ref[n % 2, j * c:(j + 1) * c, lane_block(d, j)] = ke[j * c:(j + 1) * c, :]
                dec_ref[d, ch0 + j] = jnp.exp2(b[j * c + last[d]:j * c + last[d] + 1, :])
            if q is None:
                continue
            b_mid = per_chunk_rows(b, mid[d])
            qd = (q * jnp.exp2(b - b_mid)).astype(BF16)
            kd = (k * jnp.exp2(b_mid - b)).astype(BF16)
            a = jnp.where(masks[d], lax.dot_general(qd, kd, NT_DIMS, preferred_element_type=F32), 0.0)
            amat = a if amat is None else amat + a
            qe = (q * jnp.exp2(b)).astype(BF16)
            for j in range(cpb):
                qe_ref[r0 + j * c:r0 + (j + 1) * c, lane_block(d, j)] = qe[j * c:(j + 1) * c, :]
        if q is not None:
            a_ref[n % 2] = amat.astype(BF16)

    def products(n):
        _, v_ref, q_ref, r0, ch0 = blocks[n]
        v = v_ref[r0:r0 + r, :]
        ds = lax.dot_general(v, ke_ref[n % 2], TN_DIMS, preferred_element_type=F32)
        for d in range(2):
            for j in range(cpb):
                lo = (d * cpb + j) * dk
                ds_ref[d, ch0 + j] = ds[:, lo:lo + dk]
        if q_ref is not None:
            acc_ref[r0:r0 + r, :] = _dot(a_ref[n % 2], v)

    decays(0)
    for n in range(len(blocks)):
        if n + 1 < len(blocks):
            decays(n + 1)
        scores(n)
        if n:
            products(n - 1)
    products(len(blocks) - 1)

    for d in range(2):
        def ctx_step(i, st):
            ch = (n_ctx - 1 - i) if d else i
            return st * dec_ref[d, ch] + ds_ref[d, ch]

        def lat_steps(i, st):
            blk = (n_lat // cpb - 1 - i) if d else i
            for jj in range(cpb):
                j = (cpb - 1 - jj) if d else jj
                st_ref[blk, :, (d * cpb + j) * dk:(d * cpb + j + 1) * dk] = st.astype(BF16)
                ch = n_ctx + blk * cpb + j
                st = st * dec_ref[d, ch] + ds_ref[d, ch]
            return st

        st = lax.fori_loop(0, n_ctx, ctx_step, jnp.zeros((HG_DV, dk), F32), unroll=True)
        lax.fori_loop(0, n_lat // cpb, lat_steps, st)

    def pass3(i, carry):
        rows = pl.ds(pl.multiple_of(i * r, r), r)
        inter = lax.dot_general(qe_ref[rows, :], st_ref[i], NT_DIMS, preferred_element_type=F32)
        tot = acc_ref[rows, :] + inter
        o_ref[rows, :] = (_rms(tot, gain_ref[...]) * _silu(og_ref[rows, :])).astype(BF16)
        return carry

    lax.fori_loop(0, n_lat // cpb, pass3, 0, unroll=4)


def _hgrn(hg_gain, hq, hv, ff, og, cv, cff):
    bsz, t, _ = hq.shape
    tc = cv.shape[1]

    def col(tt, off=0):
        return pl.BlockSpec((None, tt, HG_DK), lambda b, h: (b, 0, h + off))

    n_lat, n_ctx = t // CHUNK, tc // CHUNK
    cpb = next(n for n in (4, 2, 1) if n_lat % n == 0 and n_ctx % n == 0)
    return pl.pallas_call(
        functools.partial(_hgrn_bidir_kernel, n_lat=n_lat, n_ctx=n_ctx, cpb=cpb),
        grid=(bsz, HG_HEADS),
        in_specs=[pl.BlockSpec((1, HG_DV), lambda b, h: (0, 0)),
                  col(t), col(t), col(t), col(t, HG_HEADS), col(t),
                  col(tc), col(tc), col(tc, HG_HEADS)],
        out_specs=col(t),
        out_shape=jax.ShapeDtypeStruct((bsz, t, HG_WIDTH), BF16),
        scratch_shapes=[pltpu.VMEM((t, HG_DV), F32),
                        pltpu.VMEM((t, 2 * cpb * HG_DK), BF16),
                        pltpu.VMEM((2, n_ctx + n_lat, HG_DV, HG_DK), F32),
                        pltpu.VMEM((2, n_ctx + n_lat, 1, HG_DK), F32),
                        pltpu.VMEM((n_lat // cpb, HG_DV, 2 * cpb * HG_DK), BF16),
                        pltpu.VMEM((2, 4, cpb * CHUNK, HG_DK), F32),
                        pltpu.VMEM((2, cpb * CHUNK, 2 * cpb * HG_DK), BF16),
                        pltpu.VMEM((2, cpb * CHUNK, cpb * CHUNK), BF16)],
        compiler_params=_params(2),
        name="hgrn",
    )(hg_gain[None, :], hq, hv, ff, ff, og, cv, cff, cff)


def _merge_ffn_kernel(x_ref, m_ref, oa_ref, oh_ref, mg_ref, gpost1_ref, wa_ref, wh_ref, wo_ref,
                      gpre2_ref, gpost2_ref, wg_ref, wu_ref, wd_ref, o_ref, *, d):
    gate = m_ref[:, 5 * d:6 * d]
    for rows in _sub_tiles(x_ref.shape[0]):
        y = (_sigmoid(mg_ref[rows, :d]) * _dot(oa_ref[rows, :], wa_ref[...])
             + _sigmoid(mg_ref[rows, d:]) * _dot(oh_ref[rows, :], wh_ref[...]))
        z = _dot(y.astype(BF16), wo_ref[...])
        x2 = x_ref[rows, :] + gate * _rms(z, gpost1_ref[...])
        o_ref[rows, :] = _ffn_half_step(x2, m_ref, gpre2_ref, gpost2_ref, wg_ref, wu_ref, wd_ref, 6, d)


def _merge_ffn(x, mods, o_att, o_hg, mg, g_post1, wa, wh, wo, g_pre2, g_post2, wg, wu, wd, tm):
    bsz, t, d = x.shape
    nm = mods.shape[-1]

    def tile(width):
        return pl.BlockSpec((None, tm, width), lambda b, i: (b, i, 0))

    return pl.pallas_call(
        functools.partial(_merge_ffn_kernel, d=d),
        grid=(bsz, t // tm),
        in_specs=[tile(d), pl.BlockSpec((None, 1, nm), lambda b, i: (b, 0, 0)),
                  tile(ATT_WIDTH), tile(HG_WIDTH), tile(2 * d), _const_spec((1, d)),
                  _const_spec(wa.shape), _const_spec(wh.shape), _const_spec(wo.shape),
                  _const_spec((1, d)), _const_spec((1, d)),
                  _const_spec(wg.shape), _const_spec(wu.shape), _const_spec(wd.shape)],
        out_specs=tile(d),
        out_shape=jax.ShapeDtypeStruct((bsz, t, d), F32),
        compiler_params=_params(2),
        name="merge_ffn",
    )(x, mods, o_att, o_hg, mg, g_post1[None, :], wa, wh, wo, g_pre2[None, :], g_post2[None, :], wg, wu, wd)


def _rope_tables(t):
    pos = jnp.arange(t, dtype=jnp.int32)
    row = (pos // GRID_W).astype(F32)
    colp = (pos % GRID_W).astype(F32)
    inv_freq = ROPE_THETA ** (-jnp.arange(ROPE_PAIRS, dtype=F32) / ROPE_PAIRS)
    ang_r = row[:, None] * inv_freq
    ang_c = colp[:, None] * inv_freq
    ang = jnp.concatenate([ang_r, ang_r, ang_c, ang_c], axis=-1)
    cos, sin = jnp.cos(ang), jnp.sin(ang)
    first = (jnp.arange(HEAD_DIM) % (2 * ROPE_PAIRS)) < ROPE_PAIRS
    sin_lo = jnp.where(first, -sin, 0.0)
    sin_hi = jnp.where(first, 0.0, sin)
    two = lambda a: jnp.concatenate([a, a], axis=-1)
    return two(cos), two(sin_lo), two(sin_hi)


def kernel(x, c, ctx, c_ctx, w_mod, b_mod, norm_pre, norm_post, ffn_w_gate, ffn_w_up, ffn_w_down,
           w_in, q_norm, k_norm, hg_lower_bound, hg_norm, w_att_out, w_hg_out, w_o):
    assert w_in.shape[0] == 1, "single-layer block"
    bsz, t, d = x.shape
    tc = ctx.shape[1]
    assert t % GRID_W == 0 and t % CHUNK == 0 and tc % CHUNK == 0
    tm = min(256, t)
    tmc = min(256, tc)
    tmf = 512 if t % 512 == 0 else tm

    rows = -(-(bsz + 1) // 8) * 8
    cvec = jnp.concatenate([c, c_ctx[None, :], jnp.zeros((rows - bsz - 1, d), c.dtype)], axis=0)
    mods = _modulation(cvec, w_mod[0], b_mod[0])[:, None, :]
    lat_row = lambda b: b
    ctx_row = lambda b: bsz

    wg, wu, wd = ([w[0, s].astype(BF16) for s in range(2)] for w in (ffn_w_gate, ffn_w_up, ffn_w_down))
    w_in_b = w_in[0].astype(BF16)

    tml = 1024 if t % 1024 == 0 else tmf
    x1 = _ffn(x, mods, lat_row, 0, norm_pre[0, 0], norm_post[0, 0], wg[0], wu[0], wd[0], tml)
    h1 = _ffn(ctx, mods, ctx_row, 0, norm_pre[0, 0], norm_post[0, 0], wg[0], wu[0], wd[0], tmc)

    cos, slo, shi = _rope_tables(t)
    q_gain2 = jnp.concatenate([q_norm[0], q_norm[0]])[None, :]
    k_gain2 = jnp.concatenate([k_norm[0], k_norm[0]])[None, :]
    slots = hg_lower_bound.shape[1]
    lb_raw = jnp.transpose(hg_lower_bound.astype(F32), (1, 0, 2)).reshape(slots, 2 * HG_WIDTH)
    q, k, v, hq, hv, ff, og, mg = _inproj_latent(x1, mods, norm_pre[0, 1], w_in_b, q_gain2, k_gain2, lb_raw,
                                                  cos, slo, shi, tmf)
    ck, cv, chv, cff = _inproj_ctx(h1, mods, bsz, norm_pre[0, 1], w_in_b, k_gain2, lb_raw, tmc)

    o_att = _attention(q, k, ck, v, cv, tm, tmf)

    o_hg = _hgrn(hg_norm[0], hq, hv, ff, og, chv, cff)

    return _merge_ffn(x1, mods, o_att, o_hg, mg, norm_post[0, 1], w_att_out[0].astype(BF16),
                      w_hg_out[0].astype(BF16), w_o[0].astype(BF16),
                      norm_pre[0, 2], norm_post[0, 2], wg[1], wu[1], wd[1], tmf)
```

```python
import functools

import jax
import jax.numpy as jnp
from jax import lax
from jax.experimental import pallas as pl
from jax.experimental.pallas import tpu as pltpu

EPS = 1e-6
N_MOD = 9
GRID_W = 64
ROPE_THETA = 10000.0
HEAD_DIM = 64
N_Q_HEADS = 8
N_KV_HEADS = 2
GROUP = N_Q_HEADS // N_KV_HEADS
ATT_WIDTH = N_Q_HEADS * HEAD_DIM
KV_WIDTH = N_KV_HEADS * HEAD_DIM
ROPE_PAIRS = HEAD_DIM // 4
ATT_SCALE = HEAD_DIM ** -0.5
LOG2E = 1.4426950408889634
HG_HEADS = 4
HG_DK = 128
HG_DV = 128
HG_WIDTH = HG_HEADS * HG_DK
HG_SCALE = HG_DK ** -0.5
CHUNK = 64
LANES = 128
SUB_TILE = 256
VMEM_LIMIT = 56 * 1024 * 1024

BF16 = jnp.bfloat16
F32 = jnp.float32

NT_DIMS = (((1,), (1,)), ((), ()))
TN_DIMS = (((0,), (0,)), ((), ()))


def _dot(a, b):
    return jnp.dot(a, b, preferred_element_type=F32)


def _rms(x, gain):
    return x * lax.rsqrt(jnp.mean(x * x, axis=-1, keepdims=True) + EPS) * gain


def _sigmoid(x):
    return 1.0 / (1.0 + jnp.exp(-x))


def _silu(x):
    return x * _sigmoid(x)


def _params(n_grid):
    return pltpu.CompilerParams(dimension_semantics=("parallel",) * n_grid, vmem_limit_bytes=VMEM_LIMIT)


def _const_spec(shape):
    nd = len(shape)
    return pl.BlockSpec(shape, lambda *_: (0,) * nd, pipeline_mode=pl.Buffered(1))


def _mod_kernel(c_ref, w_ref, b_ref, o_ref):
    a = _silu(c_ref[...]).astype(BF16)
    o_ref[...] = _dot(a, w_ref[...].astype(BF16)) + b_ref[...]


def _modulation(cvec, w_mod, b_mod, tn=1024):
    rows, d = cvec.shape
    n = w_mod.shape[1]
    return pl.pallas_call(
        _mod_kernel,
        grid=(n // tn,),
        in_specs=[pl.BlockSpec((rows, d), lambda j: (0, 0)),
                  pl.BlockSpec((d, tn), lambda j: (0, j)),
                  pl.BlockSpec((1, tn), lambda j: (0, j))],
        out_specs=pl.BlockSpec((rows, tn), lambda j: (0, j)),
        out_shape=jax.ShapeDtypeStruct((rows, n), F32),
        compiler_params=_params(1),
        name="mod",
    )(cvec, w_mod, b_mod[None, :])


def _ffn_half_step(x, m_ref, gpre_ref, gpost_ref, wg_ref, wu_ref, wd_ref, mod0, d):
    shift = m_ref[:, (mod0 + 0) * d:(mod0 + 1) * d]
    scale = m_ref[:, (mod0 + 1) * d:(mod0 + 2) * d]
    gate = m_ref[:, (mod0 + 2) * d:(mod0 + 3) * d]
    u = (_rms(x, gpre_ref[...]) * (1.0 + scale) + shift).astype(BF16)
    h = (_silu(_dot(u, wg_ref[...])) * _dot(u, wu_ref[...])).astype(BF16)
    y = _dot(h, wd_ref[...])
    return x + 0.5 * (gate * _rms(y, gpost_ref[...]))


def _sub_tiles(rows):
    sub = SUB_TILE if rows % SUB_TILE == 0 else rows
    return [slice(r, r + sub) for r in range(0, rows, sub)]


def _ffn_kernel(x_ref, m_ref, gpre_ref, gpost_ref, wg_ref, wu_ref, wd_ref, o_ref, *, mod0, d):
    for rows in _sub_tiles(x_ref.shape[0]):
        o_ref[rows, :] = _ffn_half_step(x_ref[rows, :], m_ref, gpre_ref, gpost_ref, wg_ref, wu_ref, wd_ref,
                                        mod0, d)


def _ffn(x, mods, mod_row, mod0, g_pre, g_post, wg, wu, wd, tm):
    bsz, t, d = x.shape
    f = wg.shape[1]
    nm = mods.shape[-1]
    return pl.pallas_call(
        functools.partial(_ffn_kernel, mod0=mod0, d=d),
        grid=(bsz, t // tm),
        in_specs=[pl.BlockSpec((None, tm, d), lambda b, i: (b, i, 0)),
                  pl.BlockSpec((None, 1, nm), lambda b, i: (mod_row(b), 0, 0)),
                  _const_spec((1, d)), _const_spec((1, d)),
                  _const_spec((d, f)), _const_spec((d, f)), _const_spec((f, d))],
        out_specs=pl.BlockSpec((None, tm, d), lambda b, i: (b, i, 0)),
        out_shape=jax.ShapeDtypeStruct((bsz, t, d), F32),
        compiler_params=_params(2),
        name="ffn",
    )(x, mods, g_pre[None, :], g_post[None, :], wg, wu, wd)


def _log2_forget(raw, lbraw_ref):
    slots = [lbraw_ref[s:s + 1, :] for s in range(lbraw_ref.shape[0])]
    top = functools.reduce(jnp.maximum, slots)
    e = [jnp.exp(s - top) for s in slots]
    lb = e[0] / functools.reduce(jnp.add, e)
    return jnp.log2(lb + (1.0 - lb) * _sigmoid(raw))


def _head_rms64(z, gain):
    lane = lax.broadcasted_iota(jnp.int32, (1, LANES), 1)
    first = lane < HEAD_DIM
    sq = z * z
    lo = jnp.sum(jnp.where(first, sq, 0.0), axis=-1, keepdims=True)
    hi = jnp.sum(jnp.where(first, 0.0, sq), axis=-1, keepdims=True)
    ms = jnp.where(first, lo, hi) * (1.0 / HEAD_DIM)
    return z * lax.rsqrt(ms + EPS) * gain


def _rope128(z, cos, sin_lo, sin_hi):
    q = ROPE_PAIRS
    return z * cos + pltpu.roll(z, LANES - q, 1) * sin_lo + pltpu.roll(z, q, 1) * sin_hi


def _inproj_latent_kernel(x_ref, m_ref, gpre_ref, w_ref, qg_ref, kg_ref, lbraw_ref, cos_ref, slo_ref, shi_ref,
                          q_ref, k_ref, v_ref, hq_ref, hv_ref, ff_ref, og_ref, mg_ref, *, d):
    shift = m_ref[:, 3 * d:4 * d]
    scale = m_ref[:, 4 * d:5 * d]
    for rows in _sub_tiles(x_ref.shape[0]):
        u = (_rms(x_ref[rows, :], gpre_ref[...]) * (1.0 + scale) + shift).astype(BF16)
        cos, slo, shi = cos_ref[rows, :], slo_ref[rows, :], shi_ref[rows, :]

        def proj(lo, hi):
            return _dot(u, w_ref[:, lo:hi])

        c0 = 0
        pq = proj(c0, c0 + ATT_WIDTH)
        qs = []
        for j in range(ATT_WIDTH // LANES):
            z = _head_rms64(pq[:, j * LANES:(j + 1) * LANES], qg_ref[...])
            qs.append(_rope128(z, cos, slo, shi) * (ATT_SCALE * LOG2E))
        q_ref[rows, :] = jnp.concatenate(qs, axis=-1).astype(BF16)
        c0 += ATT_WIDTH
        pkv = proj(c0, c0 + 2 * KV_WIDTH)
        k_ref[rows, :] = _rope128(_head_rms64(pkv[:, :KV_WIDTH], kg_ref[...]), cos, slo, shi).astype(BF16)
        v_ref[rows, :] = pkv[:, KV_WIDTH:].astype(BF16)
        c0 += 2 * KV_WIDTH
        hq_ref[rows, :] = _silu(proj(c0, c0 + HG_WIDTH)) * HG_SCALE
        c0 += HG_WIDTH
        hv_ref[rows, :] = proj(c0, c0 + HG_WIDTH).astype(BF16)
        c0 += HG_WIDTH
        ff_ref[rows, :] = _log2_forget(proj(c0, c0 + 2 * HG_WIDTH), lbraw_ref)
        c0 += 2 * HG_WIDTH
        og_ref[rows, :] = proj(c0, c0 + HG_WIDTH)
        c0 += HG_WIDTH
        mg_ref[rows, :] = proj(c0, c0 + 2 * d)


def _inproj_latent(x, mods, g_pre, w_in, q_gain2, k_gain2, lb_raw, cos, slo, shi, tm):
    bsz, t, d = x.shape
    nm = mods.shape[-1]
    n_in = w_in.shape[1]

    def tile(width, dtype):
        return (pl.BlockSpec((None, tm, width), lambda b, i: (b, i, 0)),
                jax.ShapeDtypeStruct((bsz, t, width), dtype))

    outs = [tile(ATT_WIDTH, BF16), tile(KV_WIDTH, BF16), tile(KV_WIDTH, BF16), tile(HG_WIDTH, F32),
            tile(HG_WIDTH, BF16), tile(2 * HG_WIDTH, F32), tile(HG_WIDTH, F32), tile(2 * d, F32)]
    rope_spec = pl.BlockSpec((tm, LANES), lambda b, i: (i, 0))
    return pl.pallas_call(
        functools.partial(_inproj_latent_kernel, d=d),
        grid=(bsz, t // tm),
        in_specs=[pl.BlockSpec((None, tm, d), lambda b, i: (b, i, 0)),
                  pl.BlockSpec((None, 1, nm), lambda b, i: (b, 0, 0)),
                  _const_spec((1, d)), _const_spec((d, n_in)),
                  _const_spec((1, LANES)), _const_spec((1, LANES)), _const_spec(lb_raw.shape),
                  rope_spec, rope_spec, rope_spec],
        out_specs=[o[0] for o in outs],
        out_shape=[o[1] for o in outs],
        compiler_params=_params(2),
        name="inproj_latent",
    )(x, mods, g_pre[None, :], w_in, q_gain2, k_gain2, lb_raw, cos, slo, shi)


def _inproj_ctx_kernel(x_ref, m_ref, gpre_ref, w_ref, kg_ref, lbraw_ref, k_ref, v_ref, hv_ref, ff_ref, *, d):
    x = x_ref[...]
    shift = m_ref[:, 3 * d:4 * d]
    scale = m_ref[:, 4 * d:5 * d]
    u = (_rms(x, gpre_ref[...]) * (1.0 + scale) + shift).astype(BF16)
    kv0 = ATT_WIDTH
    h0 = ATT_WIDTH + 2 * KV_WIDTH + HG_WIDTH
    pkv = _dot(u, w_ref[:, kv0:kv0 + 2 * KV_WIDTH])
    k_ref[...] = _head_rms64(pkv[:, :KV_WIDTH], kg_ref[...]).astype(BF16)
    v_ref[...] = pkv[:, KV_WIDTH:].astype(BF16)
    hv_ref[...] = _dot(u, w_ref[:, h0:h0 + HG_WIDTH]).astype(BF16)
    ff_ref[...] = _log2_forget(_dot(u, w_ref[:, h0 + HG_WIDTH:h0 + 3 * HG_WIDTH]), lbraw_ref)


def _inproj_ctx(x, mods, ctx_row, g_pre, w_in, k_gain2, lb_raw, tm):
    bsz, t, d = x.shape
    nm = mods.shape[-1]

    def tile(width, dtype):
        return (pl.BlockSpec((None, tm, width), lambda b, i: (b, i, 0)),
                jax.ShapeDtypeStruct((bsz, t, width), dtype))

    outs = [tile(KV_WIDTH, BF16), tile(KV_WIDTH, BF16), tile(HG_WIDTH, BF16), tile(2 * HG_WIDTH, F32)]
    return pl.pallas_call(
        functools.partial(_inproj_ctx_kernel, d=d),
        grid=(bsz, t // tm),
        in_specs=[pl.BlockSpec((None, tm, d), lambda b, i: (b, i, 0)),
                  pl.BlockSpec((None, 1, nm), lambda b, i: (ctx_row, 0, 0)),
                  _const_spec((1, d)), _const_spec(w_in.shape),
                  _const_spec((1, LANES)), _const_spec(lb_raw.shape)],
        out_specs=[o[0] for o in outs],
        out_shape=[o[1] for o in outs],
        compiler_params=_params(2),
        name="inproj_ctx",
    )(x, mods, g_pre[None, :], w_in, k_gain2, lb_raw)


def _attn_kernel(q_ref, kl_ref, kc_ref, vl_ref, vc_ref, o_ref, st_ref, pt_ref, vt_ref):
    tq = st_ref.shape[2]
    t_lat = kl_ref.shape[0]
    n_items = (q_ref.shape[0] // tq) * N_Q_HEADS

    @pl.when(pl.program_id(1) == 0)
    def _():
        vt_ref[:, :t_lat] = vl_ref[...].astype(F32).T.astype(BF16)
        vt_ref[:, t_lat:] = vc_ref[...].astype(F32).T.astype(BF16)

    def scores(i):
        r, h = divmod(i, N_Q_HEADS)
        q = q_ref[r * tq:(r + 1) * tq, h * HEAD_DIM:(h + 1) * HEAD_DIM]
        halves = [q, jnp.zeros_like(q)] if h // GROUP == 0 else [jnp.zeros_like(q), q]
        qp = jnp.concatenate(halves, axis=1)
        s_lat = lax.dot_general(kl_ref[...], qp, NT_DIMS, preferred_element_type=F32)
        s_ctx = lax.dot_general(kc_ref[...], qp, NT_DIMS, preferred_element_type=F32)
        buf = i % st_ref.shape[0]
        st_ref[buf, :t_lat, :] = s_lat
        st_ref[buf, t_lat:, :] = s_ctx
        return jnp.maximum(jnp.max(s_lat, axis=0, keepdims=True), jnp.max(s_ctx, axis=0, keepdims=True))

    pairs, sums = [], []

    def values(i):
        kv = (i % N_Q_HEADS) // GROUP
        ot = _dot(vt_ref[kv * HEAD_DIM:(kv + 1) * HEAD_DIM, :], pt_ref[i % 2])
        pairs.append(ot / sums[i])

    ahead = st_ref.shape[0] - 1
    ms = [scores(i) for i in range(ahead)]
    for i in range(n_items):
        if i + ahead < n_items:
            ms.append(scores(i + ahead))
        p = jnp.exp2(st_ref[i % (ahead + 1)] - ms[i])
        sums.append(jnp.sum(p, axis=0, keepdims=True))
        pt_ref[i % 2] = p.astype(BF16)
        if i:
            values(i - 1)
    values(n_items - 1)
    for r in range(n_items // N_Q_HEADS):
        heads = pairs[r * N_Q_HEADS:(r + 1) * N_Q_HEADS]
        outs = [jnp.concatenate(heads[j:j + 2], axis=0).T for j in range(0, N_Q_HEADS, 2)]
        o_ref[r * tq:(r + 1) * tq, :] = jnp.concatenate(outs, axis=-1).astype(BF16)


def _attention(q, k_lat, k_ctx, v_lat, v_ctx, tq, tstep):
    bsz, t, _ = q.shape
    tc = k_ctx.shape[1]
    s = t + tc

    def whole(tt):
        return pl.BlockSpec((None, tt, KV_WIDTH), lambda b, i: (b, 0, 0))

    return pl.pallas_call(
        _attn_kernel,
        grid=(bsz, t // tstep),
        in_specs=[pl.BlockSpec((None, tstep, ATT_WIDTH), lambda b, i: (b, i, 0)),
                  whole(t), whole(tc), whole(t), whole(tc)],
        out_specs=pl.BlockSpec((None, tstep, ATT_WIDTH), lambda b, i: (b, i, 0)),
        out_shape=jax.ShapeDtypeStruct((bsz, t, ATT_WIDTH), BF16),
        scratch_shapes=[pltpu.VMEM((3, s, tq), F32),
                        pltpu.VMEM((2, s, tq), BF16),
                        pltpu.VMEM((KV_WIDTH, s), BF16)],
        compiler_params=pltpu.CompilerParams(dimension_semantics=("parallel", "arbitrary"),
                                             vmem_limit_bytes=VMEM_LIMIT),
        name="attn",
    )(q, k_lat, k_ctx, v_lat, v_ctx)


def _split3(g):
    g1 = g.astype(BF16)
    r1 = g - g1.astype(F32)
    g2 = r1.astype(BF16)
    g3 = (r1 - g2.astype(F32)).astype(BF16)
    return g1, g2, g3


def _hgrn_bidir_kernel(gain_ref, hq_ref, hv_ref, ff_ref, fb_ref, og_ref, cv_ref, cff_ref, cfb_ref,
                       o_ref, acc_ref, qe_ref, ds_ref, dec_ref, st_ref, bk_ref, ke_ref, a_ref,
                       *, n_lat, n_ctx, cpb):
    c = CHUNK
    r = cpb * c
    dk = HG_DK
    row = lax.broadcasted_iota(jnp.int32, (r, r), 0)
    col = lax.broadcasted_iota(jnp.int32, (r, r), 1)
    same_chunk = (row // c) == (col // c)
    masks = (same_chunk & (col <= row), same_chunk & (col >= row))
    tris = tuple(jnp.where(m, 1.0, 0.0).astype(BF16) for m in masks)
    last = (c - 1, 0)
    mid = (c // 2 - 1, c // 2)

    def per_chunk_rows(x, off):
        return jnp.concatenate([jnp.broadcast_to(x[j * c + off:j * c + off + 1, :], (c, x.shape[1]))
                                for j in range(cpb)], axis=0)

    def lane_block(d, j):
        return slice((d * cpb + j) * dk, (d * cpb + j + 1) * dk)

    ke_ref[...] = jnp.zeros(ke_ref.shape, BF16)

    blocks = ([((cff_ref, cfb_ref), cv_ref, None, i * r, i * cpb) for i in range(n_ctx // cpb)]
              + [((ff_ref, fb_ref), hv_ref, hq_ref, i * r, n_ctx + i * cpb) for i in range(n_lat // cpb)])

    def decays(n):
        f_refs, _, _, r0, _ = blocks[n]
        for d in range(2):
            lf = f_refs[d][r0:r0 + r, :]
            g1, g2, g3 = _split3(lf)
            bb = _dot(tris[d], jnp.concatenate([g1, g2, g3], axis=1))
            bk_ref[n % 2, d] = bb[:, :dk] + bb[:, dk:2 * dk] + bb[:, 2 * dk:]
            bk_ref[n % 2, 2 + d] = 1.0 - jnp.exp2(lf)

    def scores(n):
        _, _, q_ref, r0, ch0 = blocks[n]
        q = None if q_ref is None else q_ref[r0:r0 + r, :]
        amat = None
        for d in range(2):
            b, k = bk_ref[n % 2, d], bk_ref[n % 2, 2 + d]
            b_last = per_chunk_rows(b, last[d])
            ke = (k * jnp.exp2(b_last - b)).astype(BF16)
            for j in range(cpb):
                ke_ref[n % 2, j * c:(j + 1) * c, lane_block(d, j)] = ke[j * c:(j + 1) * c, :]
                dec_ref[d, ch0 + j] = jnp.exp2(b[j * c + last[d]:j * c + last[d] + 1, :])
            if q is None:
                continue
            b_mid = per_chunk_rows(b, mid[d])
            qd = (q * jnp.exp2(b - b_mid)).astype(BF16)
            kd = (k * jnp.exp2(b_mid - b)).astype(BF16)
            a = jnp.where(masks[d], lax.dot_general(qd, kd, NT_DIMS, preferred_element_type=F32), 0.0)
            amat = a if amat is None else amat + a
            qe_ref[r0:r0 + r, d * dk:(d + 1) * dk] = (q * jnp.exp2(b)).astype(BF16)
        if q is not None:
            a_ref[n % 2] = amat.astype(BF16)

    def products(n):
        _, v_ref, q_ref, r0, ch0 = blocks[n]
        v = v_ref[r0:r0 + r, :]
        ds = lax.dot_general(v, ke_ref[n % 2], TN_DIMS, preferred_element_type=F32)
        for d in range(2):
            for j in range(cpb):
                lo = (d * cpb + j) * dk
                ds_ref[d, ch0 + j] = ds[:, lo:lo + dk]
        if q_ref is not None:
            acc_ref[r0:r0 + r, :] = _dot(a_ref[n % 2], v)

    decays(0)
    for n in range(len(blocks)):
        if n + 1 < len(blocks):
            decays(n + 1)
        scores(n)
        if n:
            products(n - 1)
    products(len(blocks) - 1)

    for d in range(2):
        def ctx_step(i, st):
            ch = (n_ctx - 1 - i) if d else i
            return st * dec_ref[d, ch] + ds_ref[d, ch]

        def lat_steps(i, st):
            blk = (n_lat // cpb - 1 - i) if d else i
            for jj in range(cpb):
                j = (cpb - 1 - jj) if d else jj
                st_ref[blk * cpb + j, :, d * dk:(d + 1) * dk] = st.astype(BF16)
                ch = n_ctx + blk * cpb + j
                st = st * dec_ref[d, ch] + ds_ref[d, ch]
            return st

        st = lax.fori_loop(0, n_ctx, ctx_step, jnp.zeros((HG_DV, dk), F32), unroll=True)
        lax.fori_loop(0, n_lat // cpb, lat_steps, st)

    def pass3(i, carry):
        for j in range(cpb):
            rows = pl.ds(pl.multiple_of(i * r + j * c, c), c)
            inter = lax.dot_general(qe_ref[rows, :], st_ref[i * cpb + j], NT_DIMS, preferred_element_type=F32)
            tot = acc_ref[rows, :] + inter
            o_ref[rows, :] = (_rms(tot, gain_ref[...]) * _silu(og_ref[rows, :])).astype(BF16)
        return carry

    lax.fori_loop(0, n_lat // cpb, pass3, 0, unroll=4)


def _hgrn(hg_gain, hq, hv, ff, og, cv, cff):
    bsz, t, _ = hq.shape
    tc = cv.shape[1]

    def col(tt, off=0):
        return pl.BlockSpec((None, tt, HG_DK), lambda b, h: (b, 0, h + off))

    n_lat, n_ctx = t // CHUNK, tc // CHUNK
    cpb = next(n for n in (4, 2, 1) if n_lat % n == 0 and n_ctx % n == 0)
    return pl.pallas_call(
        functools.partial(_hgrn_bidir_kernel, n_lat=n_lat, n_ctx=n_ctx, cpb=cpb),
        grid=(bsz, HG_HEADS),
        in_specs=[pl.BlockSpec((1, HG_DV), lambda b, h: (0, 0)),
                  col(t), col(t), col(t), col(t, HG_HEADS), col(t),
                  col(tc), col(tc), col(tc, HG_HEADS)],
        out_specs=col(t),
        out_shape=jax.ShapeDtypeStruct((bsz, t, HG_WIDTH), BF16),
        scratch_shapes=[pltpu.VMEM((t, HG_DV), F32),
                        pltpu.VMEM((t, 2 * HG_DK), BF16),
                        pltpu.VMEM((2, n_ctx + n_lat, HG_DV, HG_DK), F32),
                        pltpu.VMEM((2, n_ctx + n_lat, 1, HG_DK), F32),
                        pltpu.VMEM((n_lat, HG_DV, 2 * HG_DK), BF16),
                        pltpu.VMEM((2, 4, cpb * CHUNK, HG_DK), F32),
                        pltpu.VMEM((2, cpb * CHUNK, 2 * cpb * HG_DK), BF16),
                        pltpu.VMEM((2, cpb * CHUNK, cpb * CHUNK), BF16)],
        compiler_params=_params(2),
        name="hgrn",
    )(hg_gain[None, :], hq, hv, ff, ff, og, cv, cff, cff)


def _merge_ffn_kernel(x_ref, m_ref, oa_ref, oh_ref, mg_ref, gpost1_ref, wa_ref, wh_ref, wo_ref,
                      gpre2_ref, gpost2_ref, wg_ref, wu_ref, wd_ref, o_ref, *, d):
    gate = m_ref[:, 5 * d:6 * d]
    for rows in _sub_tiles(x_ref.shape[0]):
        y = (_sigmoid(mg_ref[rows, :d]) * _dot(oa_ref[rows, :], wa_ref[...])
             + _sigmoid(mg_ref[rows, d:]) * _dot(oh_ref[rows, :], wh_ref[...]))
        z = _dot(y.astype(BF16), wo_ref[...])
        x2 = x_ref[rows, :] + gate * _rms(z, gpost1_ref[...])
        o_ref[rows, :] = _ffn_half_step(x2, m_ref, gpre2_ref, gpost2_ref, wg_ref, wu_ref, wd_ref, 6, d)


def _merge_ffn(x, mods, o_att, o_hg, mg, g_post1, wa, wh, wo, g_pre2, g_post2, wg, wu, wd, tm):
    bsz, t, d = x.shape
    nm = mods.shape[-1]

    def tile(width):
        return pl.BlockSpec((None, tm, width), lambda b, i: (b, i, 0))

    return pl.pallas_call(
        functools.partial(_merge_ffn_kernel, d=d),
        grid=(bsz, t // tm),
        in_specs=[tile(d), pl.BlockSpec((None, 1, nm), lambda b, i: (b, 0, 0)),
                  tile(ATT_WIDTH), tile(HG_WIDTH), tile(2 * d), _const_spec((1, d)),
                  _const_spec(wa.shape), _const_spec(wh.shape), _const_spec(wo.shape),
                  _const_spec((1, d)), _const_spec((1, d)),
                  _const_spec(wg.shape), _const_spec(wu.shape), _const_spec(wd.shape)],
        out_specs=tile(d),
        out_shape=jax.ShapeDtypeStruct((bsz, t, d), F32),
        compiler_params=_params(2),
        name="merge_ffn",
    )(x, mods, o_att, o_hg, mg, g_post1[None, :], wa, wh, wo, g_pre2[None, :], g_post2[None, :], wg, wu, wd)


def _rope_tables(t):
    pos = jnp.arange(t, dtype=jnp.int32)
    row = (pos // GRID_W).astype(F32)
    colp = (pos % GRID_W).astype(F32)
    inv_freq = ROPE_THETA ** (-jnp.arange(ROPE_PAIRS, dtype=F32) / ROPE_PAIRS)
    ang_r = row[:, None] * inv_freq
    ang_c = colp[:, None] * inv_freq
    ang = jnp.concatenate([ang_r, ang_r, ang_c, ang_c], axis=-1)
    cos, sin = jnp.cos(ang), jnp.sin(ang)
    first = (jnp.arange(HEAD_DIM) % (2 * ROPE_PAIRS)) < ROPE_PAIRS
    sin_lo = jnp.where(first, -sin, 0.0)
    sin_hi = jnp.where(first, 0.0, sin)
    two = lambda a: jnp.concatenate([a, a], axis=-1)
    return two(cos), two(sin_lo), two(sin_hi)


def kernel(x, c, ctx, c_ctx, w_mod, b_mod, norm_pre, norm_post, ffn_w_gate, ffn_w_up, ffn_w_down,
           w_in, q_norm, k_norm, hg_lower_bound, hg_norm, w_att_out, w_hg_out, w_o):
    assert w_in.shape[0] == 1, "single-layer block"
    bsz, t, d = x.shape
    tc = ctx.shape[1]
    assert t % GRID_W == 0 and t % CHUNK == 0 and tc % CHUNK == 0
    tm = min(256, t)
    tmc = min(256, tc)
    tmf = 512 if t % 512 == 0 else tm

    rows = -(-(bsz + 1) // 8) * 8
    cvec = jnp.concatenate([c, c_ctx[None, :], jnp.zeros((rows - bsz - 1, d), c.dtype)], axis=0)
    mods = _modulation(cvec, w_mod[0], b_mod[0])[:, None, :]
    lat_row = lambda b: b
    ctx_row = lambda b: bsz

    wg, wu, wd = ([w[0, s].astype(BF16) for s in range(2)] for w in (ffn_w_gate, ffn_w_up, ffn_w_down))
    w_in_b = w_in[0].astype(BF16)

    tml = 1024 if t % 1024 == 0 else tmf
    x1 = _ffn(x, mods, lat_row, 0, norm_pre[0, 0], norm_post[0, 0], wg[0], wu[0], wd[0], tml)
    h1 = _ffn(ctx, mods, ctx_row, 0, norm_pre[0, 0], norm_post[0, 0], wg[0], wu[0], wd[0], tmc)

    cos, slo, shi = _rope_tables(t)
    q_gain2 = jnp.concatenate([q_norm[0], q_norm[0]])[None, :]
    k_gain2 = jnp.concatenate([k_norm[0], k_norm[0]])[None, :]
    slots = hg_lower_bound.shape[1]
    lb_raw = jnp.transpose(hg_lower_bound.astype(F32), (1, 0, 2)).reshape(slots, 2 * HG_WIDTH)
    q, k, v, hq, hv, ff, og, mg = _inproj_latent(x1, mods, norm_pre[0, 1], w_in_b, q_gain2, k_gain2, lb_raw,
                                                  cos, slo, shi, tmf)
    ck, cv, chv, cff = _inproj_ctx(h1, mods, bsz, norm_pre[0, 1], w_in_b, k_gain2, lb_raw, tmc)

    o_att = _attention(q, k, ck, v, cv, tm, tml)

    o_hg = _hgrn(hg_norm[0], hq, hv, ff, og, chv, cff)

    return _merge_ffn(x1, mods, o_att, o_hg, mg, norm_post[0, 1], w_att_out[0].astype(BF16),
                      w_hg_out[0].astype(BF16), w_o[0].astype(BF16),
                      norm_pre[0, 2], norm_post[0, 2], wg[1], wu[1], wd[1], tmf)
```

```python
import functools

import jax
import jax.numpy as jnp
from jax import lax
from jax.experimental import pallas as pl
from jax.experimental.pallas import tpu as pltpu

EPS = 1e-6
N_MOD = 9
GRID_W = 64
ROPE_THETA = 10000.0
HEAD_DIM = 64
N_Q_HEADS = 8
N_KV_HEADS = 2
GROUP = N_Q_HEADS // N_KV_HEADS
ATT_WIDTH = N_Q_HEADS * HEAD_DIM
KV_WIDTH = N_KV_HEADS * HEAD_DIM
ROPE_PAIRS = HEAD_DIM // 4
ATT_SCALE = HEAD_DIM ** -0.5
LOG2E = 1.4426950408889634
VT_ROWS = HEAD_DIM + 16
HG_HEADS = 4
HG_DK = 128
HG_DV = 128
HG_WIDTH = HG_HEADS * HG_DK
HG_SCALE = HG_DK ** -0.5
CHUNK = 64
LANES = 128
SUB_TILE = 256
VMEM_LIMIT = 56 * 1024 * 1024

BF16 = jnp.bfloat16
F32 = jnp.float32

NT_DIMS = (((1,), (1,)), ((), ()))
TN_DIMS = (((0,), (0,)), ((), ()))


def _dot(a, b):
    return jnp.dot(a, b, preferred_element_type=F32)


def _rms(x, gain):
    return x * lax.rsqrt(jnp.mean(x * x, axis=-1, keepdims=True) + EPS) * gain


def _sigmoid(x):
    return 1.0 / (1.0 + jnp.exp(-x))


def _silu(x):
    return x * _sigmoid(x)


def _params(n_grid):
    return pltpu.CompilerParams(dimension_semantics=("parallel",) * n_grid, vmem_limit_bytes=VMEM_LIMIT)


def _const_spec(shape):
    nd = len(shape)
    return pl.BlockSpec(shape, lambda *_: (0,) * nd, pipeline_mode=pl.Buffered(1))


def _mod_kernel(c_ref, w_ref, b_ref, o_ref):
    a = _silu(c_ref[...]).astype(BF16)
    o_ref[...] = _dot(a, w_ref[...].astype(BF16)) + b_ref[...]


def _modulation(cvec, w_mod, b_mod, tn=1024):
    rows, d = cvec.shape
    n = w_mod.shape[1]
    return pl.pallas_call(
        _mod_kernel,
        grid=(n // tn,),
        in_specs=[pl.BlockSpec((rows, d), lambda j: (0, 0)),
                  pl.BlockSpec((d, tn), lambda j: (0, j)),
                  pl.BlockSpec((1, tn), lambda j: (0, j))],
        out_specs=pl.BlockSpec((rows, tn), lambda j: (0, j)),
        out_shape=jax.ShapeDtypeStruct((rows, n), F32),
        compiler_params=_params(1),
        name="mod",
    )(cvec, w_mod, b_mod[None, :])


def _ffn_half_step(x, m_ref, gpre_ref, gpost_ref, wg_ref, wu_ref, wd_ref, mod0, d):
    shift = m_ref[:, (mod0 + 0) * d:(mod0 + 1) * d]
    scale = m_ref[:, (mod0 + 1) * d:(mod0 + 2) * d]
    gate = m_ref[:, (mod0 + 2) * d:(mod0 + 3) * d]
    u = (_rms(x, gpre_ref[...]) * (1.0 + scale) + shift).astype(BF16)
    h = (_silu(_dot(u, wg_ref[...])) * _dot(u, wu_ref[...])).astype(BF16)
    y = _dot(h, wd_ref[...])
    return x + 0.5 * (gate * _rms(y, gpost_ref[...]))


def _sub_tiles(rows):
    sub = SUB_TILE if rows % SUB_TILE == 0 else rows
    return [slice(r, r + sub) for r in range(0, rows, sub)]


def _ffn_kernel(x_ref, m_ref, gpre_ref, gpost_ref, wg_ref, wu_ref, wd_ref, o_ref, *, mod0, d):
    for rows in _sub_tiles(x_ref.shape[0]):
        o_ref[rows, :] = _ffn_half_step(x_ref[rows, :], m_ref, gpre_ref, gpost_ref, wg_ref, wu_ref, wd_ref,
                                        mod0, d)


def _ffn(x, mods, mod_row, mod0, g_pre, g_post, wg, wu, wd, tm):
    bsz, t, d = x.shape
    f = wg.shape[1]
    nm = mods.shape[-1]
    return pl.pallas_call(
        functools.partial(_ffn_kernel, mod0=mod0, d=d),
        grid=(bsz, t // tm),
        in_specs=[pl.BlockSpec((None, tm, d), lambda b, i: (b, i, 0)),
                  pl.BlockSpec((None, 1, nm), lambda b, i: (mod_row(b), 0, 0)),
                  _const_spec((1, d)), _const_spec((1, d)),
                  _const_spec((d, f)), _const_spec((d, f)), _const_spec((f, d))],
        out_specs=pl.BlockSpec((None, tm, d), lambda b, i: (b, i, 0)),
        out_shape=jax.ShapeDtypeStruct((bsz, t, d), F32),
        compiler_params=_params(2),
        name="ffn",
    )(x, mods, g_pre[None, :], g_post[None, :], wg, wu, wd)


def _log2_forget(raw, lbraw_ref):
    slots = [lbraw_ref[s:s + 1, :] for s in range(lbraw_ref.shape[0])]
    top = functools.reduce(jnp.maximum, slots)
    e = [jnp.exp(s - top) for s in slots]
    lb = e[0] / functools.reduce(jnp.add, e)
    return jnp.log2(lb + (1.0 - lb) * _sigmoid(raw))


def _head_rms64(z, gain):
    lane = lax.broadcasted_iota(jnp.int32, (1, LANES), 1)
    first = lane < HEAD_DIM
    sq = z * z
    lo = jnp.sum(jnp.where(first, sq, 0.0), axis=-1, keepdims=True)
    hi = jnp.sum(jnp.where(first, 0.0, sq), axis=-1, keepdims=True)
    ms = jnp.where(first, lo, hi) * (1.0 / HEAD_DIM)
    return z * lax.rsqrt(ms + EPS) * gain


def _rope128(z, cos, sin_lo, sin_hi):
    q = ROPE_PAIRS
    return z * cos + pltpu.roll(z, LANES - q, 1) * sin_lo + pltpu.roll(z, q, 1) * sin_hi


def _inproj_latent_kernel(x_ref, m_ref, gpre_ref, w_ref, qg_ref, kg_ref, lbraw_ref, cos_ref, slo_ref, shi_ref,
                          q_ref, k_ref, v_ref, hq_ref, hv_ref, ff_ref, og_ref, mg_ref, *, d):
    shift = m_ref[:, 3 * d:4 * d]
    scale = m_ref[:, 4 * d:5 * d]
    for rows in _sub_tiles(x_ref.shape[0]):
        u = (_rms(x_ref[rows, :], gpre_ref[...]) * (1.0 + scale) + shift).astype(BF16)
        cos, slo, shi = cos_ref[rows, :], slo_ref[rows, :], shi_ref[rows, :]

        def proj(lo, hi):
            return _dot(u, w_ref[:, lo:hi])

        c0 = 0
        pq = proj(c0, c0 + ATT_WIDTH)
        qs = []
        for j in range(ATT_WIDTH // LANES):
            z = _head_rms64(pq[:, j * LANES:(j + 1) * LANES], qg_ref[...])
            qs.append(_rope128(z, cos, slo, shi) * (ATT_SCALE * LOG2E))
        q_ref[rows, :] = jnp.concatenate(qs, axis=-1).astype(BF16)
        c0 += ATT_WIDTH
        pkv = proj(c0, c0 + 2 * KV_WIDTH)
        k_ref[rows, :] = _rope128(_head_rms64(pkv[:, :KV_WIDTH], kg_ref[...]), cos, slo, shi).astype(BF16)
        v_ref[rows, :] = pkv[:, KV_WIDTH:].astype(BF16)
        c0 += 2 * KV_WIDTH
        hq_ref[rows, :] = _silu(proj(c0, c0 + HG_WIDTH)) * HG_SCALE
        c0 += HG_WIDTH
        hv_ref[rows, :] = proj(c0, c0 + HG_WIDTH).astype(BF16)
        c0 += HG_WIDTH
        ff_ref[rows, :] = _log2_forget(proj(c0, c0 + 2 * HG_WIDTH), lbraw_ref)
        c0 += 2 * HG_WIDTH
        og_ref[rows, :] = proj(c0, c0 + HG_WIDTH)
        c0 += HG_WIDTH
        mg_ref[rows, :] = proj(c0, c0 + 2 * d)


def _inproj_latent(x, mods, g_pre, w_in, q_gain2, k_gain2, lb_raw, cos, slo, shi, tm):
    bsz, t, d = x.shape
    nm = mods.shape[-1]
    n_in = w_in.shape[1]

    def tile(width, dtype):
        return (pl.BlockSpec((None, tm, width), lambda b, i: (b, i, 0)),
                jax.ShapeDtypeStruct((bsz, t, width), dtype))

    outs = [tile(ATT_WIDTH, BF16), tile(KV_WIDTH, BF16), tile(KV_WIDTH, BF16), tile(HG_WIDTH, F32),
            tile(HG_WIDTH, BF16), tile(2 * HG_WIDTH, F32), tile(HG_WIDTH, F32), tile(2 * d, F32)]
    rope_spec = pl.BlockSpec((tm, LANES), lambda b, i: (i, 0))
    return pl.pallas_call(
        functools.partial(_inproj_latent_kernel, d=d),
        grid=(bsz, t // tm),
        in_specs=[pl.BlockSpec((None, tm, d), lambda b, i: (b, i, 0)),
                  pl.BlockSpec((None, 1, nm), lambda b, i: (b, 0, 0)),
                  _const_spec((1, d)), _const_spec((d, n_in)),
                  _const_spec((1, LANES)), _const_spec((1, LANES)), _const_spec(lb_raw.shape),
                  rope_spec, rope_spec, rope_spec],
        out_specs=[o[0] for o in outs],
        out_shape=[o[1] for o in outs],
        compiler_params=_params(2),
        name="inproj_latent",
    )(x, mods, g_pre[None, :], w_in, q_gain2, k_gain2, lb_raw, cos, slo, shi)


def _inproj_ctx_kernel(x_ref, m_ref, gpre_ref, w_ref, kg_ref, lbraw_ref, k_ref, v_ref, hv_ref, ff_ref, *, d):
    x = x_ref[...]
    shift = m_ref[:, 3 * d:4 * d]
    scale = m_ref[:, 4 * d:5 * d]
    u = (_rms(x, gpre_ref[...]) * (1.0 + scale) + shift).astype(BF16)
    kv0 = ATT_WIDTH
    h0 = ATT_WIDTH + 2 * KV_WIDTH + HG_WIDTH
    pkv = _dot(u, w_ref[:, kv0:kv0 + 2 * KV_WIDTH])
    k_ref[...] = _head_rms64(pkv[:, :KV_WIDTH], kg_ref[...]).astype(BF16)
    v_ref[...] = pkv[:, KV_WIDTH:].astype(BF16)
    hv_ref[...] = _dot(u, w_ref[:, h0:h0 + HG_WIDTH]).astype(BF16)
    ff_ref[...] = _log2_forget(_dot(u, w_ref[:, h0 + HG_WIDTH:h0 + 3 * HG_WIDTH]), lbraw_ref)


def _inproj_ctx(x, mods, ctx_row, g_pre, w_in, k_gain2, lb_raw, tm):
    bsz, t, d = x.shape
    nm = mods.shape[-1]

    def tile(width, dtype):
        return (pl.BlockSpec((None, tm, width), lambda b, i: (b, i, 0)),
                jax.ShapeDtypeStruct((bsz, t, width), dtype))

    outs = [tile(KV_WIDTH, BF16), tile(KV_WIDTH, BF16), tile(HG_WIDTH, BF16), tile(2 * HG_WIDTH, F32)]
    return pl.pallas_call(
        functools.partial(_inproj_ctx_kernel, d=d),
        grid=(bsz, t // tm),
        in_specs=[pl.BlockSpec((None, tm, d), lambda b, i: (b, i, 0)),
                  pl.BlockSpec((None, 1, nm), lambda b, i: (ctx_row, 0, 0)),
                  _const_spec((1, d)), _const_spec(w_in.shape),
                  _const_spec((1, LANES)), _const_spec(lb_raw.shape)],
        out_specs=[o[0] for o in outs],
        out_shape=[o[1] for o in outs],
        compiler_params=_params(2),
        name="inproj_ctx",
    )(x, mods, g_pre[None, :], w_in, k_gain2, lb_raw)


def _attn_kernel(q_ref, kl_ref, kc_ref, vl_ref, vc_ref, o_ref, st_ref, pt_ref, vt_ref):
    tq = st_ref.shape[2]
    t_lat = kl_ref.shape[0]
    n_items = (q_ref.shape[0] // tq) * N_Q_HEADS

    @pl.when(pl.program_id(1) == 0)
    def _():
        extra = jnp.where(lax.broadcasted_iota(jnp.int32, (VT_ROWS - HEAD_DIM, vt_ref.shape[1]), 0) == 0,
                          1.0, 0.0).astype(BF16)
        for src, cols in ((vl_ref, slice(0, t_lat)), (vc_ref, slice(t_lat, None))):
            vt = src[...].astype(F32).T.astype(BF16)
            for kv in range(N_KV_HEADS):
                vt_ref[kv * VT_ROWS:kv * VT_ROWS + HEAD_DIM, cols] = vt[kv * HEAD_DIM:(kv + 1) * HEAD_DIM, :]
        for kv in range(N_KV_HEADS):
            vt_ref[kv * VT_ROWS + HEAD_DIM:(kv + 1) * VT_ROWS, :] = extra

    def scores(i):
        r, h = divmod(i, N_Q_HEADS)
        q = q_ref[r * tq:(r + 1) * tq, h * HEAD_DIM:(h + 1) * HEAD_DIM]
        halves = [q, jnp.zeros_like(q)] if h // GROUP == 0 else [jnp.zeros_like(q), q]
        qp = jnp.concatenate(halves, axis=1)
        s_lat = lax.dot_general(kl_ref[...], qp, NT_DIMS, preferred_element_type=F32)
        s_ctx = lax.dot_general(kc_ref[...], qp, NT_DIMS, preferred_element_type=F32)
        buf = i % st_ref.shape[0]
        st_ref[buf, :t_lat, :] = s_lat
        st_ref[buf, t_lat:, :] = s_ctx
        return jnp.maximum(jnp.max(s_lat, axis=0, keepdims=True), jnp.max(s_ctx, axis=0, keepdims=True))

    pairs = []

    def values(i):
        kv = (i % N_Q_HEADS) // GROUP
        ot = _dot(vt_ref[kv * VT_ROWS:(kv + 1) * VT_ROWS, :], pt_ref[i % 2])
        pairs.append(ot[:HEAD_DIM, :] / ot[HEAD_DIM:HEAD_DIM + 1, :])

    ahead = st_ref.shape[0] - 1
    ms = [scores(i) for i in range(ahead)]
    for i in range(n_items):
        if i + ahead < n_items:
            ms.append(scores(i + ahead))
        pt_ref[i % 2] = jnp.exp2((st_ref[i % (ahead + 1)] - ms[i]).astype(BF16))
        if i:
            values(i - 1)
    values(n_items - 1)
    for r in range(n_items // N_Q_HEADS):
        heads = pairs[r * N_Q_HEADS:(r + 1) * N_Q_HEADS]
        outs = [jnp.concatenate(heads[j:j + 2], axis=0).T for j in range(0, N_Q_HEADS, 2)]
        o_ref[r * tq:(r + 1) * tq, :] = jnp.concatenate(outs, axis=-1).astype(BF16)


def _attention(q, k_lat, k_ctx, v_lat, v_ctx, tq, tstep):
    bsz, t, _ = q.shape
    tc = k_ctx.shape[1]
    s = t + tc

    def whole(tt):
        return pl.BlockSpec((None, tt, KV_WIDTH), lambda b, i: (b, 0, 0))

    return pl.pallas_call(
        _attn_kernel,
        grid=(bsz, t // tstep),
        in_specs=[pl.BlockSpec((None, tstep, ATT_WIDTH), lambda b, i: (b, i, 0)),
                  whole(t), whole(tc), whole(t), whole(tc)],
        out_specs=pl.BlockSpec((None, tstep, ATT_WIDTH), lambda b, i: (b, i, 0)),
        out_shape=jax.ShapeDtypeStruct((bsz, t, ATT_WIDTH), BF16),
        scratch_shapes=[pltpu.VMEM((3, s, tq), F32),
                        pltpu.VMEM((2, s, tq), BF16),
                        pltpu.VMEM((N_KV_HEADS * VT_ROWS, s), BF16)],
        compiler_params=pltpu.CompilerParams(dimension_semantics=("parallel", "arbitrary"),
                                             vmem_limit_bytes=VMEM_LIMIT),
        name="attn",
    )(q, k_lat, k_ctx, v_lat, v_ctx)


def _split3(g):
    g1 = g.astype(BF16)
    r1 = g - g1.astype(F32)
    g2 = r1.astype(BF16)
    g3 = (r1 - g2.astype(F32)).astype(BF16)
    return g1, g2, g3


def _hgrn_bidir_kernel(gain_ref, hq_ref, hv_ref, ff_ref, fb_ref, og_ref, cv_ref, cff_ref, cfb_ref,
                       o_ref, acc_ref, qe_ref, ds_ref, dec_ref, st_ref, bk_ref, ke_ref, a_ref,
                       *, n_lat, n_ctx, cpb):
    c = CHUNK
    r = cpb * c
    dk = HG_DK
    row = lax.broadcasted_iota(jnp.int32, (r, r), 0)
    col = lax.broadcasted_iota(jnp.int32, (r, r), 1)
    same_chunk = (row // c) == (col // c)
    masks = (same_chunk & (col <= row), same_chunk & (col >= row))
    tris = tuple(jnp.where(m, 1.0, 0.0).astype(BF16) for m in masks)
    last = (c - 1, 0)
    mid = (c // 2 - 1, c // 2)

    def per_chunk_rows(x, off):
        return jnp.concatenate([jnp.broadcast_to(x[j * c + off:j * c + off + 1, :], (c, x.shape[1]))
                                for j in range(cpb)], axis=0)

    def lane_block(d, j):
        return slice((d * cpb + j) * dk, (d * cpb + j + 1) * dk)

    ke_ref[...] = jnp.zeros(ke_ref.shape, BF16)

    blocks = ([((cff_ref, cfb_ref), cv_ref, None, i * r, i * cpb) for i in range(n_ctx // cpb)]
              + [((ff_ref, fb_ref), hv_ref, hq_ref, i * r, n_ctx + i * cpb) for i in range(n_lat // cpb)])

    def decays(n):
        f_refs, _, _, r0, _ = blocks[n]
        for d in range(2):
            lf = f_refs[d][r0:r0 + r, :]
            g1, g2, g3 = _split3(lf)
            bb = _dot(tris[d], jnp.concatenate([g1, g2, g3], axis=1))
            bk_ref[n % 2, d] = bb[:, :dk] + bb[:, dk:2 * dk] + bb[:, 2 * dk:]
            bk_ref[n % 2, 2 + d] = 1.0 - jnp.exp2(lf)

    def scores(n):
        _, _, q_ref, r0, ch0 = blocks[n]
        q = None if q_ref is None else q_ref[r0:r0 + r, :]
        amat = None
        for d in range(2):
            b, k = bk_ref[n % 2, d], bk_ref[n % 2, 2 + d]
            b_last = per_chunk_rows(b, last[d])
            ke = (k * jnp.exp2(b_last - b)).astype(BF16)
            for j in range(cpb):
                ke_ref[n % 2, j * c:(j + 1) * c, lane_block(d, j)] = ke[j * c:(j + 1) * c, :]
                dec_ref[d, ch0 + j] = jnp.exp2(b[j * c + last[d]:j * c + last[d] + 1, :])
            if q is None:
                continue
            b_mid = per_chunk_rows(b, mid[d])
            qd = (q * jnp.exp2(b - b_mid)).astype(BF16)
            kd = (k * jnp.exp2(b_mid - b)).astype(BF16)
            a = jnp.where(masks[d], lax.dot_general(qd, kd, NT_DIMS, preferred_element_type=F32), 0.0)
            amat = a if amat is None else amat + a
            qe_ref[r0:r0 + r, d * dk:(d + 1) * dk] = (q * jnp.exp2(b)).astype(BF16)
        if q is not None:
            a_ref[n % 2] = amat.astype(BF16)

    def products(n):
        _, v_ref, q_ref, r0, ch0 = blocks[n]
        v = v_ref[r0:r0 + r, :]
        ds = lax.dot_general(v, ke_ref[n % 2], TN_DIMS, preferred_element_type=F32)
        for d in range(2):
            for j in range(cpb):
                lo = (d * cpb + j) * dk
                ds_ref[d, ch0 + j] = ds[:, lo:lo + dk]
        if q_ref is not None:
            acc_ref[r0:r0 + r, :] = _dot(a_ref[n % 2], v)

    decays(0)
    for n in range(len(blocks)):
        if n + 1 < len(blocks):
            decays(n + 1)
        scores(n)
        if n:
            products(n - 1)
    products(len(blocks) - 1)

    for d in range(2):
        def ctx_step(i, st):
            ch = (n_ctx - 1 - i) if d else i
            return st * dec_ref[d, ch] + ds_ref[d, ch]

        def lat_steps(i, st):
            blk = (n_lat // cpb - 1 - i) if d else i
            for jj in range(cpb):
                j = (cpb - 1 - jj) if d else jj
                st_ref[blk * cpb + j, :, d * dk:(d + 1) * dk] = st.astype(BF16)
                ch = n_ctx + blk * cpb + j
                st = st * dec_ref[d, ch] + ds_ref[d, ch]
            return st

        st = lax.fori_loop(0, n_ctx, ctx_step, jnp.zeros((HG_DV, dk), F32), unroll=True)
        lax.fori_loop(0, n_lat // cpb, lat_steps, st)

    def pass3(i, carry):
        for j in range(cpb):
            rows = pl.ds(pl.multiple_of(i * r + j * c, c), c)
            inter = lax.dot_general(qe_ref[rows, :], st_ref[i * cpb + j], NT_DIMS, preferred_element_type=F32)
            tot = acc_ref[rows, :] + inter
            o_ref[rows, :] = (_rms(tot, gain_ref[...]) * _silu(og_ref[rows, :])).astype(BF16)
        return carry

    lax.fori_loop(0, n_lat // cpb, pass3, 0, unroll=4)


def _hgrn(hg_gain, hq, hv, ff, og, cv, cff):
    bsz, t, _ = hq.shape
    tc = cv.shape[1]

    def col(tt, off=0):
        return pl.BlockSpec((None, tt, HG_DK), lambda b, h: (b, 0, h + off))

    n_lat, n_ctx = t // CHUNK, tc // CHUNK
    cpb = next(n for n in (4, 2, 1) if n_lat % n == 0 and n_ctx % n == 0)
    return pl.pallas_call(
        functools.partial(_hgrn_bidir_kernel, n_lat=n_lat, n_ctx=n_ctx, cpb=cpb),
        grid=(bsz, HG_HEADS),
        in_specs=[pl.BlockSpec((1, HG_DV), lambda b, h: (0, 0)),
                  col(t), col(t), col(t), col(t, HG_HEADS), col(t),
                  col(tc), col(tc), col(tc, HG_HEADS)],
        out_specs=col(t),
        out_shape=jax.ShapeDtypeStruct((bsz, t, HG_WIDTH), BF16),
        scratch_shapes=[pltpu.VMEM((t, HG_DV), F32),
                        pltpu.VMEM((t, 2 * HG_DK), BF16),
                        pltpu.VMEM((2, n_ctx + n_lat, HG_DV, HG_DK), F32),
                        pltpu.VMEM((2, n_ctx + n_lat, 1, HG_DK), F32),
                        pltpu.VMEM((n_lat, HG_DV, 2 * HG_DK), BF16),
                        pltpu.VMEM((2, 4, cpb * CHUNK, HG_DK), F32),
                        pltpu.VMEM((2, cpb * CHUNK, 2 * cpb * HG_DK), BF16),
                        pltpu.VMEM((2, cpb * CHUNK, cpb * CHUNK), BF16)],
        compiler_params=_params(2),
        name="hgrn",
    )(hg_gain[None, :], hq, hv, ff, ff, og, cv, cff, cff)


def _merge_ffn_kernel(x_ref, m_ref, oa_ref, oh_ref, mg_ref, gpost1_ref, wa_ref, wh_ref, wo_ref,
                      gpre2_ref, gpost2_ref, wg_ref, wu_ref, wd_ref, o_ref, *, d):
    gate = m_ref[:, 5 * d:6 * d]
    for rows in _sub_tiles(x_ref.shape[0]):
        y = (_sigmoid(mg_ref[rows, :d]) * _dot(oa_ref[rows, :], wa_ref[...])
             + _sigmoid(mg_ref[rows, d:]) * _dot(oh_ref[rows, :], wh_ref[...]))
        z = _dot(y.astype(BF16), wo_ref[...])
        x2 = x_ref[rows, :] + gate * _rms(z, gpost1_ref[...])
        o_ref[rows, :] = _ffn_half_step(x2, m_ref, gpre2_ref, gpost2_ref, wg_ref, wu_ref, wd_ref, 6, d)


def _merge_ffn(x, mods, o_att, o_hg, mg, g_post1, wa, wh, wo, g_pre2, g_post2, wg, wu, wd, tm):
    bsz, t, d = x.shape
    nm = mods.shape[-1]

    def tile(width):
        return pl.BlockSpec((None, tm, width), lambda b, i: (b, i, 0))

    return pl.pallas_call(
        functools.partial(_merge_ffn_kernel, d=d),
        grid=(bsz, t // tm),
        in_specs=[tile(d), pl.BlockSpec((None, 1, nm), lambda b, i: (b, 0, 0)),
                  tile(ATT_WIDTH), tile(HG_WIDTH), tile(2 * d), _const_spec((1, d)),
                  _const_spec(wa.shape), _const_spec(wh.shape), _const_spec(wo.shape),
                  _const_spec((1, d)), _const_spec((1, d)),
                  _const_spec(wg.shape), _const_spec(wu.shape), _const_spec(wd.shape)],
        out_specs=tile(d),
        out_shape=jax.ShapeDtypeStruct((bsz, t, d), F32),
        compiler_params=_params(2),
        name="merge_ffn",
    )(x, mods, o_att, o_hg, mg, g_post1[None, :], wa, wh, wo, g_pre2[None, :], g_post2[None, :], wg, wu, wd)


def _rope_tables(t):
    pos = jnp.arange(t, dtype=jnp.int32)
    row = (pos // GRID_W).astype(F32)
    colp = (pos % GRID_W).astype(F32)
    inv_freq = ROPE_THETA ** (-jnp.arange(ROPE_PAIRS, dtype=F32) / ROPE_PAIRS)
    ang_r = row[:, None] * inv_freq
    ang_c = colp[:, None] * inv_freq
    ang = jnp.concatenate([ang_r, ang_r, ang_c, ang_c], axis=-1)
    cos, sin = jnp.cos(ang), jnp.sin(ang)
    first = (jnp.arange(HEAD_DIM) % (2 * ROPE_PAIRS)) < ROPE_PAIRS
    sin_lo = jnp.where(first, -sin, 0.0)
    sin_hi = jnp.where(first, 0.0, sin)
    two = lambda a: jnp.concatenate([a, a], axis=-1)
    return two(cos), two(sin_lo), two(sin_hi)


def kernel(x, c, ctx, c_ctx, w_mod, b_mod, norm_pre, norm_post, ffn_w_gate, ffn_w_up, ffn_w_down,
           w_in, q_norm, k_norm, hg_lower_bound, hg_norm, w_att_out, w_hg_out, w_o):
    assert w_in.shape[0] == 1, "single-layer block"
    bsz, t, d = x.shape
    tc = ctx.shape[1]
    assert t % GRID_W == 0 and t % CHUNK == 0 and tc % CHUNK == 0
    tm = min(256, t)
    tmc = min(256, tc)
    tmf = 512 if t % 512 == 0 else tm

    rows = -(-(bsz + 1) // 8) * 8
    cvec = jnp.concatenate([c, c_ctx[None, :], jnp.zeros((rows - bsz - 1, d), c.dtype)], axis=0)
    mods = _modulation(cvec, w_mod[0], b_mod[0])[:, None, :]
    lat_row = lambda b: b
    ctx_row = lambda b: bsz

    wg, wu, wd = ([w[0, s].astype(BF16) for s in range(2)] for w in (ffn_w_gate, ffn_w_up, ffn_w_down))
    w_in_b = w_in[0].astype(BF16)

    tml = 1024 if t % 1024 == 0 else tmf
    x1 = _ffn(x, mods, lat_row, 0, norm_pre[0, 0], norm_post[0, 0], wg[0], wu[0], wd[0], tml)
    h1 = _ffn(ctx, mods, ctx_row, 0, norm_pre[0, 0], norm_post[0, 0], wg[0], wu[0], wd[0], tmc)

    cos, slo, shi = _rope_tables(t)
    q_gain2 = jnp.concatenate([q_norm[0], q_norm[0]])[None, :]
    k_gain2 = jnp.concatenate([k_norm[0], k_norm[0]])[None, :]
    slots = hg_lower_bound.shape[1]
    lb_raw = jnp.transpose(hg_lower_bound.astype(F32), (1, 0, 2)).reshape(slots, 2 * HG_WIDTH)
    q, k, v, hq, hv, ff, og, mg = _inproj_latent(x1, mods, norm_pre[0, 1], w_in_b, q_gain2, k_gain2, lb_raw,
                                                  cos, slo, shi, tmf)
    ck, cv, chv, cff = _inproj_ctx(h1, mods, bsz, norm_pre[0, 1], w_in_b, k_gain2, lb_raw, tmc)

    o_att = _attention(q, k, ck, v, cv, tm, tml)

    o_hg = _hgrn(hg_norm[0], hq, hv, ff, og, chv, cff)

    return _merge_ffn(x1, mods, o_att, o_hg, mg, norm_post[0, 1], w_att_out[0].astype(BF16),
                      w_hg_out[0].astype(BF16), w_o[0].astype(BF16),
                      norm_pre[0, 2], norm_post[0, 2], wg[1], wu[1], wd[1], tmf)
```

```python
import functools

import jax
import jax.numpy as jnp
from jax import lax
from jax.experimental import pallas as pl
from jax.experimental.pallas import tpu as pltpu

EPS = 1e-6
GRID_W = 64
ROPE_THETA = 10000.0
HEAD_DIM = 64
N_Q_HEADS = 8
N_KV_HEADS = 2
GROUP = N_Q_HEADS // N_KV_HEADS
ATT_WIDTH = N_Q_HEADS * HEAD_DIM
KV_WIDTH = N_KV_HEADS * HEAD_DIM
ROPE_PAIRS = HEAD_DIM // 4
ATT_SCALE = HEAD_DIM ** -0.5
LOG2E = 1.4426950408889634
HG_HEADS = 4
HG_DK = 128
HG_DV = 128
HG_WIDTH = HG_HEADS * HG_DK
HG_SCALE = HG_DK ** -0.5
CHUNK = 64
LANES = 128
SUB_TILE = 256
VMEM_LIMIT = 56 * 1024 * 1024
STAGE_ROWS_WIDE = 128
STAGE_ROWS_TALL = 256

BF16 = jnp.bfloat16
F32 = jnp.float32

NT_DIMS = (((1,), (1,)), ((), ()))
TN_DIMS = (((0,), (0,)), ((), ()))


def _dot(a, b):
    return jnp.dot(a, b, preferred_element_type=F32)


def _rms(x, gain):
    return x * lax.rsqrt(jnp.mean(x * x, axis=-1, keepdims=True) + EPS) * gain


def _sigmoid(x):
    return 1.0 / (1.0 + jnp.exp(-x))


def _silu(x):
    return x * _sigmoid(x)


def _params(n_grid):
    return pltpu.CompilerParams(dimension_semantics=("parallel",) * n_grid, vmem_limit_bytes=VMEM_LIMIT)


def _ordered_params(n_grid):
    return pltpu.CompilerParams(dimension_semantics=("arbitrary",) * n_grid, vmem_limit_bytes=VMEM_LIMIT)


def _const_spec(shape):
    nd = len(shape)
    return pl.BlockSpec(shape, lambda *_: (0,) * nd, pipeline_mode=pl.Buffered(1))


def _mod_kernel(c_ref, w_ref, b_ref, o_ref):
    a = _silu(c_ref[...]).astype(BF16)
    o_ref[...] = _dot(a, w_ref[...].astype(BF16)) + b_ref[...]


def _modulation(cvec, w_mod, b_mod, tn=1024):
    rows, d = cvec.shape
    n = w_mod.shape[1]
    return pl.pallas_call(
        _mod_kernel,
        grid=(n // tn,),
        in_specs=[pl.BlockSpec((rows, d), lambda j: (0, 0)),
                  pl.BlockSpec((d, tn), lambda j: (0, j)),
                  pl.BlockSpec((1, tn), lambda j: (0, j))],
        out_specs=pl.BlockSpec((rows, tn), lambda j: (0, j)),
        out_shape=jax.ShapeDtypeStruct((rows, n), F32),
        compiler_params=_params(1),
        name="mod",
    )(cvec, w_mod, b_mod[None, :])


def _ffn_half_step(x, m_ref, gpre_ref, gpost_ref, wg_ref, wu_ref, wd_ref, mod0, d):
    shift = m_ref[:, (mod0 + 0) * d:(mod0 + 1) * d]
    scale = m_ref[:, (mod0 + 1) * d:(mod0 + 2) * d]
    gate = m_ref[:, (mod0 + 2) * d:(mod0 + 3) * d]
    u = (_rms(x, gpre_ref[...]) * (1.0 + scale) + shift).astype(BF16)
    h = (_silu(_dot(u, wg_ref[...])) * _dot(u, wu_ref[...])).astype(BF16)
    y = _dot(h, wd_ref[...])
    return x + 0.5 * (gate * _rms(y, gpost_ref[...]))


def _sub_tiles(rows):
    sub = SUB_TILE if rows % SUB_TILE == 0 else rows
    return [slice(r, r + sub) for r in range(0, rows, sub)]


def _weight_copy(src, stage, sems, i, rows):
    return pltpu.make_async_copy(src.at[pl.ds(i * rows, rows), :], stage.at[i % 2], sems.at[i % 2])


def _load_weight_bf16(src, dst, stage, sems):
    rows = stage.shape[1]
    n = src.shape[0] // rows
    _weight_copy(src, stage, sems, 0, rows).start()
    for i in range(n):
        if i + 1 < n:
            _weight_copy(src, stage, sems, i + 1, rows).start()
        _weight_copy(src, stage, sems, i, rows).wait()
        dst[i * rows:(i + 1) * rows, :] = stage[i % 2].astype(BF16)


def _load_ffn_weights(w_hbm, w_vmem, stage_wide, stage_tall, sems, slot):
    @pl.when((pl.program_id(0) == 0) & (pl.program_id(1) == 0))
    def _():
        for src, dst in zip(w_hbm, w_vmem):
            stage = stage_wide if src.shape[-1] == stage_wide.shape[-1] else stage_tall
            _load_weight_bf16(src.at[slot], dst, stage, sems)


def _ffn_weight_scratch(d, f):
    return [pltpu.VMEM((d, f), BF16), pltpu.VMEM((d, f), BF16), pltpu.VMEM((f, d), BF16),
            pltpu.VMEM((2, STAGE_ROWS_WIDE, f), F32), pltpu.VMEM((2, STAGE_ROWS_TALL, d), F32),
            pltpu.SemaphoreType.DMA((2,))]


def _ffn_kernel(x_ref, m_ref, gpre_ref, gpost_ref, wg_hbm, wu_hbm, wd_hbm, o_ref,
                wg_ref, wu_ref, wd_ref, stage_wide, stage_tall, sems, *, mod0, slot, d):
    _load_ffn_weights((wg_hbm, wu_hbm, wd_hbm), (wg_ref, wu_ref, wd_ref), stage_wide, stage_tall, sems, slot)
    for rows in _sub_tiles(x_ref.shape[0]):
        o_ref[rows, :] = _ffn_half_step(x_ref[rows, :], m_ref, gpre_ref, gpost_ref, wg_ref, wu_ref, wd_ref,
                                        mod0, d)


def _ffn(x, mods, mod_row, mod0, g_pre, g_post, wg, wu, wd, slot, tm):
    bsz, t, d = x.shape
    f = wg.shape[-1]
    nm = mods.shape[-1]
    hbm = pl.BlockSpec(memory_space=pl.ANY)
    return pl.pallas_call(
        functools.partial(_ffn_kernel, mod0=mod0, slot=slot, d=d),
        grid=(bsz, t // tm),
        in_specs=[pl.BlockSpec((None, tm, d), lambda b, i: (b, i, 0)),
                  pl.BlockSpec((None, 1, nm), lambda b, i: (mod_row(b), 0, 0)),
                  _const_spec((1, d)), _const_spec((1, d)), hbm, hbm, hbm],
        out_specs=pl.BlockSpec((None, tm, d), lambda b, i: (b, i, 0)),
        out_shape=jax.ShapeDtypeStruct((bsz, t, d), F32),
        scratch_shapes=_ffn_weight_scratch(d, f),
        compiler_params=_ordered_params(2),
        name="ffn",
    )(x, mods, g_pre[None, :], g_post[None, :], wg, wu, wd)


def _log2_forget(raw, lbraw_ref):
    slots = [lbraw_ref[s:s + 1, :] for s in range(lbraw_ref.shape[0])]
    top = functools.reduce(jnp.maximum, slots)
    e = [jnp.exp(s - top) for s in slots]
    lb = e[0] / functools.reduce(jnp.add, e)
    return jnp.log2(lb + (1.0 - lb) * _sigmoid(raw))


def _head_rms64(z, gain):
    lane = lax.broadcasted_iota(jnp.int32, (1, LANES), 1)
    first = lane < HEAD_DIM
    sq = z * z
    lo = jnp.sum(jnp.where(first, sq, 0.0), axis=-1, keepdims=True)
    hi = jnp.sum(jnp.where(first, 0.0, sq), axis=-1, keepdims=True)
    ms = jnp.where(first, lo, hi) * (1.0 / HEAD_DIM)
    return z * lax.rsqrt(ms + EPS) * gain


def _rope128(z, cos, sin_lo, sin_hi):
    q = ROPE_PAIRS
    return z * cos + pltpu.roll(z, LANES - q, 1) * sin_lo + pltpu.roll(z, q, 1) * sin_hi


def _inproj_latent_kernel(x_ref, m_ref, gpre_ref, w_ref, qg_ref, kg_ref, lbraw_ref, cos_ref, slo_ref, shi_ref,
                          q_ref, k_ref, v_ref, hq_ref, hv_ref, ff_ref, og_ref, mg_ref, *, d):
    shift = m_ref[:, 3 * d:4 * d]
    scale = m_ref[:, 4 * d:5 * d]
    for rows in _sub_tiles(x_ref.shape[0]):
        u = (_rms(x_ref[rows, :], gpre_ref[...]) * (1.0 + scale) + shift).astype(BF16)
        cos, slo, shi = cos_ref[rows, :], slo_ref[rows, :], shi_ref[rows, :]

        def proj(lo, hi):
            return _dot(u, w_ref[:, lo:hi])

        c0 = 0
        pq = proj(c0, c0 + ATT_WIDTH)
        qs = []
        for j in range(ATT_WIDTH // LANES):
            z = _head_rms64(pq[:, j * LANES:(j + 1) * LANES], qg_ref[...])
            qs.append(_rope128(z, cos, slo, shi) * (ATT_SCALE * LOG2E))
        q_ref[rows, :] = jnp.concatenate(qs, axis=-1).astype(BF16)
        c0 += ATT_WIDTH
        pkv = proj(c0, c0 + 2 * KV_WIDTH)
        k_ref[rows, :] = _rope128(_head_rms64(pkv[:, :KV_WIDTH], kg_ref[...]), cos, slo, shi).astype(BF16)
        v_ref[rows, :] = pkv[:, KV_WIDTH:].astype(BF16)
        c0 += 2 * KV_WIDTH
        hq_ref[rows, :] = _silu(proj(c0, c0 + HG_WIDTH)) * HG_SCALE
        c0 += HG_WIDTH
        hv_ref[rows, :] = proj(c0, c0 + HG_WIDTH).astype(BF16)
        c0 += HG_WIDTH
        ff_ref[rows, :] = _log2_forget(proj(c0, c0 + 2 * HG_WIDTH), lbraw_ref)
        c0 += 2 * HG_WIDTH
        og_ref[rows, :] = proj(c0, c0 + HG_WIDTH)
        c0 += HG_WIDTH
        mg_ref[rows, :] = proj(c0, c0 + 2 * d)


def _inproj_latent(x, mods, g_pre, w_in, q_gain2, k_gain2, lb_raw, cos, slo, shi, tm):
    bsz, t, d = x.shape
    nm = mods.shape[-1]
    n_in = w_in.shape[1]

    def tile(width, dtype):
        return (pl.BlockSpec((None, tm, width), lambda b, i: (b, i, 0)),
                jax.ShapeDtypeStruct((bsz, t, width), dtype))

    outs = [tile(ATT_WIDTH, BF16), tile(KV_WIDTH, BF16), tile(KV_WIDTH, BF16), tile(HG_WIDTH, F32),
            tile(HG_WIDTH, BF16), tile(2 * HG_WIDTH, F32), tile(HG_WIDTH, F32), tile(2 * d, F32)]
    rope_spec = pl.BlockSpec((tm, LANES), lambda b, i: (i, 0))
    return pl.pallas_call(
        functools.partial(_inproj_latent_kernel, d=d),
        grid=(bsz, t // tm),
        in_specs=[pl.BlockSpec((None, tm, d), lambda b, i: (b, i, 0)),
                  pl.BlockSpec((None, 1, nm), lambda b, i: (b, 0, 0)),
                  _const_spec((1, d)), _const_spec((d, n_in)),
                  _const_spec((1, LANES)), _const_spec((1, LANES)), _const_spec(lb_raw.shape),
                  rope_spec, rope_spec, rope_spec],
        out_specs=[o[0] for o in outs],
        out_shape=[o[1] for o in outs],
        compiler_params=_params(2),
        name="inproj_latent",
    )(x, mods, g_pre[None, :], w_in, q_gain2, k_gain2, lb_raw, cos, slo, shi)


def _inproj_ctx_kernel(x_ref, m_ref, gpre_ref, w_ref, kg_ref, lbraw_ref, k_ref, v_ref, hv_ref, ff_ref, *, d):
    x = x_ref[...]
    shift = m_ref[:, 3 * d:4 * d]
    scale = m_ref[:, 4 * d:5 * d]
    u = (_rms(x, gpre_ref[...]) * (1.0 + scale) + shift).astype(BF16)
    kv0 = ATT_WIDTH
    h0 = ATT_WIDTH + 2 * KV_WIDTH + HG_WIDTH
    pkv = _dot(u, w_ref[:, kv0:kv0 + 2 * KV_WIDTH])
    k_ref[...] = _head_rms64(pkv[:, :KV_WIDTH], kg_ref[...]).astype(BF16)
    v_ref[...] = pkv[:, KV_WIDTH:].astype(BF16)
    hv_ref[...] = _dot(u, w_ref[:, h0:h0 + HG_WIDTH]).astype(BF16)
    ff_ref[...] = _log2_forget(_dot(u, w_ref[:, h0 + HG_WIDTH:h0 + 3 * HG_WIDTH]), lbraw_ref)


def _inproj_ctx(x, mods, ctx_row, g_pre, w_in, k_gain2, lb_raw, tm):
    bsz, t, d = x.shape
    nm = mods.shape[-1]

    def tile(width, dtype):
        return (pl.BlockSpec((None, tm, width), lambda b, i: (b, i, 0)),
                jax.ShapeDtypeStruct((bsz, t, width), dtype))

    outs = [tile(KV_WIDTH, BF16), tile(KV_WIDTH, BF16), tile(HG_WIDTH, BF16), tile(2 * HG_WIDTH, F32)]
    return pl.pallas_call(
        functools.partial(_inproj_ctx_kernel, d=d),
        grid=(bsz, t // tm),
        in_specs=[pl.BlockSpec((None, tm, d), lambda b, i: (b, i, 0)),
                  pl.BlockSpec((None, 1, nm), lambda b, i: (ctx_row, 0, 0)),
                  _const_spec((1, d)), _const_spec(w_in.shape),
                  _const_spec((1, LANES)), _const_spec(lb_raw.shape)],
        out_specs=[o[0] for o in outs],
        out_shape=[o[1] for o in outs],
        compiler_params=_params(2),
        name="inproj_ctx",
    )(x, mods, g_pre[None, :], w_in, k_gain2, lb_raw)


def _attn_kernel(q_ref, kl_ref, kc_ref, vl_ref, vc_ref, o_ref, st_ref, pt_ref, vt_ref):
    tq = st_ref.shape[2]
    t_lat = kl_ref.shape[0]
    n_items = (q_ref.shape[0] // tq) * N_Q_HEADS

    @pl.when(pl.program_id(1) == 0)
    def _():
        vt_ref[:, :t_lat] = vl_ref[...].astype(F32).T.astype(BF16)
        vt_ref[:, t_lat:] = vc_ref[...].astype(F32).T.astype(BF16)

    def scores(i):
        r, h = divmod(i, N_Q_HEADS)
        q = q_ref[r * tq:(r + 1) * tq, h * HEAD_DIM:(h + 1) * HEAD_DIM]
        halves = [q, jnp.zeros_like(q)] if h // GROUP == 0 else [jnp.zeros_like(q), q]
        qp = jnp.concatenate(halves, axis=1)
        s_lat = lax.dot_general(kl_ref[...], qp, NT_DIMS, preferred_element_type=F32)
        s_ctx = lax.dot_general(kc_ref[...], qp, NT_DIMS, preferred_element_type=F32)
        buf = i % st_ref.shape[0]
        st_ref[buf, :t_lat, :] = s_lat
        st_ref[buf, t_lat:, :] = s_ctx
        return jnp.maximum(jnp.max(s_lat, axis=0, keepdims=True), jnp.max(s_ctx, axis=0, keepdims=True))

    pairs, sums = [], []

    def values(i):
        kv = (i % N_Q_HEADS) // GROUP
        ot = _dot(vt_ref[kv * HEAD_DIM:(kv + 1) * HEAD_DIM, :], pt_ref[i % 2])
        pairs.append(ot / sums[i])

    ahead = st_ref.shape[0] - 1
    ms = [scores(i) for i in range(ahead)]
    for i in range(n_items):
        if i + ahead < n_items:
            ms.append(scores(i + ahead))
        p = jnp.exp2(st_ref[i % (ahead + 1)] - ms[i])
        sums.append(jnp.sum(p, axis=0, keepdims=True))
        pt_ref[i % 2] = p.astype(BF16)
        if i:
            values(i - 1)
    values(n_items - 1)
    for r in range(n_items // N_Q_HEADS):
        heads = pairs[r * N_Q_HEADS:(r + 1) * N_Q_HEADS]
        outs = [jnp.concatenate(heads[j:j + 2], axis=0).T for j in range(0, N_Q_HEADS, 2)]
        o_ref[r * tq:(r + 1) * tq, :] = jnp.concatenate(outs, axis=-1).astype(BF16)


def _attention(q, k_lat, k_ctx, v_lat, v_ctx, tq, tstep):
    bsz, t, _ = q.shape
    tc = k_ctx.shape[1]
    s = t + tc

    def whole(tt):
        return pl.BlockSpec((None, tt, KV_WIDTH), lambda b, i: (b, 0, 0))

    return pl.pallas_call(
        _attn_kernel,
        grid=(bsz, t // tstep),
        in_specs=[pl.BlockSpec((None, tstep, ATT_WIDTH), lambda b, i: (b, i, 0)),
                  whole(t), whole(tc), whole(t), whole(tc)],
        out_specs=pl.BlockSpec((None, tstep, ATT_WIDTH), lambda b, i: (b, i, 0)),
        out_shape=jax.ShapeDtypeStruct((bsz, t, ATT_WIDTH), BF16),
        scratch_shapes=[pltpu.VMEM((3, s, tq), F32),
                        pltpu.VMEM((2, s, tq), BF16),
                        pltpu.VMEM((KV_WIDTH, s), BF16)],
        compiler_params=pltpu.CompilerParams(dimension_semantics=("parallel", "arbitrary"),
                                             vmem_limit_bytes=VMEM_LIMIT),
        name="attn",
    )(q, k_lat, k_ctx, v_lat, v_ctx)


def _split3(g):
    g1 = g.astype(BF16)
    r1 = g - g1.astype(F32)
    g2 = r1.astype(BF16)
    g3 = (r1 - g2.astype(F32)).astype(BF16)
    return g1, g2, g3


def _hgrn_bidir_kernel(gain_ref, hq_ref, hv_ref, ff_ref, fb_ref, og_ref, cv_ref, cff_ref, cfb_ref,
                       o_ref, acc_ref, qe_ref, ds_ref, dec_ref, st_ref, bk_ref, ke_ref, a_ref,
                       *, n_lat, n_ctx, cpb):
    c = CHUNK
    r = cpb * c
    dk = HG_DK
    row = lax.broadcasted_iota(jnp.int32, (r, r), 0)
    col = lax.broadcasted_iota(jnp.int32, (r, r), 1)
    same_chunk = (row // c) == (col // c)
    masks = (same_chunk & (col <= row), same_chunk & (col >= row))
    tril = jnp.where(masks[0], 1.0, 0.0).astype(BF16)
    last = (c - 1, 0)
    mid = (c // 2 - 1, c // 2)

    def per_chunk_rows(x, off):
        return jnp.concatenate([jnp.broadcast_to(x[j * c + off:j * c + off + 1, :], (c, x.shape[1]))
                                for j in range(cpb)], axis=0)

    def lane_block(d, j):
        return slice((d * cpb + j) * dk, (d * cpb + j + 1) * dk)

    ke_ref[...] = jnp.zeros(ke_ref.shape, BF16)

    blocks = ([((cff_ref, cfb_ref), cv_ref, None, i * r, i * cpb) for i in range(n_ctx // cpb)]
              + [((ff_ref, fb_ref), hv_ref, hq_ref, i * r, n_ctx + i * cpb) for i in range(n_lat // cpb)])

    def decays(n):
        f_refs, _, _, r0, _ = blocks[n]
        lfs = [f_refs[d][r0:r0 + r, :] for d in range(2)]
        bb = _dot(tril, jnp.concatenate([g for lf in lfs for g in _split3(lf)], axis=1))
        prefix = [bb[:, (3 * d) * dk:(3 * d + 1) * dk] + bb[:, (3 * d + 1) * dk:(3 * d + 2) * dk]
                  + bb[:, (3 * d + 2) * dk:(3 * d + 3) * dk] for d in range(2)]
        bk_ref[n % 2, 0] = prefix[0]
        bk_ref[n % 2, 1] = per_chunk_rows(prefix[1], c - 1) - prefix[1] + lfs[1]
        for d in range(2):
            bk_ref[n % 2, 2 + d] = 1.0 - jnp.exp2(lfs[d])

    def scores(n):
        _, _, q_ref, r0, ch0 = blocks[n]
        q = None if q_ref is None else q_ref[r0:r0 + r, :]
        amat = None
        for d in range(2):
            b, k = bk_ref[n % 2, d], bk_ref[n % 2, 2 + d]
            b_last = per_chunk_rows(b, last[d])
            ke = (k * jnp.exp2(b_last - b)).astype(BF16)
            for j in range(cpb):
                ke_ref[n % 2, j * c:(j + 1) * c, lane_block(d, j)] = ke[j * c:(j + 1) * c, :]
                dec_ref[d, ch0 + j] = jnp.exp2(b[j * c + last[d]:j * c + last[d] + 1, :])
            if q is None:
                continue
            b_mid = per_chunk_rows(b, mid[d])
            qd = (q * jnp.exp2(b - b_mid)).astype(BF16)
            kd = (k * jnp.exp2(b_mid - b)).astype(BF16)
            a = jnp.where(masks[d], lax.dot_general(qd, kd, NT_DIMS, preferred_element_type=F32), 0.0)
            amat = a if amat is None else amat + a
            qe_ref[r0:r0 + r, d * dk:(d + 1) * dk] = (q * jnp.exp2(b)).astype(BF16)
        if q is not None:
            a_ref[n % 2] = amat.astype(BF16)

    def products(n):
        _, v_ref, q_ref, r0, ch0 = blocks[n]
        v = v_ref[r0:r0 + r, :]
        ds = lax.dot_general(v, ke_ref[n % 2], TN_DIMS, preferred_element_type=F32)
        for d in range(2):
            for j in range(cpb):
                lo = (d * cpb + j) * dk
                ds_ref[d, ch0 + j] = ds[:, lo:lo + dk]
        if q_ref is not None:
            acc_ref[r0:r0 + r, :] = _dot(a_ref[n % 2], v)

    decays(0)
    for n in range(len(blocks)):
        if n + 1 < len(blocks):
            decays(n + 1)
        scores(n)
        if n:
            products(n - 1)
    products(len(blocks) - 1)

    for d in range(2):
        def ctx_step(i, st):
            ch = (n_ctx - 1 - i) if d else i
            return st * dec_ref[d, ch] + ds_ref[d, ch]

        def lat_steps(i, st):
            blk = (n_lat // cpb - 1 - i) if d else i
            for jj in range(cpb):
                j = (cpb - 1 - jj) if d else jj
                st_ref[blk * cpb + j, :, d * dk:(d + 1) * dk] = st.astype(BF16)
                ch = n_ctx + blk * cpb + j
                st = st * dec_ref[d, ch] + ds_ref[d, ch]
            return st

        st = lax.fori_loop(0, n_ctx, ctx_step, jnp.zeros((HG_DV, dk), F32), unroll=True)
        lax.fori_loop(0, n_lat // cpb, lat_steps, st)

    def pass3(i, carry):
        for j in range(cpb):
            rows = pl.ds(pl.multiple_of(i * r + j * c, c), c)
            inter = lax.dot_general(qe_ref[rows, :], st_ref[i * cpb + j], NT_DIMS, preferred_element_type=F32)
            tot = acc_ref[rows, :] + inter
            o_ref[rows, :] = (_rms(tot, gain_ref[...]) * _silu(og_ref[rows, :])).astype(BF16)
        return carry

    lax.fori_loop(0, n_lat // cpb, pass3, 0, unroll=4)


def _hgrn(hg_gain, hq, hv, ff, og, cv, cff):
    bsz, t, _ = hq.shape
    tc = cv.shape[1]

    def col(tt, off=0):
        return pl.BlockSpec((None, tt, HG_DK), lambda b, h: (b, 0, h + off))

    n_lat, n_ctx = t // CHUNK, tc // CHUNK
    cpb = next(n for n in (4, 2, 1) if n_lat % n == 0 and n_ctx % n == 0)
    return pl.pallas_call(
        functools.partial(_hgrn_bidir_kernel, n_lat=n_lat, n_ctx=n_ctx, cpb=cpb),
        grid=(bsz, HG_HEADS),
        in_specs=[pl.BlockSpec((1, HG_DV), lambda b, h: (0, 0)),
                  col(t), col(t), col(t), col(t, HG_HEADS), col(t),
                  col(tc), col(tc), col(tc, HG_HEADS)],
        out_specs=col(t),
        out_shape=jax.ShapeDtypeStruct((bsz, t, HG_WIDTH), BF16),
        scratch_shapes=[pltpu.VMEM((t, HG_DV), F32),
                        pltpu.VMEM((t, 2 * HG_DK), BF16),
                        pltpu.VMEM((2, n_ctx + n_lat, HG_DV, HG_DK), F32),
                        pltpu.VMEM((2, n_ctx + n_lat, 1, HG_DK), F32),
                        pltpu.VMEM((n_lat, HG_DV, 2 * HG_DK), BF16),
                        pltpu.VMEM((2, 4, cpb * CHUNK, HG_DK), F32),
                        pltpu.VMEM((2, cpb * CHUNK, 2 * cpb * HG_DK), BF16),
                        pltpu.VMEM((2, cpb * CHUNK, cpb * CHUNK), BF16)],
        compiler_params=_params(2),
        name="hgrn",
    )(hg_gain[None, :], hq, hv, ff, ff, og, cv, cff, cff)


def _merge_ffn_kernel(x_ref, m_ref, oa_ref, oh_ref, mg_ref, gpost1_ref, wa_ref, wh_ref, wo_ref,
                      gpre2_ref, gpost2_ref, wg_hbm, wu_hbm, wd_hbm, o_ref,
                      wg_ref, wu_ref, wd_ref, stage_wide, stage_tall, sems, *, slot, d):
    _load_ffn_weights((wg_hbm, wu_hbm, wd_hbm), (wg_ref, wu_ref, wd_ref), stage_wide, stage_tall, sems, slot)
    gate = m_ref[:, 5 * d:6 * d]
    for rows in _sub_tiles(x_ref.shape[0]):
        y = (_sigmoid(mg_ref[rows, :d]) * _dot(oa_ref[rows, :], wa_ref[...])
             + _sigmoid(mg_ref[rows, d:]) * _dot(oh_ref[rows, :], wh_ref[...]))
        z = _dot(y.astype(BF16), wo_ref[...])
        x2 = x_ref[rows, :] + gate * _rms(z, gpost1_ref[...])
        o_ref[rows, :] = _ffn_half_step(x2, m_ref, gpre2_ref, gpost2_ref, wg_ref, wu_ref, wd_ref, 6, d)


def _merge_ffn(x, mods, o_att, o_hg, mg, g_post1, wa, wh, wo, g_pre2, g_post2, wg, wu, wd, slot, tm):
    bsz, t, d = x.shape
    nm = mods.shape[-1]
    hbm = pl.BlockSpec(memory_space=pl.ANY)

    def tile(width):
        return pl.BlockSpec((None, tm, width), lambda b, i: (b, i, 0))

    return pl.pallas_call(
        functools.partial(_merge_ffn_kernel, slot=slot, d=d),
        grid=(bsz, t // tm),
        in_specs=[tile(d), pl.BlockSpec((None, 1, nm), lambda b, i: (b, 0, 0)),
                  tile(ATT_WIDTH), tile(HG_WIDTH), tile(2 * d), _const_spec((1, d)),
                  _const_spec(wa.shape), _const_spec(wh.shape), _const_spec(wo.shape),
                  _const_spec((1, d)), _const_spec((1, d)), hbm, hbm, hbm],
        out_specs=tile(d),
        out_shape=jax.ShapeDtypeStruct((bsz, t, d), F32),
        scratch_shapes=_ffn_weight_scratch(d, wg.shape[-1]),
        compiler_params=_ordered_params(2),
        name="merge_ffn",
    )(x, mods, o_att, o_hg, mg, g_post1[None, :], wa, wh, wo, g_pre2[None, :], g_post2[None, :], wg, wu, wd)


def _rope_tables(t):
    pos = jnp.arange(t, dtype=jnp.int32)
    row = (pos // GRID_W).astype(F32)
    colp = (pos % GRID_W).astype(F32)
    inv_freq = ROPE_THETA ** (-jnp.arange(ROPE_PAIRS, dtype=F32) / ROPE_PAIRS)
    ang_r = row[:, None] * inv_freq
    ang_c = colp[:, None] * inv_freq
    ang = jnp.concatenate([ang_r, ang_r, ang_c, ang_c], axis=-1)
    cos, sin = jnp.cos(ang), jnp.sin(ang)
    first = (jnp.arange(HEAD_DIM) % (2 * ROPE_PAIRS)) < ROPE_PAIRS
    sin_lo = jnp.where(first, -sin, 0.0)
    sin_hi = jnp.where(first, 0.0, sin)
    two = lambda a: jnp.concatenate([a, a], axis=-1)
    return two(cos), two(sin_lo), two(sin_hi)


def _token_tiles(t):
    base = min(SUB_TILE, t)
    medium = 2 * SUB_TILE if t % (2 * SUB_TILE) == 0 else base
    large = 4 * SUB_TILE if t % (4 * SUB_TILE) == 0 else medium
    return base, medium, large


def kernel(x, c, ctx, c_ctx, w_mod, b_mod, norm_pre, norm_post, ffn_w_gate, ffn_w_up, ffn_w_down,
           w_in, q_norm, k_norm, hg_lower_bound, hg_norm, w_att_out, w_hg_out, w_o):
    assert w_in.shape[0] == 1, "single-layer block"
    bsz, t, d = x.shape
    tc = ctx.shape[1]
    assert t % GRID_W == 0 and t % CHUNK == 0 and tc % CHUNK == 0
    tm, tmf, tml = _token_tiles(t)
    tmc = _token_tiles(tc)[0]

    rows = -(-(bsz + 1) // 8) * 8
    cvec = jnp.concatenate([c, c_ctx[None, :], jnp.zeros((rows - bsz - 1, d), c.dtype)], axis=0)
    mods = _modulation(cvec, w_mod[0], b_mod[0])[:, None, :]
    lat_row = lambda b: b
    ctx_row = lambda b: bsz

    wg, wu, wd = ffn_w_gate[0], ffn_w_up[0], ffn_w_down[0]
    w_in_b = w_in[0].astype(BF16)

    x1 = _ffn(x, mods, lat_row, 0, norm_pre[0, 0], norm_post[0, 0], wg, wu, wd, 0, tml)
    h1 = _ffn(ctx, mods, ctx_row, 0, norm_pre[0, 0], norm_post[0, 0], wg, wu, wd, 0, tmc)

    cos, slo, shi = _rope_tables(t)
    q_gain2 = jnp.concatenate([q_norm[0], q_norm[0]])[None, :]
    k_gain2 = jnp.concatenate([k_norm[0], k_norm[0]])[None, :]
    slots = hg_lower_bound.shape[1]
    lb_raw = jnp.transpose(hg_lower_bound.astype(F32), (1, 0, 2)).reshape(slots, 2 * HG_WIDTH)
    q, k, v, hq, hv, ff, og, mg = _inproj_latent(x1, mods, norm_pre[0, 1], w_in_b, q_gain2, k_gain2, lb_raw,
                                                  cos, slo, shi, tmf)
    ck, cv, chv, cff = _inproj_ctx(h1, mods, bsz, norm_pre[0, 1], w_in_b, k_gain2, lb_raw, tmc)

    o_att = _attention(q, k, ck, v, cv, tm, tml)

    o_hg = _hgrn(hg_norm[0], hq, hv, ff, og, chv, cff)

    return _merge_ffn(x1, mods, o_att, o_hg, mg, norm_post[0, 1], w_att_out[0].astype(BF16),
                      w_hg_out[0].astype(BF16), w_o[0].astype(BF16),
                      norm_pre[0, 2], norm_post[0, 2], wg, wu, wd, 1, tmf)
```

```python
import functools

import jax
import jax.numpy as jnp
from jax import lax
from jax.experimental import pallas as pl
from jax.experimental.pallas import tpu as pltpu

EPS = 1e-6
GRID_W = 64
ROPE_THETA = 10000.0
HEAD_DIM = 64
N_Q_HEADS = 8
N_KV_HEADS = 2
GROUP = N_Q_HEADS // N_KV_HEADS
ATT_WIDTH = N_Q_HEADS * HEAD_DIM
KV_WIDTH = N_KV_HEADS * HEAD_DIM
ROPE_PAIRS = HEAD_DIM // 4
ATT_SCALE = HEAD_DIM ** -0.5
LOG2E = 1.4426950408889634
HG_HEADS = 4
HG_DK = 128
HG_DV = 128
HG_WIDTH = HG_HEADS * HG_DK
HG_SCALE = HG_DK ** -0.5
CHUNK = 64
LANES = 128
SUB_TILE = 256
VMEM_LIMIT = 56 * 1024 * 1024
STAGE_ROWS_WIDE = 128
STAGE_ROWS_TALL = 256
STAGE_SLOTS = 4

BF16 = jnp.bfloat16
F32 = jnp.float32

NT_DIMS = (((1,), (1,)), ((), ()))
TN_DIMS = (((0,), (0,)), ((), ()))


def _dot(a, b):
    return jnp.dot(a, b, preferred_element_type=F32)


def _rms(x, gain):
    return x * lax.rsqrt(jnp.mean(x * x, axis=-1, keepdims=True) + EPS) * gain


def _sigmoid(x):
    return 1.0 / (1.0 + jnp.exp(-x))


def _silu(x):
    return x * _sigmoid(x)


def _params(n_grid):
    return pltpu.CompilerParams(dimension_semantics=("parallel",) * n_grid, vmem_limit_bytes=VMEM_LIMIT)


def _ordered_params(n_grid):
    return pltpu.CompilerParams(dimension_semantics=("arbitrary",) * n_grid, vmem_limit_bytes=VMEM_LIMIT)


def _const_spec(shape):
    nd = len(shape)
    return pl.BlockSpec(shape, lambda *_: (0,) * nd, pipeline_mode=pl.Buffered(1))


def _mod_kernel(c_ref, w_ref, b_ref, o_ref):
    a = _silu(c_ref[...]).astype(BF16)
    o_ref[...] = _dot(a, w_ref[...].astype(BF16)) + b_ref[...]


def _modulation(cvec, w_mod, b_mod, tn=1024):
    rows, d = cvec.shape
    n = w_mod.shape[1]
    return pl.pallas_call(
        _mod_kernel,
        grid=(n // tn,),
        in_specs=[pl.BlockSpec((rows, d), lambda j: (0, 0)),
                  pl.BlockSpec((d, tn), lambda j: (0, j)),
                  pl.BlockSpec((1, tn), lambda j: (0, j))],
        out_specs=pl.BlockSpec((rows, tn), lambda j: (0, j)),
        out_shape=jax.ShapeDtypeStruct((rows, n), F32),
        compiler_params=_params(1),
        name="mod",
    )(cvec, w_mod, b_mod[None, :])


def _ffn_half_step(x, m_ref, gpre_ref, gpost_ref, wg_ref, wu_ref, wd_ref, mod0, d):
    shift = m_ref[:, (mod0 + 0) * d:(mod0 + 1) * d]
    scale = m_ref[:, (mod0 + 1) * d:(mod0 + 2) * d]
    gate = m_ref[:, (mod0 + 2) * d:(mod0 + 3) * d]
    u = (_rms(x, gpre_ref[...]) * (1.0 + scale) + shift).astype(BF16)
    h = (_silu(_dot(u, wg_ref[...])) * _dot(u, wu_ref[...])).astype(BF16)
    y = _dot(h, wd_ref[...])
    return x + 0.5 * (gate * _rms(y, gpost_ref[...]))


def _sub_tiles(rows):
    sub = SUB_TILE if rows % SUB_TILE == 0 else rows
    return [slice(r, r + sub) for r in range(0, rows, sub)]


def _weight_copy(src, stage, sems, i):
    slots, rows = stage.shape[0], stage.shape[1]
    return pltpu.make_async_copy(src.at[pl.ds(i * rows, rows), :], stage.at[i % slots], sems.at[i % slots])


def _load_weight_bf16(src, dst, stage, sems):
    slots, rows = stage.shape[0], stage.shape[1]
    n = src.shape[0] // rows
    for i in range(min(slots - 1, n)):
        _weight_copy(src, stage, sems, i).start()
    for i in range(n):
        if i + slots - 1 < n:
            _weight_copy(src, stage, sems, i + slots - 1).start()
        _weight_copy(src, stage, sems, i).wait()
        dst[i * rows:(i + 1) * rows, :] = stage[i % slots].astype(BF16)


def _load_ffn_weights(w_hbm, w_vmem, stage_wide, stage_tall, sems, slot):
    @pl.when((pl.program_id(0) == 0) & (pl.program_id(1) == 0))
    def _():
        for src, dst in zip(w_hbm, w_vmem):
            stage = stage_wide if src.shape[-1] == stage_wide.shape[-1] else stage_tall
            _load_weight_bf16(src.at[slot], dst, stage, sems)


def _ffn_weight_scratch(d, f):
    return [pltpu.VMEM((d, f), BF16), pltpu.VMEM((d, f), BF16), pltpu.VMEM((f, d), BF16),
            pltpu.VMEM((STAGE_SLOTS, STAGE_ROWS_WIDE, f), F32), pltpu.VMEM((STAGE_SLOTS, STAGE_ROWS_TALL, d), F32),
            pltpu.SemaphoreType.DMA((STAGE_SLOTS,))]


def _ffn_kernel(x_ref, m_ref, gpre_ref, gpost_ref, wg_hbm, wu_hbm, wd_hbm, o_ref,
                wg_ref, wu_ref, wd_ref, stage_wide, stage_tall, sems, *, mod0, slot, d):
    _load_ffn_weights((wg_hbm, wu_hbm, wd_hbm), (wg_ref, wu_ref, wd_ref), stage_wide, stage_tall, sems, slot)
    for rows in _sub_tiles(x_ref.shape[0]):
        o_ref[rows, :] = _ffn_half_step(x_ref[rows, :], m_ref, gpre_ref, gpost_ref, wg_ref, wu_ref, wd_ref,
                                        mod0, d)


def _ffn(x, mods, mod_row, mod0, g_pre, g_post, wg, wu, wd, slot, tm):
    bsz, t, d = x.shape
    f = wg.shape[-1]
    nm = mods.shape[-1]
    hbm = pl.BlockSpec(memory_space=pl.ANY)
    return pl.pallas_call(
        functools.partial(_ffn_kernel, mod0=mod0, slot=slot, d=d),
        grid=(bsz, t // tm),
        in_specs=[pl.BlockSpec((None, tm, d), lambda b, i: (b, i, 0)),
                  pl.BlockSpec((None, 1, nm), lambda b, i: (mod_row(b), 0, 0)),
                  _const_spec((1, d)), _const_spec((1, d)), hbm, hbm, hbm],
        out_specs=pl.BlockSpec((None, tm, d), lambda b, i: (b, i, 0)),
        out_shape=jax.ShapeDtypeStruct((bsz, t, d), F32),
        scratch_shapes=_ffn_weight_scratch(d, f),
        compiler_params=_ordered_params(2),
        name="ffn",
    )(x, mods, g_pre[None, :], g_post[None, :], wg, wu, wd)


def _log2_forget(raw, lbraw_ref):
    slots = [lbraw_ref[s:s + 1, :] for s in range(lbraw_ref.shape[0])]
    top = functools.reduce(jnp.maximum, slots)
    e = [jnp.exp(s - top) for s in slots]
    lb = e[0] / functools.reduce(jnp.add, e)
    return jnp.log2(lb + (1.0 - lb) * _sigmoid(raw))


def _head_rms64(z, gain):
    lane = lax.broadcasted_iota(jnp.int32, (1, LANES), 1)
    first = lane < HEAD_DIM
    sq = z * z
    lo = jnp.sum(jnp.where(first, sq, 0.0), axis=-1, keepdims=True)
    hi = jnp.sum(jnp.where(first, 0.0, sq), axis=-1, keepdims=True)
    ms = jnp.where(first, lo, hi) * (1.0 / HEAD_DIM)
    return z * lax.rsqrt(ms + EPS) * gain


def _rope128(z, cos, sin_lo, sin_hi):
    q = ROPE_PAIRS
    return z * cos + pltpu.roll(z, LANES - q, 1) * sin_lo + pltpu.roll(z, q, 1) * sin_hi


def _inproj_latent_kernel(x_ref, m_ref, gpre_ref, w_ref, qg_ref, kg_ref, lbraw_ref, cos_ref, slo_ref, shi_ref,
                          q_ref, k_ref, v_ref, hq_ref, hv_ref, ff_ref, og_ref, mg_ref, *, d):
    shift = m_ref[:, 3 * d:4 * d]
    scale = m_ref[:, 4 * d:5 * d]
    for rows in _sub_tiles(x_ref.shape[0]):
        u = (_rms(x_ref[rows, :], gpre_ref[...]) * (1.0 + scale) + shift).astype(BF16)
        cos, slo, shi = cos_ref[rows, :], slo_ref[rows, :], shi_ref[rows, :]

        def proj(lo, hi):
            return _dot(u, w_ref[:, lo:hi])

        c0 = 0
        pq = proj(c0, c0 + ATT_WIDTH)
        qs = []
        for j in range(ATT_WIDTH // LANES):
            z = _head_rms64(pq[:, j * LANES:(j + 1) * LANES], qg_ref[...])
            qs.append(_rope128(z, cos, slo, shi) * (ATT_SCALE * LOG2E))
        q_ref[rows, :] = jnp.concatenate(qs, axis=-1).astype(BF16)
        c0 += ATT_WIDTH
        pkv = proj(c0, c0 + 2 * KV_WIDTH)
        k_ref[rows, :] = _rope128(_head_rms64(pkv[:, :KV_WIDTH], kg_ref[...]), cos, slo, shi).astype(BF16)
        v_ref[rows, :] = pkv[:, KV_WIDTH:].astype(BF16)
        c0 += 2 * KV_WIDTH
        hq_ref[rows, :] = _silu(proj(c0, c0 + HG_WIDTH)) * HG_SCALE
        c0 += HG_WIDTH
        hv_ref[rows, :] = proj(c0, c0 + HG_WIDTH).astype(BF16)
        c0 += HG_WIDTH
        ff_ref[rows, :] = _log2_forget(proj(c0, c0 + 2 * HG_WIDTH), lbraw_ref)
        c0 += 2 * HG_WIDTH
        og_ref[rows, :] = proj(c0, c0 + HG_WIDTH)
        c0 += HG_WIDTH
        mg_ref[rows, :] = proj(c0, c0 + 2 * d)


def _inproj_latent(x, mods, g_pre, w_in, q_gain2, k_gain2, lb_raw, cos, slo, shi, tm):
    bsz, t, d = x.shape
    nm = mods.shape[-1]
    n_in = w_in.shape[1]

    def tile(width, dtype):
        return (pl.BlockSpec((None, tm, width), lambda b, i: (b, i, 0)),
                jax.ShapeDtypeStruct((bsz, t, width), dtype))

    outs = [tile(ATT_WIDTH, BF16), tile(KV_WIDTH, BF16), tile(KV_WIDTH, BF16), tile(HG_WIDTH, F32),
            tile(HG_WIDTH, BF16), tile(2 * HG_WIDTH, F32), tile(HG_WIDTH, F32), tile(2 * d, F32)]
    rope_spec = pl.BlockSpec((tm, LANES), lambda b, i: (i, 0))
    return pl.pallas_call(
        functools.partial(_inproj_latent_kernel, d=d),
        grid=(bsz, t // tm),
        in_specs=[pl.BlockSpec((None, tm, d), lambda b, i: (b, i, 0)),
                  pl.BlockSpec((None, 1, nm), lambda b, i: (b, 0, 0)),
                  _const_spec((1, d)), _const_spec((d, n_in)),
                  _const_spec((1, LANES)), _const_spec((1, LANES)), _const_spec(lb_raw.shape),
                  rope_spec, rope_spec, rope_spec],
        out_specs=[o[0] for o in outs],
        out_shape=[o[1] for o in outs],
        compiler_params=_params(2),
        name="inproj_latent",
    )(x, mods, g_pre[None, :], w_in, q_gain2, k_gain2, lb_raw, cos, slo, shi)


def _inproj_ctx_kernel(x_ref, m_ref, gpre_ref, w_ref, kg_ref, lbraw_ref, k_ref, v_ref, hv_ref, ff_ref, *, d):
    x = x_ref[...]
    shift = m_ref[:, 3 * d:4 * d]
    scale = m_ref[:, 4 * d:5 * d]
    u = (_rms(x, gpre_ref[...]) * (1.0 + scale) + shift).astype(BF16)
    kv0 = ATT_WIDTH
    h0 = ATT_WIDTH + 2 * KV_WIDTH + HG_WIDTH
    pkv = _dot(u, w_ref[:, kv0:kv0 + 2 * KV_WIDTH])
    k_ref[...] = _head_rms64(pkv[:, :KV_WIDTH], kg_ref[...]).astype(BF16)
    v_ref[...] = pkv[:, KV_WIDTH:].astype(BF16)
    hv_ref[...] = _dot(u, w_ref[:, h0:h0 + HG_WIDTH]).astype(BF16)
    ff_ref[...] = _log2_forget(_dot(u, w_ref[:, h0 + HG_WIDTH:h0 + 3 * HG_WIDTH]), lbraw_ref)


def _inproj_ctx(x, mods, ctx_row, g_pre, w_in, k_gain2, lb_raw, tm):
    bsz, t, d = x.shape
    nm = mods.shape[-1]

    def tile(width, dtype):
        return (pl.BlockSpec((None, tm, width), lambda b, i: (b, i, 0)),
                jax.ShapeDtypeStruct((bsz, t, width), dtype))

    outs = [tile(KV_WIDTH, BF16), tile(KV_WIDTH, BF16), tile(HG_WIDTH, BF16), tile(2 * HG_WIDTH, F32)]
    return pl.pallas_call(
        functools.partial(_inproj_ctx_kernel, d=d),
        grid=(bsz, t // tm),
        in_specs=[pl.BlockSpec((None, tm, d), lambda b, i: (b, i, 0)),
                  pl.BlockSpec((None, 1, nm), lambda b, i: (ctx_row, 0, 0)),
                  _const_spec((1, d)), _const_spec(w_in.shape),
                  _const_spec((1, LANES)), _const_spec(lb_raw.shape)],
        out_specs=[o[0] for o in outs],
        out_shape=[o[1] for o in outs],
        compiler_params=_params(2),
        name="inproj_ctx",
    )(x, mods, g_pre[None, :], w_in, k_gain2, lb_raw)


def _attn_kernel(q_ref, kl_ref, kc_ref, vl_ref, vc_ref, o_ref, st_ref, pt_ref, vt_ref):
    tq = st_ref.shape[2]
    t_lat = kl_ref.shape[0]
    n_items = (q_ref.shape[0] // tq) * N_Q_HEADS

    @pl.when(pl.program_id(1) == 0)
    def _():
        vt_ref[:, :t_lat] = vl_ref[...].astype(F32).T.astype(BF16)
        vt_ref[:, t_lat:] = vc_ref[...].astype(F32).T.astype(BF16)

    def scores(i):
        r, h = divmod(i, N_Q_HEADS)
        q = q_ref[r * tq:(r + 1) * tq, h * HEAD_DIM:(h + 1) * HEAD_DIM]
        halves = [q, jnp.zeros_like(q)] if h // GROUP == 0 else [jnp.zeros_like(q), q]
        qp = jnp.concatenate(halves, axis=1)
        s_lat = lax.dot_general(kl_ref[...], qp, NT_DIMS, preferred_element_type=F32)
        s_ctx = lax.dot_general(kc_ref[...], qp, NT_DIMS, preferred_element_type=F32)
        buf = i % st_ref.shape[0]
        st_ref[buf, :t_lat, :] = s_lat
        st_ref[buf, t_lat:, :] = s_ctx
        return jnp.maximum(jnp.max(s_lat, axis=0, keepdims=True), jnp.max(s_ctx, axis=0, keepdims=True))

    pairs, sums = [], []

    def values(i):
        kv = (i % N_Q_HEADS) // GROUP
        ot = _dot(vt_ref[kv * HEAD_DIM:(kv + 1) * HEAD_DIM, :], pt_ref[i % 2])
        pairs.append(ot / sums[i])

    ahead = st_ref.shape[0] - 1
    ms = [scores(i) for i in range(ahead)]
    for i in range(n_items):
        if i + ahead < n_items:
            ms.append(scores(i + ahead))
        p = jnp.exp2(st_ref[i % (ahead + 1)] - ms[i])
        sums.append(jnp.sum(p, axis=0, keepdims=True))
        pt_ref[i % 2] = p.astype(BF16)
        if i:
            values(i - 1)
    values(n_items - 1)
    for r in range(n_items // N_Q_HEADS):
        heads = pairs[r * N_Q_HEADS:(r + 1) * N_Q_HEADS]
        outs = [jnp.concatenate(heads[j:j + 2], axis=0).T for j in range(0, N_Q_HEADS, 2)]
        o_ref[r * tq:(r + 1) * tq, :] = jnp.concatenate(outs, axis=-1).astype(BF16)


def _attention(q, k_lat, k_ctx, v_lat, v_ctx, tq, tstep):
    bsz, t, _ = q.shape
    tc = k_ctx.shape[1]
    s = t + tc

    def whole(tt):
        return pl.BlockSpec((None, tt, KV_WIDTH), lambda b, i: (b, 0, 0))

    return pl.pallas_call(
        _attn_kernel,
        grid=(bsz, t // tstep),
        in_specs=[pl.BlockSpec((None, tstep, ATT_WIDTH), lambda b, i: (b, i, 0)),
                  whole(t), whole(tc), whole(t), whole(tc)],
        out_specs=pl.BlockSpec((None, tstep, ATT_WIDTH), lambda b, i: (b, i, 0)),
        out_shape=jax.ShapeDtypeStruct((bsz, t, ATT_WIDTH), BF16),
        scratch_shapes=[pltpu.VMEM((3, s, tq), F32),
                        pltpu.VMEM((2, s, tq), BF16),
                        pltpu.VMEM((KV_WIDTH, s), BF16)],
        compiler_params=pltpu.CompilerParams(dimension_semantics=("parallel", "arbitrary"),
                                             vmem_limit_bytes=VMEM_LIMIT),
        name="attn",
    )(q, k_lat, k_ctx, v_lat, v_ctx)


def _split3(g):
    g1 = g.astype(BF16)
    r1 = g - g1.astype(F32)
    g2 = r1.astype(BF16)
    g3 = (r1 - g2.astype(F32)).astype(BF16)
    return g1, g2, g3


def _hgrn_bidir_kernel(gain_ref, hq_ref, hv_ref, ff_ref, fb_ref, og_ref, cv_ref, cff_ref, cfb_ref,
                       o_ref, acc_ref, qe_ref, ds_ref, dec_ref, st_ref, bk_ref, ke_ref, a_ref,
                       *, n_lat, n_ctx, cpb):
    c = CHUNK
    r = cpb * c
    dk = HG_DK
    row = lax.broadcasted_iota(jnp.int32, (r, r), 0)
    col = lax.broadcasted_iota(jnp.int32, (r, r), 1)
    same_chunk = (row // c) == (col // c)
    masks = (same_chunk & (col <= row), same_chunk & (col >= row))
    tril = jnp.where(masks[0], 1.0, 0.0).astype(BF16)
    last = (c - 1, 0)
    mid = (c // 2 - 1, c // 2)

    def per_chunk_rows(x, off):
        return jnp.concatenate([jnp.broadcast_to(x[j * c + off:j * c + off + 1, :], (c, x.shape[1]))
                                for j in range(cpb)], axis=0)

    def lane_block(d, j):
        return slice((d * cpb + j) * dk, (d * cpb + j + 1) * dk)

    ke_ref[...] = jnp.zeros(ke_ref.shape, BF16)

    blocks = ([((cff_ref, cfb_ref), cv_ref, None, i * r, i * cpb) for i in range(n_ctx // cpb)]
              + [((ff_ref, fb_ref), hv_ref, hq_ref, i * r, n_ctx + i * cpb) for i in range(n_lat // cpb)])

    def decays(n):
        f_refs, _, _, r0, _ = blocks[n]
        lfs = [f_refs[d][r0:r0 + r, :] for d in range(2)]
        bb = _dot(tril, jnp.concatenate([g for lf in lfs for g in _split3(lf)], axis=1))
        prefix = [bb[:, (3 * d) * dk:(3 * d + 1) * dk] + bb[:, (3 * d + 1) * dk:(3 * d + 2) * dk]
                  + bb[:, (3 * d + 2) * dk:(3 * d + 3) * dk] for d in range(2)]
        bk_ref[n % 2, 0] = prefix[0]
        bk_ref[n % 2, 1] = per_chunk_rows(prefix[1], c - 1) - prefix[1] + lfs[1]
        for d in range(2):
            bk_ref[n % 2, 2 + d] = 1.0 - jnp.exp2(lfs[d])

    def scores(n):
        _, _, q_ref, r0, ch0 = blocks[n]
        q = None if q_ref is None else q_ref[r0:r0 + r, :]
        amat = None
        for d in range(2):
            b, k = bk_ref[n % 2, d], bk_ref[n % 2, 2 + d]
            b_last = per_chunk_rows(b, last[d])
            ke = (k * jnp.exp2(b_last - b)).astype(BF16)
            for j in range(cpb):
                ke_ref[n % 2, j * c:(j + 1) * c, lane_block(d, j)] = ke[j * c:(j + 1) * c, :]
                dec_ref[d, ch0 + j] = jnp.exp2(b[j * c + last[d]:j * c + last[d] + 1, :])
            if q is None:
                continue
            b_mid = per_chunk_rows(b, mid[d])
            qd = (q * jnp.exp2(b - b_mid)).astype(BF16)
            kd = (k * jnp.exp2(b_mid - b)).astype(BF16)
            a = jnp.where(masks[d], lax.dot_general(qd, kd, NT_DIMS, preferred_element_type=F32), 0.0)
            amat = a if amat is None else amat + a
            qe_ref[r0:r0 + r, d * dk:(d + 1) * dk] = (q * jnp.exp2(b)).astype(BF16)
        if q is not None:
            a_ref[n % 2] = amat.astype(BF16)

    def products(n):
        _, v_ref, q_ref, r0, ch0 = blocks[n]
        v = v_ref[r0:r0 + r, :]
        ds = lax.dot_general(v, ke_ref[n % 2], TN_DIMS, preferred_element_type=F32)
        for d in range(2):
            for j in range(cpb):
                lo = (d * cpb + j) * dk
                ds_ref[d, ch0 + j] = ds[:, lo:lo + dk]
        if q_ref is not None:
            acc_ref[r0:r0 + r, :] = _dot(a_ref[n % 2], v)

    decays(0)
    for n in range(len(blocks)):
        if n + 1 < len(blocks):
            decays(n + 1)
        scores(n)
        if n:
            products(n - 1)
    products(len(blocks) - 1)

    for d in range(2):
        def ctx_step(i, st):
            ch = (n_ctx - 1 - i) if d else i
            return st * dec_ref[d, ch] + ds_ref[d, ch]

        def lat_steps(i, st):
            blk = (n_lat // cpb - 1 - i) if d else i
            for jj in range(cpb):
                j = (cpb - 1 - jj) if d else jj
                st_ref[blk * cpb + j, :, d * dk:(d + 1) * dk] = st.astype(BF16)
                ch = n_ctx + blk * cpb + j
                st = st * dec_ref[d, ch] + ds_ref[d, ch]
            return st

        st = lax.fori_loop(0, n_ctx, ctx_step, jnp.zeros((HG_DV, dk), F32), unroll=True)
        lax.fori_loop(0, n_lat // cpb, lat_steps, st)

    def pass3(i, carry):
        for j in range(cpb):
            rows = pl.ds(pl.multiple_of(i * r + j * c, c), c)
            inter = lax.dot_general(qe_ref[rows, :], st_ref[i * cpb + j], NT_DIMS, preferred_element_type=F32)
            tot = acc_ref[rows, :] + inter
            o_ref[rows, :] = (_rms(tot, gain_ref[...]) * _silu(og_ref[rows, :])).astype(BF16)
        return carry

    lax.fori_loop(0, n_lat // cpb, pass3, 0, unroll=4)


def _hgrn(hg_gain, hq, hv, ff, og, cv, cff):
    bsz, t, _ = hq.shape
    tc = cv.shape[1]

    def col(tt, off=0):
        return pl.BlockSpec((None, tt, HG_DK), lambda b, h: (b, 0, h + off))

    n_lat, n_ctx = t // CHUNK, tc // CHUNK
    cpb = next(n for n in (4, 2, 1) if n_lat % n == 0 and n_ctx % n == 0)
    return pl.pallas_call(
        functools.partial(_hgrn_bidir_kernel, n_lat=n_lat, n_ctx=n_ctx, cpb=cpb),
        grid=(bsz, HG_HEADS),
        in_specs=[pl.BlockSpec((1, HG_DV), lambda b, h: (0, 0)),
                  col(t), col(t), col(t), col(t, HG_HEADS), col(t),
                  col(tc), col(tc), col(tc, HG_HEADS)],
        out_specs=col(t),
        out_shape=jax.ShapeDtypeStruct((bsz, t, HG_WIDTH), BF16),
        scratch_shapes=[pltpu.VMEM((t, HG_DV), F32),
                        pltpu.VMEM((t, 2 * HG_DK), BF16),
                        pltpu.VMEM((2, n_ctx + n_lat, HG_DV, HG_DK), F32),
                        pltpu.VMEM((2, n_ctx + n_lat, 1, HG_DK), F32),
                        pltpu.VMEM((n_lat, HG_DV, 2 * HG_DK), BF16),
                        pltpu.VMEM((2, 4, cpb * CHUNK, HG_DK), F32),
                        pltpu.VMEM((2, cpb * CHUNK, 2 * cpb * HG_DK), BF16),
                        pltpu.VMEM((2, cpb * CHUNK, cpb * CHUNK), BF16)],
        compiler_params=_params(2),
        name="hgrn",
    )(hg_gain[None, :], hq, hv, ff, ff, og, cv, cff, cff)


def _merge_ffn_kernel(x_ref, m_ref, oa_ref, oh_ref, mg_ref, gpost1_ref, wa_ref, wh_ref, wo_ref,
                      gpre2_ref, gpost2_ref, wg_hbm, wu_hbm, wd_hbm, o_ref,
                      wg_ref, wu_ref, wd_ref, stage_wide, stage_tall, sems, *, slot, d):
    _load_ffn_weights((wg_hbm, wu_hbm, wd_hbm), (wg_ref, wu_ref, wd_ref), stage_wide, stage_tall, sems, slot)
    gate = m_ref[:, 5 * d:6 * d]
    for rows in _sub_tiles(x_ref.shape[0]):
        y = (_sigmoid(mg_ref[rows, :d]) * _dot(oa_ref[rows, :], wa_ref[...])
             + _sigmoid(mg_ref[rows, d:]) * _dot(oh_ref[rows, :], wh_ref[...]))
        z = _dot(y.astype(BF16), wo_ref[...])
        x2 = x_ref[rows, :] + gate * _rms(z, gpost1_ref[...])
        o_ref[rows, :] = _ffn_half_step(x2, m_ref, gpre2_ref, gpost2_ref, wg_ref, wu_ref, wd_ref, 6, d)


def _merge_ffn(x, mods, o_att, o_hg, mg, g_post1, wa, wh, wo, g_pre2, g_post2, wg, wu, wd, slot, tm):
    bsz, t, d = x.shape
    nm = mods.shape[-1]
    hbm = pl.BlockSpec(memory_space=pl.ANY)

    def tile(width):
        return pl.BlockSpec((None, tm, width), lambda b, i: (b, i, 0))

    return pl.pallas_call(
        functools.partial(_merge_ffn_kernel, slot=slot, d=d),
        grid=(bsz, t // tm),
        in_specs=[tile(d), pl.BlockSpec((None, 1, nm), lambda b, i: (b, 0, 0)),
                  tile(ATT_WIDTH), tile(HG_WIDTH), tile(2 * d), _const_spec((1, d)),
                  _const_spec(wa.shape), _const_spec(wh.shape), _const_spec(wo.shape),
                  _const_spec((1, d)), _const_spec((1, d)), hbm, hbm, hbm],
        out_specs=tile(d),
        out_shape=jax.ShapeDtypeStruct((bsz, t, d), F32),
        scratch_shapes=_ffn_weight_scratch(d, wg.shape[-1]),
        compiler_params=_ordered_params(2),
        name="merge_ffn",
    )(x, mods, o_att, o_hg, mg, g_post1[None, :], wa, wh, wo, g_pre2[None, :], g_post2[None, :], wg, wu, wd)


def _rope_tables(t):
    pos = jnp.arange(t, dtype=jnp.int32)
    row = (pos // GRID_W).astype(F32)
    colp = (pos % GRID_W).astype(F32)
    inv_freq = ROPE_THETA ** (-jnp.arange(ROPE_PAIRS, dtype=F32) / ROPE_PAIRS)
    ang_r = row[:, None] * inv_freq
    ang_c = colp[:, None] * inv_freq
    ang = jnp.concatenate([ang_r, ang_r, ang_c, ang_c], axis=-1)
    cos, sin = jnp.cos(ang), jnp.sin(ang)
    first = (jnp.arange(HEAD_DIM) % (2 * ROPE_PAIRS)) < ROPE_PAIRS
    sin_lo = jnp.where(first, -sin, 0.0)
    sin_hi = jnp.where(first, 0.0, sin)
    two = lambda a: jnp.concatenate([a, a], axis=-1)
    return two(cos), two(sin_lo), two(sin_hi)


def _token_tiles(t):
    base = min(SUB_TILE, t)
    medium = 2 * SUB_TILE if t % (2 * SUB_TILE) == 0 else base
    large = 4 * SUB_TILE if t % (4 * SUB_TILE) == 0 else medium
    return base, medium, large


def kernel(x, c, ctx, c_ctx, w_mod, b_mod, norm_pre, norm_post, ffn_w_gate, ffn_w_up, ffn_w_down,
           w_in, q_norm, k_norm, hg_lower_bound, hg_norm, w_att_out, w_hg_out, w_o):
    assert w_in.shape[0] == 1, "single-layer block"
    bsz, t, d = x.shape
    tc = ctx.shape[1]
    assert t % GRID_W == 0 and t % CHUNK == 0 and tc % CHUNK == 0
    tm, tmf, tml = _token_tiles(t)
    _, tmc, tmc_large = _token_tiles(bsz * tc)

    rows = -(-(bsz + 1) // 8) * 8
    cvec = jnp.concatenate([c, c_ctx[None, :], jnp.zeros((rows - bsz - 1, d), c.dtype)], axis=0)
    mods = _modulation(cvec, w_mod[0], b_mod[0])[:, None, :]
    lat_row = lambda b: b
    ctx_row = lambda b: bsz

    wg, wu, wd = ffn_w_gate[0], ffn_w_up[0], ffn_w_down[0]
    w_in_b = w_in[0].astype(BF16)

    x1 = _ffn(x, mods, lat_row, 0, norm_pre[0, 0], norm_post[0, 0], wg, wu, wd, 0, tml)
    h1 = _ffn(ctx.reshape(1, bsz * tc, d), mods, ctx_row, 0, norm_pre[0, 0], norm_post[0, 0], wg, wu, wd, 0,
              tmc_large)

    cos, slo, shi = _rope_tables(t)
    q_gain2 = jnp.concatenate([q_norm[0], q_norm[0]])[None, :]
    k_gain2 = jnp.concatenate([k_norm[0], k_norm[0]])[None, :]
    slots = hg_lower_bound.shape[1]
    lb_raw = jnp.transpose(hg_lower_bound.astype(F32), (1, 0, 2)).reshape(slots, 2 * HG_WIDTH)
    q, k, v, hq, hv, ff, og, mg = _inproj_latent(x1, mods, norm_pre[0, 1], w_in_b, q_gain2, k_gain2, lb_raw,
                                                  cos, slo, shi, tmf)
    ck, cv, chv, cff = (a.reshape(bsz, tc, a.shape[-1]) for a in
                        _inproj_ctx(h1, mods, bsz, norm_pre[0, 1], w_in_b, k_gain2, lb_raw, tmc))

    o_att = _attention(q, k, ck, v, cv, tm, tml)

    o_hg = _hgrn(hg_norm[0], hq, hv, ff, og, chv, cff)

    return _merge_ffn(x1, mods, o_att, o_hg, mg, norm_post[0, 1], w_att_out[0].astype(BF16),
                      w_hg_out[0].astype(BF16), w_o[0].astype(BF16),
                      norm_pre[0, 2], norm_post[0, 2], wg, wu, wd, 1, tmf)
```

```python
import functools

import jax
import jax.numpy as jnp
from jax import lax
from jax.experimental import pallas as pl
from jax.experimental.pallas import tpu as pltpu

EPS = 1e-6
GRID_W = 64
ROPE_THETA = 10000.0
HEAD_DIM = 64
N_Q_HEADS = 8
N_KV_HEADS = 2
GROUP = N_Q_HEADS // N_KV_HEADS
ATT_WIDTH = N_Q_HEADS * HEAD_DIM
KV_WIDTH = N_KV_HEADS * HEAD_DIM
ROPE_PAIRS = HEAD_DIM // 4
ATT_SCALE = HEAD_DIM ** -0.5
LOG2E = 1.4426950408889634
HG_HEADS = 4
HG_DK = 128
HG_DV = 128
HG_WIDTH = HG_HEADS * HG_DK
HG_SCALE = HG_DK ** -0.5
CHUNK = 64
LANES = 128
SUB_TILE = 256
VMEM_LIMIT = 56 * 1024 * 1024
STAGE_ROWS_WIDE = 64
STAGE_ROWS_TALL = 128
STAGE_SLOTS = 4

BF16 = jnp.bfloat16
F32 = jnp.float32

NT_DIMS = (((1,), (1,)), ((), ()))
TN_DIMS = (((0,), (0,)), ((), ()))


def _dot(a, b):
    return jnp.dot(a, b, preferred_element_type=F32)


def _rms(x, gain):
    return x * lax.rsqrt(jnp.mean(x * x, axis=-1, keepdims=True) + EPS) * gain


def _sigmoid(x):
    return 1.0 / (1.0 + jnp.exp(-x))


def _silu(x):
    return x * _sigmoid(x)


def _params(n_grid):
    return pltpu.CompilerParams(dimension_semantics=("parallel",) * n_grid, vmem_limit_bytes=VMEM_LIMIT)


def _ordered_params(n_grid):
    return pltpu.CompilerParams(dimension_semantics=("arbitrary",) * n_grid, vmem_limit_bytes=VMEM_LIMIT)


def _const_spec(shape):
    nd = len(shape)
    return pl.BlockSpec(shape, lambda *_: (0,) * nd, pipeline_mode=pl.Buffered(1))


def _mod_kernel(c_ref, w_ref, b_ref, o_ref):
    a = _silu(c_ref[...]).astype(BF16)
    o_ref[...] = _dot(a, w_ref[...].astype(BF16)) + b_ref[...]


def _modulation(cvec, w_mod, b_mod, tn=1024):
    rows, d = cvec.shape
    n = w_mod.shape[1]
    return pl.pallas_call(
        _mod_kernel,
        grid=(n // tn,),
        in_specs=[pl.BlockSpec((rows, d), lambda j: (0, 0)),
                  pl.BlockSpec((d, tn), lambda j: (0, j)),
                  pl.BlockSpec((1, tn), lambda j: (0, j))],
        out_specs=pl.BlockSpec((rows, tn), lambda j: (0, j)),
        out_shape=jax.ShapeDtypeStruct((rows, n), F32),
        compiler_params=_params(1),
        name="mod",
    )(cvec, w_mod, b_mod[None, :])


def _ffn_half_steps(xs, m_ref, gpre_ref, gpost_ref, wg_ref, wu_ref, wd_ref, mod0, d):
    shift = m_ref[:, (mod0 + 0) * d:(mod0 + 1) * d]
    scale = m_ref[:, (mod0 + 1) * d:(mod0 + 2) * d]
    gate = m_ref[:, (mod0 + 2) * d:(mod0 + 3) * d]

    def hidden(u):
        return (_silu(_dot(u, wg_ref[...])) * _dot(u, wu_ref[...])).astype(BF16)

    def finish(x, h):
        return x + 0.5 * (gate * _rms(_dot(h, wd_ref[...]), gpost_ref[...]))

    us = [(_rms(x, gpre_ref[...]) * (1.0 + scale) + shift).astype(BF16) for x in xs]
    outs, h_prev = [], None
    for j, u in enumerate(us):
        h = hidden(u)
        if j:
            outs.append(finish(xs[j - 1], h_prev))
        h_prev = h
    outs.append(finish(xs[-1], h_prev))
    return outs


def _sub_tiles(rows):
    sub = SUB_TILE if rows % SUB_TILE == 0 else rows
    return [slice(r, r + sub) for r in range(0, rows, sub)]


def _weight_copy(src, stage, sems, i):
    slots, rows = stage.shape[0], stage.shape[1]
    return pltpu.make_async_copy(src.at[pl.ds(i * rows, rows), :], stage.at[i % slots], sems.at[i % slots])


def _load_weight_bf16(src, dst, stage, sems):
    slots, rows = stage.shape[0], stage.shape[1]
    n = src.shape[0] // rows
    for i in range(min(slots - 1, n)):
        _weight_copy(src, stage, sems, i).start()
    for i in range(n):
        if i + slots - 1 < n:
            _weight_copy(src, stage, sems, i + slots - 1).start()
        _weight_copy(src, stage, sems, i).wait()
        dst[i * rows:(i + 1) * rows, :] = stage[i % slots].astype(BF16)


def _load_ffn_weights(w_hbm, w_vmem, stage_wide, stage_tall, sems, slot):
    @pl.when((pl.program_id(0) == 0) & (pl.program_id(1) == 0))
    def _():
        for src, dst in zip(w_hbm, w_vmem):
            stage = stage_wide if src.shape[-1] == stage_wide.shape[-1] else stage_tall
            _load_weight_bf16(src.at[slot], dst, stage, sems)


def _ffn_weight_scratch(d, f):
    return [pltpu.VMEM((d, f), BF16), pltpu.VMEM((d, f), BF16), pltpu.VMEM((f, d), BF16),
            pltpu.VMEM((STAGE_SLOTS, STAGE_ROWS_WIDE, f), F32), pltpu.VMEM((STAGE_SLOTS, STAGE_ROWS_TALL, d), F32),
            pltpu.SemaphoreType.DMA((STAGE_SLOTS,))]


def _ffn_kernel(x_ref, m_ref, gpre_ref, gpost_ref, wg_hbm, wu_hbm, wd_hbm, o_ref,
                wg_ref, wu_ref, wd_ref, stage_wide, stage_tall, sems, *, mod0, slot, d):
    _load_ffn_weights((wg_hbm, wu_hbm, wd_hbm), (wg_ref, wu_ref, wd_ref), stage_wide, stage_tall, sems, slot)
    tiles = _sub_tiles(x_ref.shape[0])
    outs = _ffn_half_steps([x_ref[rows, :] for rows in tiles], m_ref, gpre_ref, gpost_ref,
                           wg_ref, wu_ref, wd_ref, mod0, d)
    for rows, out in zip(tiles, outs):
        o_ref[rows, :] = out


def _ffn(x, mods, mod_row, mod0, g_pre, g_post, wg, wu, wd, slot, tm):
    bsz, t, d = x.shape
    f = wg.shape[-1]
    nm = mods.shape[-1]
    hbm = pl.BlockSpec(memory_space=pl.ANY)
    return pl.pallas_call(
        functools.partial(_ffn_kernel, mod0=mod0, slot=slot, d=d),
        grid=(bsz, t // tm),
        in_specs=[pl.BlockSpec((None, tm, d), lambda b, i: (b, i, 0)),
                  pl.BlockSpec((None, 1, nm), lambda b, i: (mod_row(b), 0, 0)),
                  _const_spec((1, d)), _const_spec((1, d)), hbm, hbm, hbm],
        out_specs=pl.BlockSpec((None, tm, d), lambda b, i: (b, i, 0)),
        out_shape=jax.ShapeDtypeStruct((bsz, t, d), F32),
        scratch_shapes=_ffn_weight_scratch(d, f),
        compiler_params=_ordered_params(2),
        name="ffn",
    )(x, mods, g_pre[None, :], g_post[None, :], wg, wu, wd)


def _log2_forget(raw, lbraw_ref):
    slots = [lbraw_ref[s:s + 1, :] for s in range(lbraw_ref.shape[0])]
    top = functools.reduce(jnp.maximum, slots)
    e = [jnp.exp(s - top) for s in slots]
    lb = e[0] / functools.reduce(jnp.add, e)
    return jnp.log2(lb + (1.0 - lb) * _sigmoid(raw))


def _head_rms64(z, gain):
    lane = lax.broadcasted_iota(jnp.int32, (1, LANES), 1)
    first = lane < HEAD_DIM
    sq = z * z
    lo = jnp.sum(jnp.where(first, sq, 0.0), axis=-1, keepdims=True)
    hi = jnp.sum(jnp.where(first, 0.0, sq), axis=-1, keepdims=True)
    ms = jnp.where(first, lo, hi) * (1.0 / HEAD_DIM)
    return z * lax.rsqrt(ms + EPS) * gain


def _rope128(z, cos, sin_lo, sin_hi):
    q = ROPE_PAIRS
    return z * cos + pltpu.roll(z, LANES - q, 1) * sin_lo + pltpu.roll(z, q, 1) * sin_hi


def _inproj_latent_kernel(x_ref, m_ref, gpre_ref, w_ref, qg_ref, kg_ref, lbraw_ref, cos_ref, slo_ref, shi_ref,
                          q_ref, k_ref, v_ref, hq_ref, hv_ref, ff_ref, og_ref, mg_ref, *, d):
    shift = m_ref[:, 3 * d:4 * d]
    scale = m_ref[:, 4 * d:5 * d]
    tiles = _sub_tiles(x_ref.shape[0])
    us = [(_rms(x_ref[rows, :], gpre_ref[...]) * (1.0 + scale) + shift).astype(BF16) for rows in tiles]
    for rows, u in zip(tiles, us):
        cos, slo, shi = cos_ref[rows, :], slo_ref[rows, :], shi_ref[rows, :]

        def proj(lo, hi, u=u):
            return _dot(u, w_ref[:, lo:hi])

        c0 = 0
        pq = proj(c0, c0 + ATT_WIDTH)
        qs = []
        for j in range(ATT_WIDTH // LANES):
            z = _head_rms64(pq[:, j * LANES:(j + 1) * LANES], qg_ref[...])
            qs.append(_rope128(z, cos, slo, shi) * (ATT_SCALE * LOG2E))
        q_ref[rows, :] = jnp.concatenate(qs, axis=-1).astype(BF16)
        c0 += ATT_WIDTH
        pkv = proj(c0, c0 + 2 * KV_WIDTH)
        k_ref[rows, :] = _rope128(_head_rms64(pkv[:, :KV_WIDTH], kg_ref[...]), cos, slo, shi).astype(BF16)
        v_ref[rows, :] = pkv[:, KV_WIDTH:].astype(BF16)
        c0 += 2 * KV_WIDTH
        hq_ref[rows, :] = _silu(proj(c0, c0 + HG_WIDTH)) * HG_SCALE
        c0 += HG_WIDTH
        hv_ref[rows, :] = proj(c0, c0 + HG_WIDTH).astype(BF16)
        c0 += HG_WIDTH
        ff_ref[rows, :] = _log2_forget(proj(c0, c0 + 2 * HG_WIDTH), lbraw_ref)
        c0 += 2 * HG_WIDTH
        og_ref[rows, :] = proj(c0, c0 + HG_WIDTH)
        c0 += HG_WIDTH
        mg_ref[rows, :] = proj(c0, c0 + 2 * d)


def _inproj_latent(x, mods, g_pre, w_in, q_gain2, k_gain2, lb_raw, cos, slo, shi, tm):
    bsz, t, d = x.shape
    nm = mods.shape[-1]
    n_in = w_in.shape[1]

    def tile(width, dtype):
        return (pl.BlockSpec((None, tm, width), lambda b, i: (b, i, 0)),
                jax.ShapeDtypeStruct((bsz, t, width), dtype))

    outs = [tile(ATT_WIDTH, BF16), tile(KV_WIDTH, BF16), tile(KV_WIDTH, BF16), tile(HG_WIDTH, F32),
            tile(HG_WIDTH, BF16), tile(2 * HG_WIDTH, F32), tile(HG_WIDTH, F32), tile(2 * d, F32)]
    rope_spec = pl.BlockSpec((tm, LANES), lambda b, i: (i, 0))
    return pl.pallas_call(
        functools.partial(_inproj_latent_kernel, d=d),
        grid=(bsz, t // tm),
        in_specs=[pl.BlockSpec((None, tm, d), lambda b, i: (b, i, 0)),
                  pl.BlockSpec((None, 1, nm), lambda b, i: (b, 0, 0)),
                  _const_spec((1, d)), _const_spec((d, n_in)),
                  _const_spec((1, LANES)), _const_spec((1, LANES)), _const_spec(lb_raw.shape),
                  rope_spec, rope_spec, rope_spec],
        out_specs=[o[0] for o in outs],
        out_shape=[o[1] for o in outs],
        compiler_params=_params(2),
        name="inproj_latent",
    )(x, mods, g_pre[None, :], w_in, q_gain2, k_gain2, lb_raw, cos, slo, shi)


def _inproj_ctx_kernel(x_ref, m_ref, gpre_ref, w_ref, kg_ref, lbraw_ref, k_ref, v_ref, hv_ref, ff_ref, *, d):
    x = x_ref[...]
    shift = m_ref[:, 3 * d:4 * d]
    scale = m_ref[:, 4 * d:5 * d]
    u = (_rms(x, gpre_ref[...]) * (1.0 + scale) + shift).astype(BF16)
    kv0 = ATT_WIDTH
    h0 = ATT_WIDTH + 2 * KV_WIDTH + HG_WIDTH
    pkv = _dot(u, w_ref[:, kv0:kv0 + 2 * KV_WIDTH])
    k_ref[...] = _head_rms64(pkv[:, :KV_WIDTH], kg_ref[...]).astype(BF16)
    v_ref[...] = pkv[:, KV_WIDTH:].astype(BF16)
    hv_ref[...] = _dot(u, w_ref[:, h0:h0 + HG_WIDTH]).astype(BF16)
    ff_ref[...] = _log2_forget(_dot(u, w_ref[:, h0 + HG_WIDTH:h0 + 3 * HG_WIDTH]), lbraw_ref)


def _inproj_ctx(x, mods, ctx_row, g_pre, w_in, k_gain2, lb_raw, tm):
    bsz, t, d = x.shape
    nm = mods.shape[-1]

    def tile(width, dtype):
        return (pl.BlockSpec((None, tm, width), lambda b, i: (b, i, 0)),
                jax.ShapeDtypeStruct((bsz, t, width), dtype))

    outs = [tile(KV_WIDTH, BF16), tile(KV_WIDTH, BF16), tile(HG_WIDTH, BF16), tile(2 * HG_WIDTH, F32)]
    return pl.pallas_call(
        functools.partial(_inproj_ctx_kernel, d=d),
        grid=(bsz, t // tm),
        in_specs=[pl.BlockSpec((None, tm, d), lambda b, i: (b, i, 0)),
                  pl.BlockSpec((None, 1, nm), lambda b, i: (ctx_row, 0, 0)),
                  _const_spec((1, d)), _const_spec(w_in.shape),
                  _const_spec((1, LANES)), _const_spec(lb_raw.shape)],
        out_specs=[o[0] for o in outs],
        out_shape=[o[1] for o in outs],
        compiler_params=_params(2),
        name="inproj_ctx",
    )(x, mods, g_pre[None, :], w_in, k_gain2, lb_raw)


def _attn_kernel(q_ref, kl_ref, kc_ref, vl_ref, vc_ref, o_ref, st_ref, pt_ref, vt_ref):
    tq = st_ref.shape[2]
    t_lat = kl_ref.shape[0]
    n_items = (q_ref.shape[0] // tq) * N_Q_HEADS

    @pl.when(pl.program_id(1) == 0)
    def _():
        vt_ref[:, :t_lat] = vl_ref[...].astype(F32).T.astype(BF16)
        vt_ref[:, t_lat:] = vc_ref[...].astype(F32).T.astype(BF16)

    def scores(i):
        r, h = divmod(i, N_Q_HEADS)
        q = q_ref[r * tq:(r + 1) * tq, h * HEAD_DIM:(h + 1) * HEAD_DIM]
        halves = [q, jnp.zeros_like(q)] if h // GROUP == 0 else [jnp.zeros_like(q), q]
        qp = jnp.concatenate(halves, axis=1)
        s_lat = lax.dot_general(kl_ref[...], qp, NT_DIMS, preferred_element_type=F32)
        s_ctx = lax.dot_general(kc_ref[...], qp, NT_DIMS, preferred_element_type=F32)
        buf = i % st_ref.shape[0]
        st_ref[buf, :t_lat, :] = s_lat
        st_ref[buf, t_lat:, :] = s_ctx
        return jnp.maximum(jnp.max(s_lat, axis=0, keepdims=True), jnp.max(s_ctx, axis=0, keepdims=True))

    pairs, sums = [], []

    def values(i):
        kv = (i % N_Q_HEADS) // GROUP
        ot = _dot(vt_ref[kv * HEAD_DIM:(kv + 1) * HEAD_DIM, :], pt_ref[i % 2])
        pairs.append(ot / sums[i])

    ahead = st_ref.shape[0] - 1
    ms = [scores(i) for i in range(ahead)]
    for i in range(n_items):
        if i + ahead < n_items:
            ms.append(scores(i + ahead))
        p = jnp.exp2(st_ref[i % (ahead + 1)] - ms[i])
        sums.append(jnp.sum(p, axis=0, keepdims=True))
        pt_ref[i % 2] = p.astype(BF16)
        if i:
            values(i - 1)
    values(n_items - 1)
    for r in range(n_items // N_Q_HEADS):
        heads = pairs[r * N_Q_HEADS:(r + 1) * N_Q_HEADS]
        outs = [jnp.concatenate(heads[j:j + 2], axis=0).T for j in range(0, N_Q_HEADS, 2)]
        o_ref[r * tq:(r + 1) * tq, :] = jnp.concatenate(outs, axis=-1).astype(BF16)


def _attention(q, k_lat, k_ctx, v_lat, v_ctx, tq, tstep):
    bsz, t, _ = q.shape
    tc = k_ctx.shape[1]
    s = t + tc

    def whole(tt):
        return pl.BlockSpec((None, tt, KV_WIDTH), lambda b, i: (b, 0, 0))

    return pl.pallas_call(
        _attn_kernel,
        grid=(bsz, t // tstep),
        in_specs=[pl.BlockSpec((None, tstep, ATT_WIDTH), lambda b, i: (b, i, 0)),
                  whole(t), whole(tc), whole(t), whole(tc)],
        out_specs=pl.BlockSpec((None, tstep, ATT_WIDTH), lambda b, i: (b, i, 0)),
        out_shape=jax.ShapeDtypeStruct((bsz, t, ATT_WIDTH), BF16),
        scratch_shapes=[pltpu.VMEM((3, s, tq), F32),
                        pltpu.VMEM((2, s, tq), BF16),
                        pltpu.VMEM((KV_WIDTH, s), BF16)],
        compiler_params=pltpu.CompilerParams(dimension_semantics=("parallel", "arbitrary"),
                                             vmem_limit_bytes=VMEM_LIMIT),
        name="attn",
    )(q, k_lat, k_ctx, v_lat, v_ctx)


def _split3(g):
    g1 = g.astype(BF16)
    r1 = g - g1.astype(F32)
    g2 = r1.astype(BF16)
    g3 = (r1 - g2.astype(F32)).astype(BF16)
    return g1, g2, g3


def _hgrn_bidir_kernel(gain_ref, hq_ref, hv_ref, ff_ref, fb_ref, og_ref, cv_ref, cff_ref, cfb_ref,
                       o_ref, acc_ref, qe_ref, ds_ref, dec_ref, st_ref, bk_ref, ke_ref, a_ref,
                       *, n_lat, n_ctx, cpb):
    c = CHUNK
    r = cpb * c
    dk = HG_DK
    row = lax.broadcasted_iota(jnp.int32, (r, r), 0)
    col = lax.broadcasted_iota(jnp.int32, (r, r), 1)
    same_chunk = (row // c) == (col // c)
    masks = (same_chunk & (col <= row), same_chunk & (col >= row))
    tril = jnp.where(masks[0], 1.0, 0.0).astype(BF16)
    last = (c - 1, 0)
    mid = (c // 2 - 1, c // 2)

    def per_chunk_rows(x, off):
        return jnp.concatenate([jnp.broadcast_to(x[j * c + off:j * c + off + 1, :], (c, x.shape[1]))
                                for j in range(cpb)], axis=0)

    def lane_block(d, j):
        return slice((d * cpb + j) * dk, (d * cpb + j + 1) * dk)

    ke_ref[...] = jnp.zeros(ke_ref.shape, BF16)

    blocks = ([((cff_ref, cfb_ref), cv_ref, None, i * r, i * cpb) for i in range(n_ctx // cpb)]
              + [((ff_ref, fb_ref), hv_ref, hq_ref, i * r, n_ctx + i * cpb) for i in range(n_lat // cpb)])

    def decays(n):
        f_refs, _, _, r0, _ = blocks[n]
        lfs = [f_refs[d][r0:r0 + r, :] for d in range(2)]
        bb = _dot(tril, jnp.concatenate([g for lf in lfs for g in _split3(lf)], axis=1))
        prefix = [bb[:, (3 * d) * dk:(3 * d + 1) * dk] + bb[:, (3 * d + 1) * dk:(3 * d + 2) * dk]
                  + bb[:, (3 * d + 2) * dk:(3 * d + 3) * dk] for d in range(2)]
        bk_ref[n % 2, 0] = prefix[0]
        bk_ref[n % 2, 1] = per_chunk_rows(prefix[1], c - 1) - prefix[1] + lfs[1]
        for d in range(2):
            bk_ref[n % 2, 2 + d] = 1.0 - jnp.exp2(lfs[d])

    def scores(n):
        _, _, q_ref, r0, ch0 = blocks[n]
        q = None if q_ref is None else q_ref[r0:r0 + r, :]
        amat = None
        for d in range(2):
            b, k = bk_ref[n % 2, d], bk_ref[n % 2, 2 + d]
            b_last = per_chunk_rows(b, last[d])
            ke = (k * jnp.exp2(b_last - b)).astype(BF16)
            for j in range(cpb):
                ke_ref[n % 2, j * c:(j + 1) * c, lane_block(d, j)] = ke[j * c:(j + 1) * c, :]
                dec_ref[d, ch0 + j] = jnp.exp2(b[j * c + last[d]:j * c + last[d] + 1, :])
            if q is None:
                continue
            b_mid = per_chunk_rows(b, mid[d])
            qd = (q * jnp.exp2(b - b_mid)).astype(BF16)
            kd = (k * jnp.exp2(b_mid - b)).astype(BF16)
            a = jnp.where(masks[d], lax.dot_general(qd, kd, NT_DIMS, preferred_element_type=F32), 0.0)
            amat = a if amat is None else amat + a
            qe_ref[r0:r0 + r, d * dk:(d + 1) * dk] = (q * jnp.exp2(b)).astype(BF16)
        if q is not None:
            a_ref[n % 2] = amat.astype(BF16)

    def products(n):
        _, v_ref, q_ref, r0, ch0 = blocks[n]
        v = v_ref[r0:r0 + r, :]
        ds = lax.dot_general(v, ke_ref[n % 2], TN_DIMS, preferred_element_type=F32)
        for d in range(2):
            for j in range(cpb):
                lo = (d * cpb + j) * dk
                ds_ref[d, ch0 + j] = ds[:, lo:lo + dk]
        if q_ref is not None:
            acc_ref[r0:r0 + r, :] = _dot(a_ref[n % 2], v)

    decays(0)
    for n in range(len(blocks)):
        if n + 1 < len(blocks):
            decays(n + 1)
        scores(n)
        if n:
            products(n - 1)
    products(len(blocks) - 1)

    for d in range(2):
        def ctx_step(i, st):
            ch = (n_ctx - 1 - i) if d else i
            return st * dec_ref[d, ch] + ds_ref[d, ch]

        def lat_steps(i, st):
            blk = (n_lat // cpb - 1 - i) if d else i
            for jj in range(cpb):
                j = (cpb - 1 - jj) if d else jj
                st_ref[blk * cpb + j, :, d * dk:(d + 1) * dk] = st.astype(BF16)
                ch = n_ctx + blk * cpb + j
                st = st * dec_ref[d, ch] + ds_ref[d, ch]
            return st

        st = lax.fori_loop(0, n_ctx, ctx_step, jnp.zeros((HG_DV, dk), F32), unroll=True)
        lax.fori_loop(0, n_lat // cpb, lat_steps, st)

    def pass3(i, carry):
        for j in range(cpb):
            rows = pl.ds(pl.multiple_of(i * r + j * c, c), c)
            inter = lax.dot_general(qe_ref[rows, :], st_ref[i * cpb + j], NT_DIMS, preferred_element_type=F32)
            tot = acc_ref[rows, :] + inter
            o_ref[rows, :] = (_rms(tot, gain_ref[...]) * _silu(og_ref[rows, :])).astype(BF16)
        return carry

    lax.fori_loop(0, n_lat // cpb, pass3, 0, unroll=4)


def _hgrn(hg_gain, hq, hv, ff, og, cv, cff):
    bsz, t, _ = hq.shape
    tc = cv.shape[1]

    def col(tt, off=0):
        return pl.BlockSpec((None, tt, HG_DK), lambda b, h: (b, 0, h + off))

    n_lat, n_ctx = t // CHUNK, tc // CHUNK
    cpb = next(n for n in (4, 2, 1) if n_lat % n == 0 and n_ctx % n == 0)
    return pl.pallas_call(
        functools.partial(_hgrn_bidir_kernel, n_lat=n_lat, n_ctx=n_ctx, cpb=cpb),
        grid=(bsz, HG_HEADS),
        in_specs=[pl.BlockSpec((1, HG_DV), lambda b, h: (0, 0)),
                  col(t), col(t), col(t), col(t, HG_HEADS), col(t),
                  col(tc), col(tc), col(tc, HG_HEADS)],
        out_specs=col(t),
        out_shape=jax.ShapeDtypeStruct((bsz, t, HG_WIDTH), BF16),
        scratch_shapes=[pltpu.VMEM((t, HG_DV), F32),
                        pltpu.VMEM((t, 2 * HG_DK), BF16),
                        pltpu.VMEM((2, n_ctx + n_lat, HG_DV, HG_DK), F32),
                        pltpu.VMEM((2, n_ctx + n_lat, 1, HG_DK), F32),
                        pltpu.VMEM((n_lat, HG_DV, 2 * HG_DK), BF16),
                        pltpu.VMEM((2, 4, cpb * CHUNK, HG_DK), F32),
                        pltpu.VMEM((2, cpb * CHUNK, 2 * cpb * HG_DK), BF16),
                        pltpu.VMEM((2, cpb * CHUNK, cpb * CHUNK), BF16)],
        compiler_params=_params(2),
        name="hgrn",
    )(hg_gain[None, :], hq, hv, ff, ff, og, cv, cff, cff)


def _merge_ffn_kernel(x_ref, m_ref, oa_ref, oh_ref, mg_ref, gpost1_ref, wa_ref, wh_ref, wo_ref,
                      gpre2_ref, gpost2_ref, wg_hbm, wu_hbm, wd_hbm, o_ref,
                      wg_ref, wu_ref, wd_ref, stage_wide, stage_tall, sems, *, slot, d):
    _load_ffn_weights((wg_hbm, wu_hbm, wd_hbm), (wg_ref, wu_ref, wd_ref), stage_wide, stage_tall, sems, slot)
    gate = m_ref[:, 5 * d:6 * d]
    tiles, x2 = _sub_tiles(x_ref.shape[0]), []
    for rows in tiles:
        y = (_sigmoid(mg_ref[rows, :d]) * _dot(oa_ref[rows, :], wa_ref[...])
             + _sigmoid(mg_ref[rows, d:]) * _dot(oh_ref[rows, :], wh_ref[...]))
        z = _dot(y.astype(BF16), wo_ref[...])
        x2.append(x_ref[rows, :] + gate * _rms(z, gpost1_ref[...]))
    outs = _ffn_half_steps(x2, m_ref, gpre2_ref, gpost2_ref, wg_ref, wu_ref, wd_ref, 6, d)
    for rows, out in zip(tiles, outs):
        o_ref[rows, :] = out


def _merge_ffn(x, mods, o_att, o_hg, mg, g_post1, wa, wh, wo, g_pre2, g_post2, wg, wu, wd, slot, tm):
    bsz, t, d = x.shape
    nm = mods.shape[-1]
    hbm = pl.BlockSpec(memory_space=pl.ANY)

    def tile(width):
        return pl.BlockSpec((None, tm, width), lambda b, i: (b, i, 0))

    return pl.pallas_call(
        functools.partial(_merge_ffn_kernel, slot=slot, d=d),
        grid=(bsz, t // tm),
        in_specs=[tile(d), pl.BlockSpec((None, 1, nm), lambda b, i: (b, 0, 0)),
                  tile(ATT_WIDTH), tile(HG_WIDTH), tile(2 * d), _const_spec((1, d)),
                  _const_spec(wa.shape), _const_spec(wh.shape), _const_spec(wo.shape),
                  _const_spec((1, d)), _const_spec((1, d)), hbm, hbm, hbm],
        out_specs=tile(d),
        out_shape=jax.ShapeDtypeStruct((bsz, t, d), F32),
        scratch_shapes=_ffn_weight_scratch(d, wg.shape[-1]),
        compiler_params=_ordered_params(2),
        name="merge_ffn",
    )(x, mods, o_att, o_hg, mg, g_post1[None, :], wa, wh, wo, g_pre2[None, :], g_post2[None, :], wg, wu, wd)


def _rope_tables(t):
    pos = jnp.arange(t, dtype=jnp.int32)
    row = (pos // GRID_W).astype(F32)
    colp = (pos % GRID_W).astype(F32)
    inv_freq = ROPE_THETA ** (-jnp.arange(ROPE_PAIRS, dtype=F32) / ROPE_PAIRS)
    ang_r = row[:, None] * inv_freq
    ang_c = colp[:, None] * inv_freq
    ang = jnp.concatenate([ang_r, ang_r, ang_c, ang_c], axis=-1)
    cos, sin = jnp.cos(ang), jnp.sin(ang)
    first = (jnp.arange(HEAD_DIM) % (2 * ROPE_PAIRS)) < ROPE_PAIRS
    sin_lo = jnp.where(first, -sin, 0.0)
    sin_hi = jnp.where(first, 0.0, sin)
    two = lambda a: jnp.concatenate([a, a], axis=-1)
    return two(cos), two(sin_lo), two(sin_hi)


def _token_tiles(t):
    base = min(SUB_TILE, t)
    medium = 2 * SUB_TILE if t % (2 * SUB_TILE) == 0 else base
    large = 4 * SUB_TILE if t % (4 * SUB_TILE) == 0 else medium
    return base, medium, large


def kernel(x, c, ctx, c_ctx, w_mod, b_mod, norm_pre, norm_post, ffn_w_gate, ffn_w_up, ffn_w_down,
           w_in, q_norm, k_norm, hg_lower_bound, hg_norm, w_att_out, w_hg_out, w_o):
    assert w_in.shape[0] == 1, "single-layer block"
    bsz, t, d = x.shape
    tc = ctx.shape[1]
    assert t % GRID_W == 0 and t % CHUNK == 0 and tc % CHUNK == 0
    tm, tmf, tml = _token_tiles(t)
    _, tmc, tmc_large = _token_tiles(bsz * tc)

    rows = -(-(bsz + 1) // 8) * 8
    cvec = jnp.concatenate([c, c_ctx[None, :], jnp.zeros((rows - bsz - 1, d), c.dtype)], axis=0)
    mods = _modulation(cvec, w_mod[0], b_mod[0])[:, None, :]
    lat_row = lambda b: b
    ctx_row = lambda b: bsz

    wg, wu, wd = ffn_w_gate[0], ffn_w_up[0], ffn_w_down[0]
    w_in_b = w_in[0].astype(BF16)

    x1 = _ffn(x, mods, lat_row, 0, norm_pre[0, 0], norm_post[0, 0], wg, wu, wd, 0, tml)
    h1 = _ffn(ctx.reshape(1, bsz * tc, d), mods, ctx_row, 0, norm_pre[0, 0], norm_post[0, 0], wg, wu, wd, 0,
              tmc_large)

    cos, slo, shi = _rope_tables(t)
    q_gain2 = jnp.concatenate([q_norm[0], q_norm[0]])[None, :]
    k_gain2 = jnp.concatenate([k_norm[0], k_norm[0]])[None, :]
    slots = hg_lower_bound.shape[1]
    lb_raw = jnp.transpose(hg_lower_bound.astype(F32), (1, 0, 2)).reshape(slots, 2 * HG_WIDTH)
    q, k, v, hq, hv, ff, og, mg = _inproj_latent(x1, mods, norm_pre[0, 1], w_in_b, q_gain2, k_gain2, lb_raw,
                                                  cos, slo, shi, tmf)
    ck, cv, chv, cff = (a.reshape(bsz, tc, a.shape[-1]) for a in
                        _inproj_ctx(h1, mods, bsz, norm_pre[0, 1], w_in_b, k_gain2, lb_raw, tmc))

    o_att = _attention(q, k, ck, v, cv, tm, tml)

    o_hg = _hgrn(hg_norm[0], hq, hv, ff, og, chv, cff)

    return _merge_ffn(x1, mods, o_att, o_hg, mg, norm_post[0, 1], w_att_out[0].astype(BF16),
                      w_hg_out[0].astype(BF16), w_o[0].astype(BF16),
                      norm_pre[0, 2], norm_post[0, 2], wg, wu, wd, 1, tmf)
```

```python
import functools

import jax
import jax.numpy as jnp
from jax import lax
from jax.experimental import pallas as pl
from jax.experimental.pallas import tpu as pltpu

EPS = 1e-6
GRID_W = 64
ROPE_THETA = 10000.0
HEAD_DIM = 64
N_Q_HEADS = 8
N_KV_HEADS = 2
GROUP = N_Q_HEADS // N_KV_HEADS
ATT_WIDTH = N_Q_HEADS * HEAD_DIM
KV_WIDTH = N_KV_HEADS * HEAD_DIM
ROPE_PAIRS = HEAD_DIM // 4
ATT_SCALE = HEAD_DIM ** -0.5
LOG2E = 1.4426950408889634
HG_HEADS = 4
HG_DK = 128
HG_DV = 128
HG_WIDTH = HG_HEADS * HG_DK
HG_SCALE = HG_DK ** -0.5
CHUNK = 64
LANES = 128
SUB_TILE = 256
VMEM_LIMIT = 56 * 1024 * 1024
STAGE_ROWS_WIDE = 64
STAGE_ROWS_TALL = 128
STAGE_SLOTS = 4

BF16 = jnp.bfloat16
F32 = jnp.float32

NT_DIMS = (((1,), (1,)), ((), ()))
TN_DIMS = (((0,), (0,)), ((), ()))


def _dot(a, b):
    return jnp.dot(a, b, preferred_element_type=F32)


def _rms(x, gain):
    return x * lax.rsqrt(jnp.mean(x * x, axis=-1, keepdims=True) + EPS) * gain


def _sigmoid(x):
    return 1.0 / (1.0 + jnp.exp(-x))


def _silu(x):
    return x * _sigmoid(x)


def _params(n_grid):
    return pltpu.CompilerParams(dimension_semantics=("parallel",) * n_grid, vmem_limit_bytes=VMEM_LIMIT)


def _ordered_params(n_grid):
    return pltpu.CompilerParams(dimension_semantics=("arbitrary",) * n_grid, vmem_limit_bytes=VMEM_LIMIT)


def _const_spec(shape):
    nd = len(shape)
    return pl.BlockSpec(shape, lambda *_: (0,) * nd, pipeline_mode=pl.Buffered(1))


def _mod_kernel(c_ref, w_ref, b_ref, o_ref):
    a = _silu(c_ref[...]).astype(BF16)
    o_ref[...] = _dot(a, w_ref[...].astype(BF16)) + b_ref[...]


def _modulation(cvec, w_mod, b_mod):
    rows, d = cvec.shape
    n = w_mod.shape[1]
    tn = 3 * d
    assert n % tn == 0
    return pl.pallas_call(
        _mod_kernel,
        grid=(n // tn,),
        in_specs=[pl.BlockSpec((rows, d), lambda j: (0, 0)),
                  pl.BlockSpec((d, tn), lambda j: (0, j)),
                  pl.BlockSpec((1, tn), lambda j: (0, j))],
        out_specs=pl.BlockSpec((rows, tn), lambda j: (0, j)),
        out_shape=jax.ShapeDtypeStruct((rows, n), F32),
        compiler_params=_params(1),
        name="mod",
    )(cvec, w_mod, b_mod[None, :])


def _ffn_half_steps(xs, m_ref, gpre_ref, gpost_ref, wg_ref, wu_ref, wd_ref, mod0, d):
    shift = m_ref[:, (mod0 + 0) * d:(mod0 + 1) * d]
    scale = m_ref[:, (mod0 + 1) * d:(mod0 + 2) * d]
    gate = m_ref[:, (mod0 + 2) * d:(mod0 + 3) * d]

    def hidden(u):
        return (_silu(_dot(u, wg_ref[...])) * _dot(u, wu_ref[...])).astype(BF16)

    def finish(x, h):
        return x + 0.5 * (gate * _rms(_dot(h, wd_ref[...]), gpost_ref[...]))

    us = [(_rms(x, gpre_ref[...]) * (1.0 + scale) + shift).astype(BF16) for x in xs]
    outs, h_prev = [], None
    for j, u in enumerate(us):
        h = hidden(u)
        if j:
            outs.append(finish(xs[j - 1], h_prev))
        h_prev = h
    outs.append(finish(xs[-1], h_prev))
    return outs


def _sub_tiles(rows):
    sub = SUB_TILE if rows % SUB_TILE == 0 else rows
    return [slice(r, r + sub) for r in range(0, rows, sub)]


def _weight_copy(src, stage, sems, i):
    slots, rows = stage.shape[0], stage.shape[1]
    return pltpu.make_async_copy(src.at[pl.ds(i * rows, rows), :], stage.at[i % slots], sems.at[i % slots])


def _load_weight_bf16(src, dst, stage, sems):
    slots, rows = stage.shape[0], stage.shape[1]
    n = src.shape[0] // rows
    for i in range(min(slots - 1, n)):
        _weight_copy(src, stage, sems, i).start()
    for i in range(n):
        if i + slots - 1 < n:
            _weight_copy(src, stage, sems, i + slots - 1).start()
        _weight_copy(src, stage, sems, i).wait()
        dst[i * rows:(i + 1) * rows, :] = stage[i % slots].astype(BF16)


def _load_ffn_weights(w_hbm, w_vmem, stage_wide, stage_tall, sems, slot):
    @pl.when((pl.program_id(0) == 0) & (pl.program_id(1) == 0))
    def _():
        for src, dst in zip(w_hbm, w_vmem):
            stage = stage_wide if src.shape[-1] == stage_wide.shape[-1] else stage_tall
            _load_weight_bf16(src.at[slot], dst, stage, sems)


def _ffn_weight_scratch(d, f):
    return [pltpu.VMEM((d, f), BF16), pltpu.VMEM((d, f), BF16), pltpu.VMEM((f, d), BF16),
            pltpu.VMEM((STAGE_SLOTS, STAGE_ROWS_WIDE, f), F32), pltpu.VMEM((STAGE_SLOTS, STAGE_ROWS_TALL, d), F32),
            pltpu.SemaphoreType.DMA((STAGE_SLOTS,))]


def _ffn_kernel(x_ref, m_ref, gpre_ref, gpost_ref, wg_hbm, wu_hbm, wd_hbm, o_ref,
                wg_ref, wu_ref, wd_ref, stage_wide, stage_tall, sems, *, mod0, slot, d):
    _load_ffn_weights((wg_hbm, wu_hbm, wd_hbm), (wg_ref, wu_ref, wd_ref), stage_wide, stage_tall, sems, slot)
    tiles = _sub_tiles(x_ref.shape[0])
    outs = _ffn_half_steps([x_ref[rows, :] for rows in tiles], m_ref, gpre_ref, gpost_ref,
                           wg_ref, wu_ref, wd_ref, mod0, d)
    for rows, out in zip(tiles, outs):
        o_ref[rows, :] = out


def _ffn(x, mods, mod_row, mod0, g_pre, g_post, wg, wu, wd, slot, tm):
    bsz, t, d = x.shape
    f = wg.shape[-1]
    nm = mods.shape[-1]
    hbm = pl.BlockSpec(memory_space=pl.ANY)
    return pl.pallas_call(
        functools.partial(_ffn_kernel, mod0=mod0, slot=slot, d=d),
        grid=(bsz, t // tm),
        in_specs=[pl.BlockSpec((None, tm, d), lambda b, i: (b, i, 0)),
                  pl.BlockSpec((None, 1, nm), lambda b, i: (mod_row(b), 0, 0)),
                  _const_spec((1, d)), _const_spec((1, d)), hbm, hbm, hbm],
        out_specs=pl.BlockSpec((None, tm, d), lambda b, i: (b, i, 0)),
        out_shape=jax.ShapeDtypeStruct((bsz, t, d), F32),
        scratch_shapes=_ffn_weight_scratch(d, f),
        compiler_params=_ordered_params(2),
        name="ffn",
    )(x, mods, g_pre[None, :], g_post[None, :], wg, wu, wd)


def _log2_forget(raw, lbraw_ref):
    slots = [lbraw_ref[s:s + 1, :] for s in range(lbraw_ref.shape[0])]
    top = functools.reduce(jnp.maximum, slots)
    e = [jnp.exp(s - top) for s in slots]
    lb = e[0] / functools.reduce(jnp.add, e)
    return jnp.log2(lb + (1.0 - lb) * _sigmoid(raw))


def _head_rms64(z, gain):
    lane = lax.broadcasted_iota(jnp.int32, (1, LANES), 1)
    first = lane < HEAD_DIM
    sq = z * z
    lo = jnp.sum(jnp.where(first, sq, 0.0), axis=-1, keepdims=True)
    hi = jnp.sum(jnp.where(first, 0.0, sq), axis=-1, keepdims=True)
    ms = jnp.where(first, lo, hi) * (1.0 / HEAD_DIM)
    return z * lax.rsqrt(ms + EPS) * gain


def _rope128(z, cos, sin_lo, sin_hi):
    q = ROPE_PAIRS
    return z * cos + pltpu.roll(z, LANES - q, 1) * sin_lo + pltpu.roll(z, q, 1) * sin_hi


def _inproj_latent_kernel(x_ref, m_ref, gpre_ref, w_ref, qg_ref, kg_ref, lbraw_ref, cos_ref, slo_ref, shi_ref,
                          q_ref, k_ref, v_ref, hq_ref, hv_ref, ff_ref, og_ref, mg_ref, *, d):
    shift = m_ref[:, 3 * d:4 * d]
    scale = m_ref[:, 4 * d:5 * d]
    tiles = _sub_tiles(x_ref.shape[0])
    us = [(_rms(x_ref[rows, :], gpre_ref[...]) * (1.0 + scale) + shift).astype(BF16) for rows in tiles]
    for rows, u in zip(tiles, us):
        cos, slo, shi = cos_ref[rows, :], slo_ref[rows, :], shi_ref[rows, :]

        def proj(lo, hi, u=u):
            return _dot(u, w_ref[:, lo:hi])

        c_q, c_kv = 0, ATT_WIDTH
        c_hq = c_kv + 2 * KV_WIDTH
        c_hv = c_hq + HG_WIDTH
        c_ff = c_hv + HG_WIDTH
        c_og = c_ff + 2 * HG_WIDTH
        c_mg = c_og + HG_WIDTH
        pq = proj(c_q, c_q + ATT_WIDTH)
        qs = []
        for j in range(ATT_WIDTH // LANES):
            z = _head_rms64(pq[:, j * LANES:(j + 1) * LANES], qg_ref[...])
            qs.append(_rope128(z, cos, slo, shi) * (ATT_SCALE * LOG2E))
        q_ref[rows, :] = jnp.concatenate(qs, axis=-1).astype(BF16)
        mg_ref[rows, :d] = proj(c_mg, c_mg + d)
        pkv = proj(c_kv, c_kv + 2 * KV_WIDTH)
        k_ref[rows, :] = _rope128(_head_rms64(pkv[:, :KV_WIDTH], kg_ref[...]), cos, slo, shi).astype(BF16)
        v_ref[rows, :] = pkv[:, KV_WIDTH:].astype(BF16)
        og_ref[rows, :] = proj(c_og, c_og + HG_WIDTH)
        hq_ref[rows, :] = _silu(proj(c_hq, c_hq + HG_WIDTH)) * HG_SCALE
        mg_ref[rows, d:] = proj(c_mg + d, c_mg + 2 * d)
        ff_ref[rows, :] = _log2_forget(proj(c_ff, c_ff + 2 * HG_WIDTH), lbraw_ref)
        hv_ref[rows, :] = proj(c_hv, c_hv + HG_WIDTH).astype(BF16)


def _inproj_latent(x, mods, g_pre, w_in, q_gain2, k_gain2, lb_raw, cos, slo, shi, tm):
    bsz, t, d = x.shape
    nm = mods.shape[-1]
    n_in = w_in.shape[1]

    def tile(width, dtype):
        return (pl.BlockSpec((None, tm, width), lambda b, i: (b, i, 0)),
                jax.ShapeDtypeStruct((bsz, t, width), dtype))

    outs = [tile(ATT_WIDTH, BF16), tile(KV_WIDTH, BF16), tile(KV_WIDTH, BF16), tile(HG_WIDTH, F32),
            tile(HG_WIDTH, BF16), tile(2 * HG_WIDTH, F32), tile(HG_WIDTH, F32), tile(2 * d, F32)]
    rope_spec = pl.BlockSpec((tm, LANES), lambda b, i: (i, 0))
    return pl.pallas_call(
        functools.partial(_inproj_latent_kernel, d=d),
        grid=(bsz, t // tm),
        in_specs=[pl.BlockSpec((None, tm, d), lambda b, i: (b, i, 0)),
                  pl.BlockSpec((None, 1, nm), lambda b, i: (b, 0, 0)),
                  _const_spec((1, d)), _const_spec((d, n_in)),
                  _const_spec((1, LANES)), _const_spec((1, LANES)), _const_spec(lb_raw.shape),
                  rope_spec, rope_spec, rope_spec],
        out_specs=[o[0] for o in outs],
        out_shape=[o[1] for o in outs],
        compiler_params=_params(2),
        name="inproj_latent",
    )(x, mods, g_pre[None, :], w_in, q_gain2, k_gain2, lb_raw, cos, slo, shi)


def _inproj_ctx_kernel(x_ref, m_ref, gpre_ref, w_ref, kg_ref, lbraw_ref, k_ref, v_ref, hv_ref, ff_ref, *, d):
    x = x_ref[...]
    shift = m_ref[:, 3 * d:4 * d]
    scale = m_ref[:, 4 * d:5 * d]
    u = (_rms(x, gpre_ref[...]) * (1.0 + scale) + shift).astype(BF16)
    kv0 = ATT_WIDTH
    h0 = ATT_WIDTH + 2 * KV_WIDTH + HG_WIDTH
    pkv = _dot(u, w_ref[:, kv0:kv0 + 2 * KV_WIDTH])
    k_ref[...] = _head_rms64(pkv[:, :KV_WIDTH], kg_ref[...]).astype(BF16)
    v_ref[...] = pkv[:, KV_WIDTH:].astype(BF16)
    hv_ref[...] = _dot(u, w_ref[:, h0:h0 + HG_WIDTH]).astype(BF16)
    ff_ref[...] = _log2_forget(_dot(u, w_ref[:, h0 + HG_WIDTH:h0 + 3 * HG_WIDTH]), lbraw_ref)


def _inproj_ctx(x, mods, ctx_row, g_pre, w_in, k_gain2, lb_raw, tm):
    bsz, t, d = x.shape
    nm = mods.shape[-1]

    def tile(width, dtype):
        return (pl.BlockSpec((None, tm, width), lambda b, i: (b, i, 0)),
                jax.ShapeDtypeStruct((bsz, t, width), dtype))

    outs = [tile(KV_WIDTH, BF16), tile(KV_WIDTH, BF16), tile(HG_WIDTH, BF16), tile(2 * HG_WIDTH, F32)]
    return pl.pallas_call(
        functools.partial(_inproj_ctx_kernel, d=d),
        grid=(bsz, t // tm),
        in_specs=[pl.BlockSpec((None, tm, d), lambda b, i: (b, i, 0)),
                  pl.BlockSpec((None, 1, nm), lambda b, i: (ctx_row, 0, 0)),
                  _const_spec((1, d)), _const_spec(w_in.shape),
                  _const_spec((1, LANES)), _const_spec(lb_raw.shape)],
        out_specs=[o[0] for o in outs],
        out_shape=[o[1] for o in outs],
        compiler_params=_params(2),
        name="inproj_ctx",
    )(x, mods, g_pre[None, :], w_in, k_gain2, lb_raw)


def _attn_kernel(q_ref, kl_ref, kc_ref, vl_ref, vc_ref, o_ref, st_ref, pt_ref, vt_ref):
    tq = st_ref.shape[2]
    t_lat = kl_ref.shape[0]
    n_items = (q_ref.shape[0] // tq) * N_Q_HEADS

    @pl.when(pl.program_id(1) == 0)
    def _():
        vt_ref[:, :t_lat] = vl_ref[...].astype(F32).T.astype(BF16)
        vt_ref[:, t_lat:] = vc_ref[...].astype(F32).T.astype(BF16)

    def scores(i):
        r, h = divmod(i, N_Q_HEADS)
        q = q_ref[r * tq:(r + 1) * tq, h * HEAD_DIM:(h + 1) * HEAD_DIM]
        halves = [q, jnp.zeros_like(q)] if h // GROUP == 0 else [jnp.zeros_like(q), q]
        qp = jnp.concatenate(halves, axis=1)
        s_lat = lax.dot_general(kl_ref[...], qp, NT_DIMS, preferred_element_type=F32)
        s_ctx = lax.dot_general(kc_ref[...], qp, NT_DIMS, preferred_element_type=F32)
        buf = i % st_ref.shape[0]
        st_ref[buf, :t_lat, :] = s_lat
        st_ref[buf, t_lat:, :] = s_ctx
        return jnp.maximum(jnp.max(s_lat, axis=0, keepdims=True), jnp.max(s_ctx, axis=0, keepdims=True))

    pairs, sums = [], []

    def values(i):
        kv = (i % N_Q_HEADS) // GROUP
        ot = _dot(vt_ref[kv * HEAD_DIM:(kv + 1) * HEAD_DIM, :], pt_ref[i % 2])
        pairs.append(ot / sums[i])

    ahead = st_ref.shape[0] - 1
    ms = [scores(i) for i in range(ahead)]
    for i in range(n_items):
        if i + ahead < n_items:
            ms.append(scores(i + ahead))
        p = jnp.exp2(st_ref[i % (ahead + 1)] - ms[i])
        sums.append(jnp.sum(p, axis=0, keepdims=True))
        pt_ref[i % 2] = p.astype(BF16)
        if i:
            values(i - 1)
    values(n_items - 1)
    for r in range(n_items // N_Q_HEADS):
        heads = pairs[r * N_Q_HEADS:(r + 1) * N_Q_HEADS]
        outs = [jnp.concatenate(heads[j:j + 2], axis=0).T for j in range(0, N_Q_HEADS, 2)]
        o_ref[r * tq:(r + 1) * tq, :] = jnp.concatenate(outs, axis=-1).astype(BF16)


def _attention(q, k_lat, k_ctx, v_lat, v_ctx, tq, tstep):
    bsz, t, _ = q.shape
    tc = k_ctx.shape[1]
    s = t + tc

    def whole(tt):
        return pl.BlockSpec((None, tt, KV_WIDTH), lambda b, i: (b, 0, 0))

    return pl.pallas_call(
        _attn_kernel,
        grid=(bsz, t // tstep),
        in_specs=[pl.BlockSpec((None, tstep, ATT_WIDTH), lambda b, i: (b, i, 0)),
                  whole(t), whole(tc), whole(t), whole(tc)],
        out_specs=pl.BlockSpec((None, tstep, ATT_WIDTH), lambda b, i: (b, i, 0)),
        out_shape=jax.ShapeDtypeStruct((bsz, t, ATT_WIDTH), BF16),
        scratch_shapes=[pltpu.VMEM((3, s, tq), F32),
                        pltpu.VMEM((2, s, tq), BF16),
                        pltpu.VMEM((KV_WIDTH, s), BF16)],
        compiler_params=pltpu.CompilerParams(dimension_semantics=("parallel", "arbitrary"),
                                             vmem_limit_bytes=VMEM_LIMIT),
        name="attn",
    )(q, k_lat, k_ctx, v_lat, v_ctx)


def _split3(g):
    g1 = g.astype(BF16)
    r1 = g - g1.astype(F32)
    g2 = r1.astype(BF16)
    g3 = (r1 - g2.astype(F32)).astype(BF16)
    return g1, g2, g3


def _hgrn_bidir_kernel(gain_ref, hq_ref, hv_ref, ff_ref, fb_ref, og_ref, cv_ref, cff_ref, cfb_ref,
                       o_ref, acc_ref, qe_ref, ds_ref, dec_ref, st_ref, bk_ref, ke_ref, a_ref,
                       *, n_lat, n_ctx, cpb):
    c = CHUNK
    r = cpb * c
    dk = HG_DK
    row = lax.broadcasted_iota(jnp.int32, (r, r), 0)
    col = lax.broadcasted_iota(jnp.int32, (r, r), 1)
    same_chunk = (row // c) == (col // c)
    masks = (same_chunk & (col <= row), same_chunk & (col >= row))
    tril = jnp.where(masks[0], 1.0, 0.0).astype(BF16)
    last = (c - 1, 0)
    mid = (c // 2 - 1, c // 2)

    def per_chunk_rows(x, off):
        return jnp.concatenate([jnp.broadcast_to(x[j * c + off:j * c + off + 1, :], (c, x.shape[1]))
                                for j in range(cpb)], axis=0)

    def lane_block(d, j):
        return slice((d * cpb + j) * dk, (d * cpb + j + 1) * dk)

    ke_ref[...] = jnp.zeros(ke_ref.shape, BF16)

    blocks = ([((cff_ref, cfb_ref), cv_ref, None, i * r, i * cpb) for i in range(n_ctx // cpb)]
              + [((ff_ref, fb_ref), hv_ref, hq_ref, i * r, n_ctx + i * cpb) for i in range(n_lat // cpb)])

    def decays(n):
        f_refs, _, _, r0, _ = blocks[n]
        lfs = [f_refs[d][r0:r0 + r, :] for d in range(2)]
        bb = _dot(tril, jnp.concatenate([g for lf in lfs for g in _split3(lf)], axis=1))
        prefix = [bb[:, (3 * d) * dk:(3 * d + 1) * dk] + bb[:, (3 * d + 1) * dk:(3 * d + 2) * dk]
                  + bb[:, (3 * d + 2) * dk:(3 * d + 3) * dk] for d in range(2)]
        bk_ref[n % 2, 0] = prefix[0]
        bk_ref[n % 2, 1] = per_chunk_rows(prefix[1], c - 1) - prefix[1] + lfs[1]
        for d in range(2):
            bk_ref[n % 2, 2 + d] = 1.0 - jnp.exp2(lfs[d])

    def scores(n):
        _, _, q_ref, r0, ch0 = blocks[n]
        q = None if q_ref is None else q_ref[r0:r0 + r, :]
        amat = None
        for d in range(2):
            b, k = bk_ref[n % 2, d], bk_ref[n % 2, 2 + d]
            b_last = per_chunk_rows(b, last[d])
            ke = (k * jnp.exp2(b_last - b)).astype(BF16)
            for j in range(cpb):
                ke_ref[n % 2, j * c:(j + 1) * c, lane_block(d, j)] = ke[j * c:(j + 1) * c, :]
                dec_ref[d, ch0 + j] = jnp.exp2(b[j * c + last[d]:j * c + last[d] + 1, :])
            if q is None:
                continue
            b_mid = per_chunk_rows(b, mid[d])
            qd = (q * jnp.exp2(b - b_mid)).astype(BF16)
            kd = (k * jnp.exp2(b_mid - b)).astype(BF16)
            a = jnp.where(masks[d], lax.dot_general(qd, kd, NT_DIMS, preferred_element_type=F32), 0.0)
            amat = a if amat is None else amat + a
            qe_ref[r0:r0 + r, d * dk:(d + 1) * dk] = (q * jnp.exp2(b)).astype(BF16)
        if q is not None:
            a_ref[n % 2] = amat.astype(BF16)

    def products(n):
        _, v_ref, q_ref, r0, ch0 = blocks[n]
        v = v_ref[r0:r0 + r, :]
        ds = lax.dot_general(v, ke_ref[n % 2], TN_DIMS, preferred_element_type=F32)
        for d in range(2):
            for j in range(cpb):
                lo = (d * cpb + j) * dk
                ds_ref[d, ch0 + j] = ds[:, lo:lo + dk]
        if q_ref is not None:
            acc_ref[r0:r0 + r, :] = _dot(a_ref[n % 2], v)

    decays(0)
    for n in range(len(blocks)):
        if n + 1 < len(blocks):
            decays(n + 1)
        scores(n)
        if n:
            products(n - 1)
    products(len(blocks) - 1)

    for d in range(2):
        def ctx_step(i, st):
            ch = (n_ctx - 1 - i) if d else i
            return st * dec_ref[d, ch] + ds_ref[d, ch]

        def lat_steps(i, st):
            blk = (n_lat // cpb - 1 - i) if d else i
            for jj in range(cpb):
                j = (cpb - 1 - jj) if d else jj
                st_ref[blk * cpb + j, :, d * dk:(d + 1) * dk] = st.astype(BF16)
                ch = n_ctx + blk * cpb + j
                st = st * dec_ref[d, ch] + ds_ref[d, ch]
            return st

        st = lax.fori_loop(0, n_ctx, ctx_step, jnp.zeros((HG_DV, dk), F32), unroll=True)
        lax.fori_loop(0, n_lat // cpb, lat_steps, st)

    def pass3(i, carry):
        for j in range(cpb):
            rows = pl.ds(pl.multiple_of(i * r + j * c, c), c)
            inter = lax.dot_general(qe_ref[rows, :], st_ref[i * cpb + j], NT_DIMS, preferred_element_type=F32)
            tot = acc_ref[rows, :] + inter
            o_ref[rows, :] = (_rms(tot, gain_ref[...]) * _silu(og_ref[rows, :])).astype(BF16)
        return carry

    lax.fori_loop(0, n_lat // cpb, pass3, 0, unroll=4)


def _hgrn(hg_gain, hq, hv, ff, og, cv, cff):
    bsz, t, _ = hq.shape
    tc = cv.shape[1]

    def col(tt, off=0):
        return pl.BlockSpec((None, tt, HG_DK), lambda b, h: (b, 0, h + off))

    n_lat, n_ctx = t // CHUNK, tc // CHUNK
    cpb = next(n for n in (4, 2, 1) if n_lat % n == 0 and n_ctx % n == 0)
    return pl.pallas_call(
        functools.partial(_hgrn_bidir_kernel, n_lat=n_lat, n_ctx=n_ctx, cpb=cpb),
        grid=(bsz, HG_HEADS),
        in_specs=[pl.BlockSpec((1, HG_DV), lambda b, h: (0, 0)),
                  col(t), col(t), col(t), col(t, HG_HEADS), col(t),
                  col(tc), col(tc), col(tc, HG_HEADS)],
        out_specs=col(t),
        out_shape=jax.ShapeDtypeStruct((bsz, t, HG_WIDTH), BF16),
        scratch_shapes=[pltpu.VMEM((t, HG_DV), F32),
                        pltpu.VMEM((t, 2 * HG_DK), BF16),
                        pltpu.VMEM((2, n_ctx + n_lat, HG_DV, HG_DK), F32),
                        pltpu.VMEM((2, n_ctx + n_lat, 1, HG_DK), F32),
                        pltpu.VMEM((n_lat, HG_DV, 2 * HG_DK), BF16),
                        pltpu.VMEM((2, 4, cpb * CHUNK, HG_DK), F32),
                        pltpu.VMEM((2, cpb * CHUNK, 2 * cpb * HG_DK), BF16),
                        pltpu.VMEM((2, cpb * CHUNK, cpb * CHUNK), BF16)],
        compiler_params=_params(2),
        name="hgrn",
    )(hg_gain[None, :], hq, hv, ff, ff, og, cv, cff, cff)


def _merge_ffn_kernel(x_ref, m_ref, oa_ref, oh_ref, mg_ref, gpost1_ref, wa_ref, wh_ref, wo_ref,
                      gpre2_ref, gpost2_ref, wg_hbm, wu_hbm, wd_hbm, o_ref,
                      wg_ref, wu_ref, wd_ref, stage_wide, stage_tall, sems, *, slot, d):
    _load_ffn_weights((wg_hbm, wu_hbm, wd_hbm), (wg_ref, wu_ref, wd_ref), stage_wide, stage_tall, sems, slot)
    gate = m_ref[:, 5 * d:6 * d]
    tiles, x2 = _sub_tiles(x_ref.shape[0]), []
    for rows in tiles:
        y = (_sigmoid(mg_ref[rows, :d]) * _dot(oa_ref[rows, :], wa_ref[...])
             + _sigmoid(mg_ref[rows, d:]) * _dot(oh_ref[rows, :], wh_ref[...]))
        z = _dot(y.astype(BF16), wo_ref[...])
        x2.append(x_ref[rows, :] + gate * _rms(z, gpost1_ref[...]))
    outs = _ffn_half_steps(x2, m_ref, gpre2_ref, gpost2_ref, wg_ref, wu_ref, wd_ref, 6, d)
    for rows, out in zip(tiles, outs):
        o_ref[rows, :] = out


def _merge_ffn(x, mods, o_att, o_hg, mg, g_post1, wa, wh, wo, g_pre2, g_post2, wg, wu, wd, slot, tm):
    bsz, t, d = x.shape
    nm = mods.shape[-1]
    hbm = pl.BlockSpec(memory_space=pl.ANY)

    def tile(width):
        return pl.BlockSpec((None, tm, width), lambda b, i: (b, i, 0))

    return pl.pallas_call(
        functools.partial(_merge_ffn_kernel, slot=slot, d=d),
        grid=(bsz, t // tm),
        in_specs=[tile(d), pl.BlockSpec((None, 1, nm), lambda b, i: (b, 0, 0)),
                  tile(ATT_WIDTH), tile(HG_WIDTH), tile(2 * d), _const_spec((1, d)),
                  _const_spec(wa.shape), _const_spec(wh.shape), _const_spec(wo.shape),
                  _const_spec((1, d)), _const_spec((1, d)), hbm, hbm, hbm],
        out_specs=tile(d),
        out_shape=jax.ShapeDtypeStruct((bsz, t, d), F32),
        scratch_shapes=_ffn_weight_scratch(d, wg.shape[-1]),
        compiler_params=_ordered_params(2),
        name="merge_ffn",
    )(x, mods, o_att, o_hg, mg, g_post1[None, :], wa, wh, wo, g_pre2[None, :], g_post2[None, :], wg, wu, wd)


def _rope_tables(t):
    pos = jnp.arange(t, dtype=jnp.int32)
    row = (pos // GRID_W).astype(F32)
    colp = (pos % GRID_W).astype(F32)
    inv_freq = ROPE_THETA ** (-jnp.arange(ROPE_PAIRS, dtype=F32) / ROPE_PAIRS)
    ang_r = row[:, None] * inv_freq
    ang_c = colp[:, None] * inv_freq
    ang = jnp.concatenate([ang_r, ang_r, ang_c, ang_c], axis=-1)
    cos, sin = jnp.cos(ang), jnp.sin(ang)
    first = (jnp.arange(HEAD_DIM) % (2 * ROPE_PAIRS)) < ROPE_PAIRS
    sin_lo = jnp.where(first, -sin, 0.0)
    sin_hi = jnp.where(first, 0.0, sin)
    two = lambda a: jnp.concatenate([a, a], axis=-1)
    return two(cos), two(sin_lo), two(sin_hi)


def _token_tiles(t):
    base = min(SUB_TILE, t)
    medium = 2 * SUB_TILE if t % (2 * SUB_TILE) == 0 else base
    large = 4 * SUB_TILE if t % (4 * SUB_TILE) == 0 else medium
    return base, medium, large


def kernel(x, c, ctx, c_ctx, w_mod, b_mod, norm_pre, norm_post, ffn_w_gate, ffn_w_up, ffn_w_down,
           w_in, q_norm, k_norm, hg_lower_bound, hg_norm, w_att_out, w_hg_out, w_o):
    assert w_in.shape[0] == 1, "single-layer block"
    bsz, t, d = x.shape
    tc = ctx.shape[1]
    assert t % GRID_W == 0 and t % CHUNK == 0 and tc % CHUNK == 0
    tm, tmf, tml = _token_tiles(t)
    _, tmc, tmc_large = _token_tiles(bsz * tc)

    rows = -(-(bsz + 1) // 8) * 8
    cvec = jnp.concatenate([c, c_ctx[None, :], jnp.zeros((rows - bsz - 1, d), c.dtype)], axis=0)
    mods = _modulation(cvec, w_mod[0], b_mod[0])[:, None, :]
    lat_row = lambda b: b
    ctx_row = lambda b: bsz

    wg, wu, wd = ffn_w_gate[0], ffn_w_up[0], ffn_w_down[0]
    w_in_b = w_in[0].astype(BF16)

    x1 = _ffn(x, mods, lat_row, 0, norm_pre[0, 0], norm_post[0, 0], wg, wu, wd, 0, tml)
    h1 = _ffn(ctx.reshape(1, bsz * tc, d), mods, ctx_row, 0, norm_pre[0, 0], norm_post[0, 0], wg, wu, wd, 0,
              tmc_large)

    cos, slo, shi = _rope_tables(t)
    q_gain2 = jnp.concatenate([q_norm[0], q_norm[0]])[None, :]
    k_gain2 = jnp.concatenate([k_norm[0], k_norm[0]])[None, :]
    slots = hg_lower_bound.shape[1]
    lb_raw = jnp.transpose(hg_lower_bound.astype(F32), (1, 0, 2)).reshape(slots, 2 * HG_WIDTH)
    q, k, v, hq, hv, ff, og, mg = _inproj_latent(x1, mods, norm_pre[0, 1], w_in_b, q_gain2, k_gain2, lb_raw,
                                                  cos, slo, shi, tmf)
    ck, cv, chv, cff = (a.reshape(bsz, tc, a.shape[-1]) for a in
                        _inproj_ctx(h1, mods, bsz, norm_pre[0, 1], w_in_b, k_gain2, lb_raw, tmc))

    o_att = _attention(q, k, ck, v, cv, tm, tml)

    o_hg = _hgrn(hg_norm[0], hq, hv, ff, og, chv, cff)

    return _merge_ffn(x1, mods, o_att, o_hg, mg, norm_post[0, 1], w_att_out[0].astype(BF16),
                      w_hg_out[0].astype(BF16), w_o[0].astype(BF16),
                      norm_pre[0, 2], norm_post[0, 2], wg, wu, wd, 1, tmf)
```
